```python
import math
import jax
import jax.numpy as jnp
from jax import lax
import numpy as np

D_MODEL = 2048
BATCH = 32
SEQ = 256
DEPTH = 4
DEC_BATCH = 4
DEC_SEQ = 4096
PAST_LEN = 512

GRID_W = 64
HEAD_DIM = 128
MIX_WIDTH = D_MODEL
A_Q_HEADS = 8
A_KV_HEADS = 2
A_GROUP = A_Q_HEADS // A_KV_HEADS
A_WINDOW = 128
A_BLOCK = 128
B_HEADS = 8
NA_ROWS = 8
NA_COLS = 16
C_HEADS = 8
C_V_DIM = 2 * HEAD_DIM
C_BLOCK = 128
N_EXPERTS = 16
EC_CAPACITY_FACTOR = 2
D_EXPERT = 2048
ROPE_BASE = 10000.0
LN_EPS = 1e-5
RMS_EPS = 1e-5
N_EVEN = (DEPTH + 1) // 2
N_ODD = DEPTH // 2
EVEN_IN = (A_Q_HEADS + 2 * A_KV_HEADS + 3 * B_HEADS) * HEAD_DIM
ODD_IN = 3 * C_HEADS * 2 * HEAD_DIM
EVEN_SPLITS = (A_Q_HEADS * HEAD_DIM, (A_Q_HEADS + A_KV_HEADS) * HEAD_DIM, (A_Q_HEADS + 2 * A_KV_HEADS) * HEAD_DIM, (A_Q_HEADS + 2 * A_KV_HEADS + B_HEADS) * HEAD_DIM, (A_Q_HEADS + 2 * A_KV_HEADS + 2 * B_HEADS) * HEAD_DIM)
DEEPNORM_ALPHA = (2 * DEPTH) ** 0.25
DEEPNORM_BETA = (8 * DEPTH) ** -0.25
ATTN_SCALE = HEAD_DIM ** -0.5

kernel_name = 'hybrid_diffusion_prefix_trunk_step'


def layer_norm(x, g, b):
    xf = x.astype(jnp.float32)
    mu = jnp.mean(xf, -1, keepdims=True)
    var = jnp.mean(jnp.square(xf - mu), -1, keepdims=True)
    return ((xf - mu) * lax.rsqrt(var + LN_EPS)).astype(x.dtype) * g + b


def adaln_params(cond, w, b):
    m = jax.nn.silu(cond) @ w + b
    return m.reshape(m.shape[0], 6, 1, D_MODEL)


def modulate(x, shift, scale):
    return x * (1 + scale) + shift


def axial_rope_tables(n_tokens, dtype):
    t = jnp.arange(n_tokens)
    row = (t // GRID_W).astype(jnp.float32)
    col = (t % GRID_W).astype(jnp.float32)
    axis_dim = HEAD_DIM // 2
    inv_freq = ROPE_BASE ** (-jnp.arange(0, axis_dim, 2, dtype=jnp.float32) / axis_dim)
    ang_r = row[:, None, None] * inv_freq
    ang_c = col[:, None, None] * inv_freq
    return (jnp.cos(ang_r).astype(dtype), jnp.sin(ang_r).astype(dtype), jnp.cos(ang_c).astype(dtype), jnp.sin(ang_c).astype(dtype))


def _rotate(x, cos, sin):
    x1, x2 = jnp.split(x, 2, axis=-1)
    return jnp.concatenate([x1 * cos - x2 * sin, x2 * cos + x1 * sin], axis=-1)


def apply_axial_rope(x, rope):
    cos_r, sin_r, cos_c, sin_c = rope
    xr, xc = jnp.split(x, 2, axis=-1)
    return jnp.concatenate([_rotate(xr, cos_r, sin_r), _rotate(xc, cos_c, sin_c)], axis=-1)


def sink_softmax(s, sink):
    m = jnp.maximum(jnp.max(s, -1, keepdims=True), sink)
    e = jnp.exp(s - m)
    return e / (jnp.sum(e, -1, keepdims=True) + jnp.exp(sink - m))


def gqa_sink_dense(q, k, v, sink):
    s = jnp.einsum('nqkgd,nskd->nkgqs', q, k, preferred_element_type=jnp.float32) * ATTN_SCALE
    p = sink_softmax(s, sink.astype(jnp.float32)[None, :, :, None, None]).astype(v.dtype)
    return jnp.einsum('nkgqs,nskd->nqkgd', p, v)


def mha_dense(q, k, v):
    s = jnp.einsum('nqhd,nkhd->nhqk', q, k, preferred_element_type=jnp.float32) * ATTN_SCALE
    p = jax.nn.softmax(s, axis=-1).astype(v.dtype)
    return jnp.einsum('nhqk,nkhd->nqhd', p, v)


def window_gqa_sink(q, k, v, k_ctx, v_ctx, sink):
    N, T = q.shape[0], q.shape[1]
    nb = T // A_BLOCK
    qb = q.reshape(N, nb, A_BLOCK, A_KV_HEADS, A_GROUP, HEAD_DIM)

    def band(x):
        xp = jnp.pad(x, [(0, 0), (A_BLOCK, A_BLOCK), (0, 0), (0, 0)])
        xp = xp.reshape(N, nb + 2, A_BLOCK, A_KV_HEADS, HEAD_DIM)
        return jnp.concatenate([xp[:, :-2], xp[:, 1:-1], xp[:, 2:]], axis=2)

    kb, vb = band(k), band(v)
    s_loc = jnp.einsum('nbqkgd,nbskd->nbkgqs', qb, kb, preferred_element_type=jnp.float32) * ATTN_SCALE
    q_off = jnp.arange(A_BLOCK)[:, None]
    k_off = jnp.arange(3 * A_BLOCK)[None, :] - A_BLOCK
    k_abs = jnp.arange(nb)[:, None, None] * A_BLOCK + k_off[None]
    valid = (jnp.abs(k_off - q_off) <= A_WINDOW)[None] & (k_abs >= 0) & (k_abs < T)
    s_loc = jnp.where(valid[None, :, None, None], s_loc, -jnp.inf)
    s_ctx = jnp.einsum('nbqkgd,nskd->nbkgqs', qb, k_ctx, preferred_element_type=jnp.float32) * ATTN_SCALE
    p = sink_softmax(jnp.concatenate([s_loc, s_ctx], axis=-1), sink.astype(jnp.float32)[None, None, :, :, None, None]).astype(v.dtype)
    n_loc = 3 * A_BLOCK
    o = jnp.einsum('nbkgqs,nbskd->nbqkgd', p[..., :n_loc], vb) + jnp.einsum('nbkgqs,nskd->nbqkgd', p[..., n_loc:], v_ctx)
    return o.reshape(N, T, A_Q_HEADS * HEAD_DIM)


def neighborhood_attn(q, k, v, k_ctx, v_ctx, rpb):
    N, T = q.shape[0], q.shape[1]
    rows = T // GRID_W
    kh = min(NA_ROWS, rows)
    qg = q.reshape(N, rows, GRID_W, B_HEADS, HEAD_DIM)
    kg = k.reshape(N, rows, GRID_W, B_HEADS, HEAD_DIM)
    vg = v.reshape(N, rows, GRID_W, B_HEADS, HEAD_DIM)
    r = jnp.arange(rows)
    r0 = jnp.clip(r - kh // 2, 0, rows - kh)
    row_idx = r0[:, None] + jnp.arange(kh)[None, :]
    ks = kg[:, row_idx]
    vs = vg[:, row_idx]
    s_loc = jnp.einsum('nrqhd,nrowhd->nrhqow', qg, ks, preferred_element_type=jnp.float32) * ATTN_SCALE
    cq = jnp.arange(GRID_W)
    c0 = jnp.clip(cq - NA_COLS // 2, 0, GRID_W - NA_COLS)
    col_ok = (cq[None, :] >= c0[:, None]) & (cq[None, :] < c0[:, None] + NA_COLS)
    dr = row_idx - r[:, None] + (NA_ROWS - 1)
    dc = jnp.clip(cq[None, :] - cq[:, None] + (NA_COLS - 1), 0, 2 * NA_COLS - 2)
    bias = rpb[:, dr[:, None, :, None], dc[None, :, None, :]]
    s_loc = s_loc + jnp.moveaxis(bias, 0, 1).astype(jnp.float32)[None]
    s_loc = jnp.where(col_ok[:, None, :], s_loc, -jnp.inf)
    n_loc = kh * GRID_W
    s_loc = s_loc.reshape(N, rows, B_HEADS, GRID_W, n_loc)
    s_ctx = jnp.einsum('nrqhd,nshd->nrhqs', qg, k_ctx, preferred_element_type=jnp.float32) * ATTN_SCALE
    p = jax.nn.softmax(jnp.concatenate([s_loc, s_ctx], axis=-1), axis=-1).astype(v.dtype)
    p_loc = p[..., :n_loc].reshape(N, rows, B_HEADS, GRID_W, kh, GRID_W)
    o = jnp.einsum('nrhqow,nrowhd->nrqhd', p_loc, vs) + jnp.einsum('nrhqs,nshd->nrqhd', p[..., n_loc:], v_ctx)
    return o.reshape(N, T, B_HEADS * HEAD_DIM)


def diff_lambda_value(lam_p, lambda_init):
    lp = lam_p.astype(jnp.float32)
    return jnp.exp(jnp.sum(lp[0] * lp[1])) - jnp.exp(jnp.sum(lp[2] * lp[3])) + lambda_init


def diff_attend(q, k, v, lam):
    s = jnp.einsum('nqhid,nkhid->nhiqk', q, k, preferred_element_type=jnp.float32) * ATTN_SCALE
    p = jax.nn.softmax(s, axis=-1)
    a = (p[:, :, 0] - lam * p[:, :, 1]).astype(v.dtype)
    return jnp.einsum('nhqk,nkhe->nqhe', a, v)


def diff_head_norm(o, g, lambda_init):
    of = o.astype(jnp.float32)
    of = of * lax.rsqrt(jnp.mean(of * of, -1, keepdims=True) + RMS_EPS)
    return (of.astype(o.dtype) * g) * (1.0 - lambda_init)


def expert_choice_ffn(h, w_router, w_gate, w_up, w_down):
    N, T, D = h.shape
    cap = EC_CAPACITY_FACTOR * T // N_EXPERTS
    aff = jax.nn.softmax(jnp.einsum('ntd,de->nte', h, w_router, preferred_element_type=jnp.float32), axis=-1)
    gate, idx = lax.top_k(jnp.swapaxes(aff, 1, 2), cap)
    xs = jax.vmap(lambda x, i: x[i])(h, idx)
    a = jnp.einsum('necd,edf->necf', xs, w_gate)
    u = jnp.einsum('necd,edf->necf', xs, w_up)
    y = jnp.einsum('necf,efd->necd', jax.nn.silu(a) * u, w_down) * gate[..., None].astype(h.dtype)
    return jax.vmap(lambda yy, i: jnp.zeros((T, D), yy.dtype).at[i.reshape(-1)].add(yy.reshape(-1, D)))(y, idx)


def even_ctx(h, w_in, w_out, sink):
    N, L, _ = h.shape
    qa, ka, va, qb, kb, vb = jnp.split(h @ w_in, EVEN_SPLITS, axis=-1)
    qa = qa.reshape(N, L, A_KV_HEADS, A_GROUP, HEAD_DIM)
    ka = ka.reshape(N, L, A_KV_HEADS, HEAD_DIM)
    va = va.reshape(N, L, A_KV_HEADS, HEAD_DIM)
    qb = qb.reshape(N, L, B_HEADS, HEAD_DIM)
    kb = kb.reshape(N, L, B_HEADS, HEAD_DIM)
    vb = vb.reshape(N, L, B_HEADS, HEAD_DIM)
    oa = gqa_sink_dense(qa, ka, va, sink).reshape(N, L, A_Q_HEADS * HEAD_DIM)
    ob = mha_dense(qb, kb, vb).reshape(N, L, B_HEADS * HEAD_DIM)
    return (jnp.concatenate([oa, ob], axis=-1) @ w_out, ka, va, kb, vb)


def even_lat(h, ka_c, va_c, kb_c, vb_c, w_in, w_out, sink, rpb, rope):
    N, T, _ = h.shape
    qa, ka, va, qb, kb, vb = jnp.split(h @ w_in, EVEN_SPLITS, axis=-1)
    qa = apply_axial_rope(qa.reshape(N, T, A_Q_HEADS, HEAD_DIM), rope).reshape(N, T, A_KV_HEADS, A_GROUP, HEAD_DIM)
    ka = apply_axial_rope(ka.reshape(N, T, A_KV_HEADS, HEAD_DIM), rope)
    va = va.reshape(N, T, A_KV_HEADS, HEAD_DIM)
    oa = window_gqa_sink(qa, ka, va, ka_c, va_c, sink)
    ob = neighborhood_attn(qb.reshape(N, T, B_HEADS, HEAD_DIM), kb.reshape(N, T, B_HEADS, HEAD_DIM), vb.reshape(N, T, B_HEADS, HEAD_DIM), kb_c, vb_c, rpb)
    return jnp.concatenate([oa, ob], axis=-1) @ w_out


def odd_ctx(h, w_in, w_out, lam_p, subln, lambda_init):
    N, L, _ = h.shape
    q, k, v = jnp.split(h @ w_in, 3, axis=-1)
    q = q.reshape(N, L, C_HEADS, 2, HEAD_DIM)
    k = k.reshape(N, L, C_HEADS, 2, HEAD_DIM)
    v = v.reshape(N, L, C_HEADS, C_V_DIM)
    lam = diff_lambda_value(lam_p, lambda_init)
    o = diff_head_norm(diff_attend(q, k, v, lam), subln, lambda_init)
    return (o.reshape(N, L, MIX_WIDTH) @ w_out, k, v)


def odd_lat(h, k_c, v_c, w_in, w_out, lam_p, subln, lambda_init, rope):
    N, T, _ = h.shape
    q, k, v = jnp.split(h @ w_in, 3, axis=-1)
    q = apply_axial_rope(q.reshape(N, T, 2 * C_HEADS, HEAD_DIM), rope).reshape(N, T, C_HEADS, 2, HEAD_DIM)
    k = apply_axial_rope(k.reshape(N, T, 2 * C_HEADS, HEAD_DIM), rope).reshape(N, T, C_HEADS, 2, HEAD_DIM)
    v = v.reshape(N, T, C_HEADS, C_V_DIM)
    k_all = jnp.concatenate([k, k_c], axis=1)
    v_all = jnp.concatenate([v, v_c], axis=1)
    lam = diff_lambda_value(lam_p, lambda_init)
    nb = T // C_BLOCK
    qb = jnp.moveaxis(q.reshape(N, nb, C_BLOCK, C_HEADS, 2, HEAD_DIM), 1, 0)
    o = lax.map(lambda qq: diff_attend(qq, k_all, v_all, lam), qb)
    o = jnp.moveaxis(o, 0, 1).reshape(N, T, C_HEADS, C_V_DIM)
    o = diff_head_norm(o, subln, lambda_init)
    return o.reshape(N, T, MIX_WIDTH) @ w_out


def setup_inputs(seed: int = 0) -> dict:
    key = jax.random.key(seed)
    ks = jax.random.split(key, 26)

    def nrm(k, shape, scale):
        return jax.random.normal(k, shape, jnp.float32) * scale

    return {
        'x_prompt': nrm(ks[0], (BATCH, SEQ, D_MODEL), 1.0),
        'x_sample': nrm(ks[1], (DEC_BATCH, DEC_SEQ, D_MODEL), 1.0),
        'cache_a_k': nrm(ks[2], (DEC_BATCH, N_EVEN, PAST_LEN, A_KV_HEADS, HEAD_DIM), 1.0),
        'cache_a_v': nrm(ks[3], (DEC_BATCH, N_EVEN, PAST_LEN, A_KV_HEADS, HEAD_DIM), 1.0),
        'cache_b_k': nrm(ks[4], (DEC_BATCH, N_EVEN, PAST_LEN, B_HEADS, HEAD_DIM), 1.0),
        'cache_b_v': nrm(ks[5], (DEC_BATCH, N_EVEN, PAST_LEN, B_HEADS, HEAD_DIM), 1.0),
        'cache_c_k': nrm(ks[6], (DEC_BATCH, N_ODD, PAST_LEN, C_HEADS, 2, HEAD_DIM), 1.0),
        'cache_c_v': nrm(ks[7], (DEC_BATCH, N_ODD, PAST_LEN, C_HEADS, C_V_DIM), 1.0),
        'c': nrm(ks[8], (DEC_BATCH, D_MODEL), 1.0),
        'c_ctx': nrm(ks[9], (D_MODEL,), 1.0),
        'w_mod': nrm(ks[10], (DEPTH, D_MODEL, 6 * D_MODEL), 0.5 * D_MODEL ** -0.5),
        'b_mod': nrm(ks[11], (DEPTH, 6 * D_MODEL), 0.02),
        'w_in_even': nrm(ks[12], (N_EVEN, D_MODEL, EVEN_IN), D_MODEL ** -0.5),
        'w_out_even': nrm(ks[13], (N_EVEN, MIX_WIDTH, D_MODEL), DEEPNORM_BETA * MIX_WIDTH ** -0.5),
        'a_sink': nrm(ks[14], (N_EVEN, A_KV_HEADS, A_GROUP), 1.0),
        'na_rpb': nrm(ks[15], (N_EVEN, B_HEADS, 2 * NA_ROWS - 1, 2 * NA_COLS - 1), 0.1),
        'w_in_odd': nrm(ks[16], (N_ODD, D_MODEL, ODD_IN), D_MODEL ** -0.5),
        'w_out_odd': nrm(ks[17], (N_ODD, MIX_WIDTH, D_MODEL), DEEPNORM_BETA * MIX_WIDTH ** -0.5),
        'diff_lambda': nrm(ks[18], (N_ODD, 4, HEAD_DIM), 0.1),
        'diff_subln': 1.0 + nrm(ks[19], (N_ODD, C_V_DIM), 0.02),
        'ln_g': 1.0 + nrm(ks[20], (DEPTH, 2, D_MODEL), 0.02),
        'ln_b': nrm(ks[21], (DEPTH, 2, D_MODEL), 0.02),
        'w_router': nrm(ks[22], (DEPTH, D_MODEL, N_EXPERTS), D_MODEL ** -0.5),
        'w_gate': nrm(ks[23], (DEPTH, N_EXPERTS, D_MODEL, D_EXPERT), D_MODEL ** -0.5),
        'w_up': nrm(ks[24], (DEPTH, N_EXPERTS, D_MODEL, D_EXPERT), D_MODEL ** -0.5),
        'w_down': nrm(ks[25], (DEPTH, N_EXPERTS, D_EXPERT, D_MODEL), DEEPNORM_BETA * D_EXPERT ** -0.5),
    }


def reference(x_prompt, x_sample, cache_a_k, cache_a_v, cache_b_k, cache_b_v, cache_c_k, cache_c_v, c, c_ctx, w_mod, b_mod, w_in_even, w_out_even, a_sink, na_rpb, w_in_odd, w_out_odd, diff_lambda, diff_subln, ln_g, ln_b, w_router, w_gate, w_up, w_down):
    rope = axial_rope_tables(x_sample.shape[1], x_sample.dtype)
    xp, xs = x_prompt, x_sample
    new_a_k, new_a_v, new_b_k, new_b_v, new_c_k, new_c_v = [], [], [], [], [], []
    for l in range(DEPTH):
        m_p = adaln_params(c_ctx[None], w_mod[l], b_mod[l])
        m_s = adaln_params(c, w_mod[l], b_mod[l])
        hp = modulate(xp, m_p[:, 0], m_p[:, 1])
        hs = modulate(xs, m_s[:, 0], m_s[:, 1])
        if l % 2 == 0:
            e = l // 2
            yp, ka, va, kb, vb = even_ctx(hp, w_in_even[e], w_out_even[e], a_sink[e])
            new_a_k.append(ka)
            new_a_v.append(va)
            new_b_k.append(kb)
            new_b_v.append(vb)
            ys = even_lat(hs, cache_a_k[:, e], cache_a_v[:, e], cache_b_k[:, e], cache_b_v[:, e], w_in_even[e], w_out_even[e], a_sink[e], na_rpb[e], rope)
        else:
            o = l // 2
            lambda_init = 0.8 - 0.6 * math.exp(-0.3 * l)
            yp, kc, vc = odd_ctx(hp, w_in_odd[o], w_out_odd[o], diff_lambda[o], diff_subln[o], lambda_init)
            new_c_k.append(kc)
            new_c_v.append(vc)
            ys = odd_lat(hs, cache_c_k[:, o], cache_c_v[:, o], w_in_odd[o], w_out_odd[o], diff_lambda[o], diff_subln[o], lambda_init, rope)
        xp = layer_norm(DEEPNORM_ALPHA * xp + m_p[:, 2] * yp, ln_g[l, 0], ln_b[l, 0])
        xs = layer_norm(DEEPNORM_ALPHA * xs + m_s[:, 2] * ys, ln_g[l, 0], ln_b[l, 0])
        hp = modulate(xp, m_p[:, 3], m_p[:, 4])
        hs = modulate(xs, m_s[:, 3], m_s[:, 4])
        fp = expert_choice_ffn(hp, w_router[l], w_gate[l], w_up[l], w_down[l])
        fs = expert_choice_ffn(hs, w_router[l], w_gate[l], w_up[l], w_down[l])
        xp = layer_norm(DEEPNORM_ALPHA * xp + m_p[:, 5] * fp, ln_g[l, 1], ln_b[l, 1])
        xs = layer_norm(DEEPNORM_ALPHA * xs + m_s[:, 5] * fs, ln_g[l, 1], ln_b[l, 1])
    return (xp, xs, jnp.stack(new_a_k, axis=1), jnp.stack(new_a_v, axis=1), jnp.stack(new_b_k, axis=1), jnp.stack(new_b_v, axis=1), jnp.stack(new_c_k, axis=1), jnp.stack(new_c_v, axis=1))
```

```python
import functools
import math

import jax
import jax.numpy as jnp
from jax import lax
from jax.experimental import pallas as pl
from jax.experimental.pallas import tpu as pltpu

BF = jnp.bfloat16
F32 = jnp.float32

GRID_W = 64
HEAD_DIM = 128
A_Q_HEADS = 8
A_KV_HEADS = 2
A_GROUP = A_Q_HEADS // A_KV_HEADS
A_BLOCK = 128
B_HEADS = 8
NA_ROWS = 8
NA_COLS = 16
C_HEADS = 8
C_V_DIM = 2 * HEAD_DIM
N_EXPERTS = 16
EC_CAPACITY_FACTOR = 2
ROPE_BASE = 10000.0
LN_EPS = 1e-5
RMS_EPS = 1e-5
ATTN_SCALE = HEAD_DIM ** -0.5
A_WIDTH = (A_Q_HEADS + 2 * A_KV_HEADS) * HEAD_DIM
B_WIDTH = B_HEADS * HEAD_DIM

VMEM_LIMIT_BYTES = 52 * 1024 * 1024
TOKEN_BLOCK = 256
SLOT_CHUNK = 128


def _cparams(*sem):
    return pltpu.CompilerParams(dimension_semantics=sem, vmem_limit_bytes=VMEM_LIMIT_BYTES)


def _nt_dot(a, b):
    return lax.dot_general(a, b, (((1,), (1,)), ((), ())), preferred_element_type=F32)


def _dot(a, b):
    return jnp.dot(a, b, preferred_element_type=F32)


def _layer_norm(z, g, b):
    mu = jnp.mean(z, -1, keepdims=True)
    d = z - mu
    var = jnp.mean(d * d, -1, keepdims=True)
    return d * lax.rsqrt(var + LN_EPS) * g + b


def _adaln_kernel(c_ref, w_ref, b_ref, o_ref):
    c = c_ref[...]
    s = (c * jax.nn.sigmoid(c)).astype(BF)
    o_ref[...] = _dot(s, w_ref[...].astype(BF)) + b_ref[...]


def adaln_all(cond, w_mod, b_mod):
    L, D, D6 = w_mod.shape
    R = cond.shape[0]
    tn = D // 2
    return pl.pallas_call(
        _adaln_kernel,
        grid=(L, D6 // tn),
        in_specs=[
            pl.BlockSpec((R, D), lambda l, j: (0, 0)),
            pl.BlockSpec((None, D, tn), lambda l, j: (l, 0, j)),
            pl.BlockSpec((None, 1, tn), lambda l, j: (l, 0, j)),
        ],
        out_specs=pl.BlockSpec((None, R, tn), lambda l, j: (l, 0, j)),
        out_shape=jax.ShapeDtypeStruct((L, R, D6), F32),
        compiler_params=_cparams("parallel", "parallel"),
    )(cond, w_mod, b_mod.reshape(L, 1, D6))


def _modulate_kernel(x_ref, m_ref, o_ref):
    o_ref[...] = (x_ref[...] * (1.0 + m_ref[1:2, :]) + m_ref[0:1, :]).astype(o_ref.dtype)


def modulate_rows(x, mods, layer, row0, per_req, T):
    M, D = x.shape
    tt = min(T, 512)
    return pl.pallas_call(
        _modulate_kernel,
        grid=(M // tt,),
        in_specs=[
            pl.BlockSpec((tt, D), lambda i: (i, 0)),
            pl.BlockSpec((None, None, 6, D), lambda i: (layer, row0 + ((i * tt) // T) * per_req, 0, 0)),
        ],
        out_specs=pl.BlockSpec((tt, D), lambda i: (i, 0)),
        out_shape=jax.ShapeDtypeStruct((M, D), BF),
        compiler_params=_cparams("parallel"),
    )(x, mods)


def _swap32(x):
    lane = lax.broadcasted_iota(jnp.int32, (1, HEAD_DIM), 1)
    return jnp.where((lane % 64) < 32, pltpu.roll(x, HEAD_DIM - 32, 1), pltpu.roll(x, 32, 1))


def _proj_kernel(*refs, n_rope_heads, heads_per_tile):
    if n_rope_heads:
        h_ref, w_ref, cos_ref, sin_ref, o_ref, wbf = refs
    else:
        h_ref, w_ref, o_ref, wbf = refs
    j = pl.program_id(0)
    i = pl.program_id(1)

    @pl.when(i == 0)
    def _():
        wbf[...] = w_ref[...].astype(BF)

    acc = _dot(h_ref[...], wbf[...])
    for hh in range(heads_per_tile):
        sl = slice(hh * HEAD_DIM, (hh + 1) * HEAD_DIM)
        x = acc[:, sl]
        if n_rope_heads == 0:
            o_ref[:, sl] = x.astype(o_ref.dtype)
        else:
            head = j * heads_per_tile + hh

            @pl.when(head < n_rope_heads)
            def _():
                o_ref[:, sl] = (x * cos_ref[...] + _swap32(x) * sin_ref[...]).astype(o_ref.dtype)

            @pl.when(head >= n_rope_heads)
            def _():
                o_ref[:, sl] = x.astype(o_ref.dtype)


def project(h, w, widx, col0, width, n_rope_heads, rope, T, out_dtype):
    M, D = h.shape
    heads = width // HEAD_DIM
    hpt = max(k for k in range(1, 9) if heads % k == 0 and col0 % (k * HEAD_DIM) == 0)
    tn = hpt * HEAD_DIM
    tm = min(T if n_rope_heads else M, 1024)
    tpb = max(T // tm, 1)
    in_specs = [
        pl.BlockSpec((tm, D), lambda j, i: (i, 0)),
        pl.BlockSpec((None, D, tn), lambda j, i: (widx, 0, col0 // tn + j)),
    ]
    args = [h, w]
    if n_rope_heads:
        in_specs += [pl.BlockSpec((tm, HEAD_DIM), lambda j, i: (i % tpb, 0))] * 2
        args += list(rope)
    return pl.pallas_call(
        functools.partial(_proj_kernel, n_rope_heads=n_rope_heads, heads_per_tile=hpt),
        grid=(width // tn, M // tm),
        in_specs=in_specs,
        out_specs=pl.BlockSpec((tm, tn), lambda j, i: (i, j)),
        out_shape=jax.ShapeDtypeStruct((M, width), out_dtype),
        scratch_shapes=[pltpu.VMEM((D, tn), BF)],
        compiler_params=_cparams("parallel", "arbitrary"),
    )(*args)


def rope_tables(T):
    t = jnp.arange(T)
    row = (t // GRID_W).astype(F32)
    col = (t % GRID_W).astype(F32)
    axis_dim = HEAD_DIM // 2
    inv_freq = ROPE_BASE ** (-jnp.arange(0, axis_dim, 2, dtype=F32) / axis_dim)
    ar = row[:, None] * inv_freq
    ac = col[:, None] * inv_freq
    cos = jnp.concatenate([jnp.cos(ar), jnp.cos(ar), jnp.cos(ac), jnp.cos(ac)], -1)
    sin = jnp.concatenate([-jnp.sin(ar), jnp.sin(ar), -jnp.sin(ac), jnp.sin(ac)], -1)
    return cos, sin


def _softmax_parts(parts, sink=None):
    m = None
    for s in parts:
        mi = jnp.max(s, -1, keepdims=True)
        m = mi if m is None else jnp.maximum(m, mi)
    if sink is not None:
        m = jnp.maximum(m, sink)
    es = [jnp.exp(s - m) for s in parts]
    den = None
    for e in es:
        di = jnp.sum(e, -1, keepdims=True)
        den = di if den is None else den + di
    if sink is not None:
        den = den + jnp.exp(sink - m)
    inv = 1.0 / den
    return [e * inv for e in es]


def _ctx_even_kernel(sink_ref, a_ref, qb_ref, kb_ref, vb_ref, o_ref, *, sink_base):
    T = a_ref.shape[0]
    for kv in range(A_KV_HEADS):
        k = a_ref[:, (A_Q_HEADS + kv) * HEAD_DIM:(A_Q_HEADS + kv + 1) * HEAD_DIM].astype(BF)
        v = a_ref[:, (A_Q_HEADS + A_KV_HEADS + kv) * HEAD_DIM:(A_Q_HEADS + A_KV_HEADS + kv + 1) * HEAD_DIM].astype(BF)
        for g in range(A_GROUP):
            hq = kv * A_GROUP + g
            q = a_ref[:, hq * HEAD_DIM:(hq + 1) * HEAD_DIM].astype(BF)
            s = _nt_dot(q, k) * ATTN_SCALE
            sink = jnp.full((T, 1), sink_ref[sink_base + hq], F32)
            (p,) = _softmax_parts([s], sink)
            o_ref[:, hq * HEAD_DIM:(hq + 1) * HEAD_DIM] = _dot(p.astype(BF), v).astype(o_ref.dtype)
    for hb in range(B_HEADS):
        sl = slice(hb * HEAD_DIM, (hb + 1) * HEAD_DIM)
        s = _nt_dot(qb_ref[:, sl].astype(BF), kb_ref[:, sl].astype(BF)) * ATTN_SCALE
        (p,) = _softmax_parts([s])
        o_ref[:, A_Q_HEADS * HEAD_DIM + hb * HEAD_DIM:A_Q_HEADS * HEAD_DIM + (hb + 1) * HEAD_DIM] = _dot(
            p.astype(BF), vb_ref[:, sl].astype(BF)).astype(o_ref.dtype)


def ctx_even_attention(arr_a, qb, kb, vb, sinks, e, N, T):
    M = N * T
    return pl.pallas_call(
        functools.partial(_ctx_even_kernel, sink_base=e * A_Q_HEADS),
        grid=(N,),
        in_specs=[
            pl.BlockSpec(memory_space=pltpu.SMEM),
            pl.BlockSpec((T, A_WIDTH), lambda n: (n, 0)),
            pl.BlockSpec((T, B_WIDTH), lambda n: (n, 0)),
            pl.BlockSpec((T, B_WIDTH), lambda n: (n, 0)),
            pl.BlockSpec((T, B_WIDTH), lambda n: (n, 0)),
        ],
        out_specs=pl.BlockSpec((T, A_Q_HEADS * HEAD_DIM + B_WIDTH), lambda n: (n, 0)),
        out_shape=jax.ShapeDtypeStruct((M, A_Q_HEADS * HEAD_DIM + B_WIDTH), BF),
        compiler_params=_cparams("parallel"),
    )(sinks, arr_a, qb, kb, vb)


def _win_kernel(sink_ref, q_ref, kvp_ref, kvc_ref, kvn_ref, kc_ref, vc_ref, o_ref, *, sink_base, nb):
    i = pl.program_id(1)
    R = A_GROUP * A_BLOCK
    qq = lax.broadcasted_iota(jnp.int32, (R, 1), 0) % A_BLOCK
    kk = lax.broadcasted_iota(jnp.int32, (1, A_BLOCK), 1)
    ok_prev = jnp.logical_and(kk >= qq, i > 0)
    ok_next = jnp.logical_and(kk <= qq, i < nb - 1)
    for kv in range(A_KV_HEADS):
        ksl = slice(kv * HEAD_DIM, (kv + 1) * HEAD_DIM)
        vsl = slice((A_KV_HEADS + kv) * HEAD_DIM, (A_KV_HEADS + kv + 1) * HEAD_DIM)
        q = jnp.concatenate(
            [q_ref[:, (kv * A_GROUP + g) * HEAD_DIM:(kv * A_GROUP + g + 1) * HEAD_DIM] for g in range(A_GROUP)], axis=0)
        sink = jnp.concatenate(
            [jnp.full((A_BLOCK, 1), sink_ref[sink_base + kv * A_GROUP + g], F32) for g in range(A_GROUP)], axis=0)
        s_p = jnp.where(ok_prev, _nt_dot(q, kvp_ref[:, ksl]) * ATTN_SCALE, -jnp.inf)
        s_c = _nt_dot(q, kvc_ref[:, ksl]) * ATTN_SCALE
        s_n = jnp.where(ok_next, _nt_dot(q, kvn_ref[:, ksl]) * ATTN_SCALE, -jnp.inf)
        s_x = _nt_dot(q, kc_ref[:, ksl].astype(BF)) * ATTN_SCALE
        p_p, p_c, p_n, p_x = _softmax_parts([s_p, s_c, s_n, s_x], sink)
        o = (_dot(p_p.astype(BF), kvp_ref[:, vsl]) + _dot(p_c.astype(BF), kvc_ref[:, vsl])
             + _dot(p_n.astype(BF), kvn_ref[:, vsl]) + _dot(p_x.astype(BF), vc_ref[:, ksl].astype(BF)))
        for g in range(A_GROUP):
            hq = kv * A_GROUP + g
            o_ref[:, hq * HEAD_DIM:(hq + 1) * HEAD_DIM] = o[g * A_BLOCK:(g + 1) * A_BLOCK].astype(o_ref.dtype)


def window_attention(arr_a, cache_k, cache_v, sinks, e, N, T):
    M = N * T
    nb = T // A_BLOCK
    L = cache_k.shape[2]
    QW = A_Q_HEADS * HEAD_DIM
    KVW = 2 * A_KV_HEADS * HEAD_DIM
    kvblk = QW // KVW
    return pl.pallas_call(
        functools.partial(_win_kernel, sink_base=e * A_Q_HEADS, nb=nb),
        grid=(N, nb),
        in_specs=[
            pl.BlockSpec(memory_space=pltpu.SMEM),
            pl.BlockSpec((A_BLOCK, QW), lambda n, i: (n * nb + i, 0)),
            pl.BlockSpec((A_BLOCK, KVW), lambda n, i: (n * nb + jnp.maximum(i - 1, 0), kvblk)),
            pl.BlockSpec((A_BLOCK, KVW), lambda n, i: (n * nb + i, kvblk)),
            pl.BlockSpec((A_BLOCK, KVW), lambda n, i: (n * nb + jnp.minimum(i + 1, nb - 1), kvblk)),
            pl.BlockSpec((None, None, L, A_KV_HEADS * HEAD_DIM), lambda n, i: (n, e, 0, 0)),
            pl.BlockSpec((None, None, L, A_KV_HEADS * HEAD_DIM), lambda n, i: (n, e, 0, 0)),
        ],
        out_specs=pl.BlockSpec((A_BLOCK, QW), lambda n, i: (n * nb + i, 0)),
        out_shape=jax.ShapeDtypeStruct((M, QW), BF),
        compiler_params=_cparams("parallel", "parallel"),
    )(sinks, arr_a, arr_a, arr_a, arr_a, cache_k, cache_v)


NB_Q_ROWS = TOKEN_BLOCK // GRID_W
NB_WIN_BLOCKS = 3


def neighbourhood_bias(rpb, T):
    rows = T // GRID_W
    nblk = T // TOKEN_BLOCK
    qq = jnp.arange(TOKEN_BLOCK)
    kk = jnp.arange(NB_WIN_BLOCKS * TOKEN_BLOCK)
    out = []
    for blk in (0, 1, nblk - 1):
        w0 = min(max(blk - 1, 0), nblk - NB_WIN_BLOCKS) * NB_Q_ROWS
        rq = blk * NB_Q_ROWS + qq // GRID_W
        cq = qq % GRID_W
        rk = w0 + kk // GRID_W
        ck = kk % GRID_W
        r0 = jnp.clip(rq - NA_ROWS // 2, 0, rows - NA_ROWS)
        c0 = jnp.clip(cq - NA_COLS // 2, 0, GRID_W - NA_COLS)
        ok = ((rk[None, :] >= r0[:, None]) & (rk[None, :] < r0[:, None] + NA_ROWS)
              & (ck[None, :] >= c0[:, None]) & (ck[None, :] < c0[:, None] + NA_COLS))
        dr = jnp.clip(rk[None, :] - rq[:, None] + (NA_ROWS - 1), 0, 2 * NA_ROWS - 2)
        dc = jnp.clip(ck[None, :] - cq[:, None] + (NA_COLS - 1), 0, 2 * NA_COLS - 2)
        out.append(jnp.where(ok[None], rpb[:, dr, dc].astype(F32), -jnp.inf))
    return jnp.stack(out, axis=1)


def _nbr_kernel(q_ref, k0_ref, k1_ref, k2_ref, v0_ref, v1_ref, v2_ref, kc_ref, vc_ref, bias_ref, o_ref):
    for h in range(B_HEADS):
        sl = slice(h * HEAD_DIM, (h + 1) * HEAD_DIM)
        q = q_ref[:, sl]
        parts = []
        for w, k_ref in enumerate((k0_ref, k1_ref, k2_ref)):
            b = bias_ref[h, :, w * TOKEN_BLOCK:(w + 1) * TOKEN_BLOCK]
            parts.append(_nt_dot(q, k_ref[:, sl]) * ATTN_SCALE + b)
        parts.append(_nt_dot(q, kc_ref[:, sl].astype(BF)) * ATTN_SCALE)
        p0, p1, p2, px = _softmax_parts(parts)
        o = (_dot(p0.astype(BF), v0_ref[:, sl]) + _dot(p1.astype(BF), v1_ref[:, sl])
             + _dot(p2.astype(BF), v2_ref[:, sl]) + _dot(px.astype(BF), vc_ref[:, sl].astype(BF)))
        o_ref[:, sl] = o.astype(o_ref.dtype)


def neighbourhood_attention(qb, kb, vb, cache_k, cache_v, bias3, e, N, T):
    M = N * T
    nblk = T // TOKEN_BLOCK
    L = cache_k.shape[2]

    def win(w):
        return lambda n, i: (n * nblk + jnp.clip(i - 1, 0, nblk - NB_WIN_BLOCKS) + w, 0)

    def case(n, i):
        return (0, jnp.where(i == 0, 0, jnp.where(i == nblk - 1, 2, 1)), 0, 0)

    blk = pl.BlockSpec((TOKEN_BLOCK, B_WIDTH), lambda n, i: (n * nblk + i, 0))
    return pl.pallas_call(
        _nbr_kernel,
        grid=(N, nblk),
        in_specs=[
            blk,
            pl.BlockSpec((TOKEN_BLOCK, B_WIDTH), win(0)),
            pl.BlockSpec((TOKEN_BLOCK, B_WIDTH), win(1)),
            pl.BlockSpec((TOKEN_BLOCK, B_WIDTH), win(2)),
            pl.BlockSpec((TOKEN_BLOCK, B_WIDTH), win(0)),
            pl.BlockSpec((TOKEN_BLOCK, B_WIDTH), win(1)),
            pl.BlockSpec((TOKEN_BLOCK, B_WIDTH), win(2)),
            pl.BlockSpec((None, None, L, B_WIDTH), lambda n, i: (n, e, 0, 0)),
            pl.BlockSpec((None, None, L, B_WIDTH), lambda n, i: (n, e, 0, 0)),
            pl.BlockSpec((B_HEADS, None, TOKEN_BLOCK, NB_WIN_BLOCKS * TOKEN_BLOCK), case),
        ],
        out_specs=blk,
        out_shape=jax.ShapeDtypeStruct((M, B_WIDTH), BF),
        compiler_params=_cparams("parallel", "arbitrary"),
    )(qb, kb, kb, kb, vb, vb, vb, cache_k, cache_v, bias3)


def _diff_kernel(*refs, has_ctx, lambda_init):
    if has_ctx:
        lam_ref, g_ref, q_ref, k_ref, v_ref, kc_ref, vc_ref, o_ref = refs
    else:
        lam_ref, g_ref, q_ref, k_ref, v_ref, o_ref = refs
    lp = lam_ref[...]
    lam = (jnp.exp(jnp.sum(lp[0:1] * lp[1:2], -1, keepdims=True))
           - jnp.exp(jnp.sum(lp[2:3] * lp[3:4], -1, keepdims=True)) + lambda_init)
    probs = []
    for half in range(2):
        sl = slice(half * HEAD_DIM, (half + 1) * HEAD_DIM)
        q = q_ref[:, sl].astype(BF)
        parts = [_nt_dot(q, k_ref[:, sl].astype(BF)) * ATTN_SCALE]
        if has_ctx:
            parts.append(_nt_dot(q, kc_ref[:, sl].astype(BF)) * ATTN_SCALE)
        probs.append(_softmax_parts(parts))
    a = (probs[0][0] - lam * probs[1][0]).astype(BF)
    o = _dot(a, v_ref[...].astype(BF))
    if has_ctx:
        ac = (probs[0][1] - lam * probs[1][1]).astype(BF)
        o = o + _dot(ac, vc_ref[...].astype(BF))
    of = o * lax.rsqrt(jnp.mean(o * o, -1, keepdims=True) + RMS_EPS)
    o_ref[...] = ((of * g_ref[...]) * (1.0 - lambda_init)).astype(o_ref.dtype)


def diff_attention(q, k, v, lam_p, subln, lambda_init, N, T, ctx=None, widx=0):
    M = N * T
    tq = min(T, 256)
    nq = T // tq
    W = C_V_DIM
    in_specs = [
        pl.BlockSpec((None, 4, HEAD_DIM), lambda n, h, i: (widx, 0, 0)),
        pl.BlockSpec((None, 1, W), lambda n, h, i: (widx, 0, 0)),
        pl.BlockSpec((tq, W), lambda n, h, i: (n * nq + i, h)),
        pl.BlockSpec((T, W), lambda n, h, i: (n, h)),
        pl.BlockSpec((T, W), lambda n, h, i: (n, h)),
    ]
    args = [lam_p, subln.reshape(subln.shape[0], 1, W), q, k, v]
    if ctx is not None:
        L = ctx[0].shape[2]
        in_specs += [pl.BlockSpec((None, None, L, W), lambda n, h, i: (n, widx, 0, h))] * 2
        args += list(ctx)
    return pl.pallas_call(
        functools.partial(_diff_kernel, has_ctx=ctx is not None, lambda_init=lambda_init),
        grid=(N, C_HEADS, nq),
        in_specs=in_specs,
        out_specs=pl.BlockSpec((tq, W), lambda n, h, i: (n * nq + i, h)),
        out_shape=jax.ShapeDtypeStruct((M, C_HEADS * W), BF),
        compiler_params=_cparams("parallel", "parallel", "arbitrary"),
    )(*args)


def _outproj_kernel(*refs, n_in, alpha):
    o_refs = refs[:n_in]
    w_refs = refs[n_in:2 * n_in]
    x_ref, m_ref, g_ref, b_ref, wr_ref, xo_ref, h_ref, aff_ref = refs[2 * n_in:]
    y = None
    for o_r, w_r in zip(o_refs, w_refs):
        t = _dot(o_r[...], w_r[...])
        y = t if y is None else y + t
    z = alpha * x_ref[...] + m_ref[2:3, :] * y
    xn = _layer_norm(z, g_ref[...], b_ref[...])
    xo_ref[...] = xn
    hb = (xn * (1.0 + m_ref[4:5, :]) + m_ref[3:4, :]).astype(BF)
    h_ref[...] = hb
    logits = _dot(hb, wr_ref[...])
    mx = jnp.max(logits, -1, keepdims=True)
    ex = jnp.exp(logits - mx)
    aff_ref[...] = ex / jnp.sum(ex, -1, keepdims=True)


def outproj_ln_router(os_, w_out, widx, x, mods, layer, row0, per_req, T, ln_g, ln_b, w_router, alpha):
    M, D = x.shape
    tm = min(T, 256)
    E = w_router.shape[-1]
    in_specs, args, koff = [], [], 0
    for o in os_:
        in_specs.append(pl.BlockSpec((tm, o.shape[1]), lambda i: (i, 0)))
        args.append(o)
    for o in os_:
        wk = o.shape[1]
        in_specs.append(pl.BlockSpec((None, wk, D), functools.partial(lambda i, kb: (widx, kb, 0), kb=koff // wk)))
        args.append(w_out)
        koff += wk
    in_specs += [
        pl.BlockSpec((tm, D), lambda i: (i, 0)),
        pl.BlockSpec((None, None, 6, D), lambda i: (layer, row0 + ((i * tm) // T) * per_req, 0, 0)),
        pl.BlockSpec((None, 1, D), lambda i: (layer, 0, 0)),
        pl.BlockSpec((None, 1, D), lambda i: (layer, 0, 0)),
        pl.BlockSpec((None, D, E), lambda i: (layer, 0, 0)),
    ]
    args += [x, mods, ln_g, ln_b, w_router]
    return pl.pallas_call(
        functools.partial(_outproj_kernel, n_in=len(os_), alpha=alpha),
        grid=(M // tm,),
        in_specs=in_specs,
        out_specs=[
            pl.BlockSpec((tm, D), lambda i: (i, 0)),
            pl.BlockSpec((tm, D), lambda i: (i, 0)),
            pl.BlockSpec((tm, E), lambda i: (i, 0)),
        ],
        out_shape=[
            jax.ShapeDtypeStruct((M, D), F32),
            jax.ShapeDtypeStruct((M, D), BF),
            jax.ShapeDtypeStruct((M, E), F32),
        ],
        compiler_params=_cparams("parallel"),
    )(*args)


def _route_kernel(aff_ref, scol_ref, srow_ref, cnt_ref, ahi_ref, amid_ref, alo_ref, *, cap, nblk):
    E = N_EXPERTS
    B = aff_ref.shape[0] // nblk
    a = aff_ref[...]
    bits = pltpu.bitcast(a, jnp.int32)

    def search(it, ans):
        cand = ans | jnp.left_shift(jnp.int32(1), 30 - it)
        cnt = jnp.sum((bits >= cand).astype(F32), axis=0, keepdims=True)
        return jnp.where(cnt >= cap, cand, ans)

    thr = lax.fori_loop(0, 31, search, jnp.zeros((1, E), jnp.int32))
    gt = bits > thr
    eq = bits == thr
    need = cap - jnp.sum(gt.astype(F32), axis=0, keepdims=True)

    r = lax.broadcasted_iota(jnp.int32, (B, B), 0)
    c = lax.broadcasted_iota(jnp.int32, (B, B), 1)
    lower = (c < r).astype(BF)
    upper = (r < c).astype(BF)
    eye = (lax.broadcasted_iota(jnp.int32, (E, E), 0) == lax.broadcasted_iota(jnp.int32, (E, E), 1)).astype(BF)

    carry = jnp.zeros((1, E), F32)
    sels = []
    for b in range(nblk):
        sl = slice(b * B, (b + 1) * B)
        eqb = eq[sl].astype(BF)
        pref = _dot(lower, eqb) + carry
        carry = carry + jnp.sum(eqb.astype(F32), axis=0, keepdims=True)
        sels.append(jnp.logical_or(gt[sl], jnp.logical_and(eq[sl], pref < need)))

    carry = jnp.zeros((1, E), F32)
    carry_t = jnp.zeros((E, 1), F32)
    for b in range(nblk):
        sl = slice(b * B, (b + 1) * B)
        selb = sels[b].astype(BF)
        cnt_ref[b:b + 1, :] = carry.astype(jnp.int32)
        pos = _dot(lower, selb) + carry
        scol_ref[sl, :] = jnp.where(sels[b], pos, -1.0)
        carry = carry + jnp.sum(selb.astype(F32), axis=0, keepdims=True)
        sel_t = _nt_dot(eye, selb)
        pos_t = _dot(sel_t.astype(BF), upper) + carry_t
        srow_ref[b] = jnp.where(sel_t > 0.5, pos_t, -1.0)
        carry_t = carry_t + jnp.sum(sel_t, axis=1, keepdims=True)
    cnt_ref[nblk:nblk + 1, :] = carry.astype(jnp.int32)

    hi = a.astype(BF)
    r1 = a - hi.astype(F32)
    mid = r1.astype(BF)
    ahi_ref[...] = hi
    amid_ref[...] = mid
    alo_ref[...] = (r1 - mid.astype(F32)).astype(BF)


def route(aff, N, T):
    E = N_EXPERTS
    cap = EC_CAPACITY_FACTOR * T // E
    B = min(T, TOKEN_BLOCK)
    nblk = T // B
    return pl.pallas_call(
        functools.partial(_route_kernel, cap=cap, nblk=nblk),
        grid=(N,),
        in_specs=[pl.BlockSpec((T, E), lambda n: (n, 0))],
        out_specs=[
            pl.BlockSpec((None, T, E), lambda n: (n, 0, 0)),
            pl.BlockSpec((None, nblk, E, B), lambda n: (n, 0, 0, 0)),
            pl.BlockSpec((None, nblk + 1, E), lambda n: (n, 0, 0)),
            pl.BlockSpec((None, T, E), lambda n: (n, 0, 0)),
            pl.BlockSpec((None, T, E), lambda n: (n, 0, 0)),
            pl.BlockSpec((None, T, E), lambda n: (n, 0, 0)),
        ],
        out_shape=[
            jax.ShapeDtypeStruct((N, T, E), F32),
            jax.ShapeDtypeStruct((N, nblk, E, B), F32),
            jax.ShapeDtypeStruct((N, nblk + 1, E), jnp.int32),
            jax.ShapeDtypeStruct((N, T, E), BF),
            jax.ShapeDtypeStruct((N, T, E), BF),
            jax.ShapeDtypeStruct((N, T, E), BF),
        ],
        compiler_params=_cparams("parallel"),
    )(aff)


def _gather_small_kernel(h_ref, srow_ref, ahi_ref, amid_ref, alo_ref, x_ref, g_ref, *, cap):
    E = N_EXPERTS
    S = E * cap
    T = h_ref.shape[0]
    sr = srow_ref[...].astype(BF)
    rep_t = (lax.broadcasted_iota(jnp.int32, (S, E), 0) // cap == lax.broadcasted_iota(jnp.int32, (S, E), 1)).astype(BF)
    slot_of_row = (lax.broadcasted_iota(jnp.int32, (S, 1), 0) % cap).astype(F32)
    onehot = (_dot(rep_t, sr) == slot_of_row).astype(BF)
    x = _dot(onehot, h_ref[...])
    for e in range(E):
        x_ref[e] = x[e * cap:(e + 1) * cap].astype(x_ref.dtype)
    gates = _dot(onehot, ahi_ref[...]) + _dot(onehot, amid_ref[...]) + _dot(onehot, alo_ref[...])
    lane = lax.broadcasted_iota(jnp.int32, (S, E), 1)
    row_e = lax.broadcasted_iota(jnp.int32, (S, E), 0) // cap
    gate = jnp.sum(jnp.where(lane == row_e, gates, 0.0), axis=1, keepdims=True)
    gb = jnp.broadcast_to(gate, (S, HEAD_DIM))
    for e in range(E):
        g_ref[e] = gb[e * cap:(e + 1) * cap]


def gather_small(h, srow, ahi, amid, alo, N, T):
    E = N_EXPERTS
    cap = EC_CAPACITY_FACTOR * T // E
    D = h.shape[1]
    return pl.pallas_call(
        functools.partial(_gather_small_kernel, cap=cap),
        grid=(N,),
        in_specs=[
            pl.BlockSpec((T, D), lambda n: (n, 0)),
            pl.BlockSpec((None, None, E, T), lambda n: (n, 0, 0, 0)),
            pl.BlockSpec((None, T, E), lambda n: (n, 0, 0)),
            pl.BlockSpec((None, T, E), lambda n: (n, 0, 0)),
            pl.BlockSpec((None, T, E), lambda n: (n, 0, 0)),
        ],
        out_specs=[
            pl.BlockSpec((E, None, cap, D), lambda n: (0, n, 0, 0)),
            pl.BlockSpec((E, None, cap, HEAD_DIM), lambda n: (0, n, 0, 0)),
        ],
        out_shape=[
            jax.ShapeDtypeStruct((E, N, cap, D), BF),
            jax.ShapeDtypeStruct((E, N, cap, HEAD_DIM), F32),
        ],
        compiler_params=_cparams("parallel"),
    )(h, srow, ahi, amid, alo)


def _gather_big_kernel(cnt_ref, h_ref, srow_ref, ahi_ref, amid_ref, alo_ref, x_ref, g_ref, acc, gacc, *, cap, nblk):
    E = N_EXPERTS
    C = SLOT_CHUNK
    B = TOKEN_BLOCK
    n = pl.program_id(0)
    e = pl.program_id(2)
    base = (n * E + e) * (nblk + 1)
    lane_e = lax.broadcasted_iota(jnp.int32, (C, E), 1) == e
    for j in range(cap // C):
        b_lo = jnp.int32(0)
        b_hi = jnp.int32(0)
        for b in range(nblk):
            b_lo = b_lo + (cnt_ref[base + b + 1] <= j * C).astype(jnp.int32)
            b_hi = b_hi + (cnt_ref[base + b] < (j + 1) * C).astype(jnp.int32)
        acc[...] = jnp.zeros_like(acc)
        gacc[...] = jnp.zeros_like(gacc)
        slots = (j * C + lax.broadcasted_iota(jnp.int32, (C, 1), 0)).astype(F32)

        def body(b, carry):
            off = pl.multiple_of(b * B, B)
            onehot = (srow_ref[pl.ds(b, 1), :] == slots).astype(BF)
            acc[...] += _dot(onehot, h_ref[pl.ds(off, B), :])
            gacc[...] += (_dot(onehot, ahi_ref[pl.ds(off, B), :]) + _dot(onehot, amid_ref[pl.ds(off, B), :])
                          + _dot(onehot, alo_ref[pl.ds(off, B), :]))
            return carry

        lax.fori_loop(b_lo, b_hi, body, 0)
        x_ref[j * C:(j + 1) * C, :] = acc[...].astype(x_ref.dtype)
        gate = jnp.sum(jnp.where(lane_e, gacc[...], 0.0), axis=1, keepdims=True)
        g_ref[j * C:(j + 1) * C, :] = jnp.broadcast_to(gate, (C, HEAD_DIM))


def gather_big(h, srow, cnt_flat, ahi, amid, alo, N, T):
    E = N_EXPERTS
    cap = EC_CAPACITY_FACTOR * T // E
    D = h.shape[1]
    nblk = T // TOKEN_BLOCK
    nd = 2 if D % 256 == 0 else 1
    Dh = D // nd
    return pl.pallas_call(
        functools.partial(_gather_big_kernel, cap=cap, nblk=nblk),
        grid_spec=pltpu.PrefetchScalarGridSpec(
            num_scalar_prefetch=1,
            grid=(N, nd, E),
            in_specs=[
                pl.BlockSpec((T, Dh), lambda n, d, e, c: (n, d)),
                pl.BlockSpec((None, None, nblk, TOKEN_BLOCK), lambda n, d, e, c: (n, e, 0, 0)),
                pl.BlockSpec((None, T, E), lambda n, d, e, c: (n, 0, 0)),
                pl.BlockSpec((None, T, E), lambda n, d, e, c: (n, 0, 0)),
                pl.BlockSpec((None, T, E), lambda n, d, e, c: (n, 0, 0)),
            ],
            out_specs=[
                pl.BlockSpec((None, None, cap, Dh), lambda n, d, e, c: (e, n, 0, d)),
                pl.BlockSpec((None, None, cap, HEAD_DIM), lambda n, d, e, c: (e, n, 0, d)),
            ],
            scratch_shapes=[pltpu.VMEM((SLOT_CHUNK, Dh), F32), pltpu.VMEM((SLOT_CHUNK, E), F32)],
        ),
        out_shape=[
            jax.ShapeDtypeStruct((E, N, cap, D), BF),
            jax.ShapeDtypeStruct((E, N, cap, nd * HEAD_DIM), F32),
        ],
        compiler_params=_cparams("parallel", "arbitrary", "arbitrary"),
    )(cnt_flat, h, srow, ahi, amid, alo)


def _gate_up_kernel(x_ref, wg_ref, wu_ref, o_ref, wg_bf, wu_bf):
    @pl.when(pl.program_id(2) == 0)
    def _():
        wg_bf[...] = wg_ref[...].astype(BF)
        wu_bf[...] = wu_ref[...].astype(BF)

    x = x_ref[...]
    a = _dot(x, wg_bf[...])
    u = _dot(x, wu_bf[...])
    o_ref[...] = ((a * jax.nn.sigmoid(a)) * u).astype(o_ref.dtype)


def expert_gate_up(x, w_gate, w_up, layer):
    E, R, D = x.shape
    F = w_gate.shape[-1]
    tr = min(R, 512)
    tf = min(F, 512)
    return pl.pallas_call(
        _gate_up_kernel,
        grid=(E, F // tf, R // tr),
        in_specs=[
            pl.BlockSpec((None, tr, D), lambda e, f, r: (e, r, 0)),
            pl.BlockSpec((None, None, D, tf), lambda e, f, r: (layer, e, 0, f)),
            pl.BlockSpec((None, None, D, tf), lambda e, f, r: (layer, e, 0, f)),
        ],
        out_specs=pl.BlockSpec((None, tr, tf), lambda e, f, r: (e, r, f)),
        out_shape=jax.ShapeDtypeStruct((E, R, F), BF),
        scratch_shapes=[pltpu.VMEM((D, tf), BF), pltpu.VMEM((D, tf), BF)],
        compiler_params=_cparams("parallel", "parallel", "arbitrary"),
    )(x, w_gate, w_up)


def _down_kernel(h_ref, w_ref, g_ref, o_ref, w_bf):
    @pl.when(pl.program_id(2) == 0)
    def _():
        w_bf[...] = w_ref[...].astype(BF)

    o_ref[...] = (_dot(h_ref[...], w_bf[...]) * g_ref[:, 0:1]).astype(o_ref.dtype)


def expert_down(h, w_down, gate, layer):
    E, R, F = h.shape
    D = w_down.shape[-1]
    tr = min(R, 512)
    td = min(D, 512)
    return pl.pallas_call(
        _down_kernel,
        grid=(E, D // td, R // tr),
        in_specs=[
            pl.BlockSpec((None, tr, F), lambda e, d, r: (e, r, 0)),
            pl.BlockSpec((None, None, F, td), lambda e, d, r: (layer, e, 0, d)),
            pl.BlockSpec((None, tr, HEAD_DIM), lambda e, d, r: (e, r, 0)),
        ],
        out_specs=pl.BlockSpec((None, tr, td), lambda e, d, r: (e, r, d)),
        out_shape=jax.ShapeDtypeStruct((E, R, D), BF),
        scratch_shapes=[pltpu.VMEM((F, td), BF)],
        compiler_params=_cparams("parallel", "parallel", "arbitrary"),
    )(h, w_down, gate)


def _residual_epilogue(f, x_ref, m_ref, mn_ref, g_ref, b_ref, xo_ref, h_ref, alpha):
    z = alpha * x_ref[...] + m_ref[5:6, :] * f
    xn = _layer_norm(z, g_ref[...], b_ref[...])
    xo_ref[...] = xn
    h_ref[...] = (xn * (1.0 + mn_ref[1:2, :]) + mn_ref[0:1, :]).astype(h_ref.dtype)


def _scatter_small_kernel(y_ref, scol_ref, x_ref, m_ref, mn_ref, g_ref, b_ref, xo_ref, h_ref, *, cap, alpha):
    E = N_EXPERTS
    S = E * cap
    sc = scol_ref[...].astype(BF)
    rep = (lax.broadcasted_iota(jnp.int32, (E, S), 0) == lax.broadcasted_iota(jnp.int32, (E, S), 1) // cap).astype(BF)
    slot_of_col = (lax.broadcasted_iota(jnp.int32, (1, S), 1) % cap).astype(F32)
    onehot_t = (_dot(sc, rep) == slot_of_col).astype(BF)
    y = jnp.concatenate([y_ref[e] for e in range(E)], axis=0)
    f = _dot(onehot_t, y)
    _residual_epilogue(f, x_ref, m_ref, mn_ref, g_ref, b_ref, xo_ref, h_ref, alpha)


def scatter_small(y, scol, x, mods, layer, next_layer, row0, per_req, ln_g, ln_b, N, T, alpha):
    E = N_EXPERTS
    cap = EC_CAPACITY_FACTOR * T // E
    D = x.shape[1]
    return pl.pallas_call(
        functools.partial(_scatter_small_kernel, cap=cap, alpha=alpha),
        grid=(N,),
        in_specs=[
            pl.BlockSpec((E, None, cap, D), lambda n: (0, n, 0, 0)),
            pl.BlockSpec((None, T, E), lambda n: (n, 0, 0)),
            pl.BlockSpec((T, D), lambda n: (n, 0)),
            pl.BlockSpec((None, None, 6, D), lambda n: (layer, row0 + n * per_req, 0, 0)),
            pl.BlockSpec((None, None, 6, D), lambda n: (next_layer, row0 + n * per_req, 0, 0)),
            pl.BlockSpec((None, 1, D), lambda n: (layer, 0, 0)),
            pl.BlockSpec((None, 1, D), lambda n: (layer, 0, 0)),
        ],
        out_specs=[pl.BlockSpec((T, D), lambda n: (n, 0)), pl.BlockSpec((T, D), lambda n: (n, 0))],
        out_shape=[jax.ShapeDtypeStruct((N * T, D), F32), jax.ShapeDtypeStruct((N * T, D), BF)],
        compiler_params=_cparams("parallel"),
    )(y, scol, x, mods, mods, ln_g, ln_b)


def _scatter_big_kernel(cnt_ref, y_ref, scol_ref, x_ref, m_ref, mn_ref, g_ref, b_ref, xo_ref, h_ref, acc,
                        *, nblk, tiles, alpha):
    E = N_EXPERTS
    C = SLOT_CHUNK
    B = TOKEN_BLOCK
    n = pl.program_id(0)
    tb = pl.program_id(1)
    e = pl.program_id(2)
    base = (n * E + e) * (nblk + 1)

    @pl.when(e == 0)
    def _():
        acc[...] = jnp.zeros_like(acc)

    lane_e = lax.broadcasted_iota(jnp.int32, (B, E), 1) == e
    for bl in range(tiles):
        blk = tb * tiles + bl
        lo = cnt_ref[base + blk]
        hi = cnt_ref[base + blk + 1]
        scol = jnp.sum(jnp.where(lane_e, scol_ref[bl * B:(bl + 1) * B, :], 0.0), axis=1, keepdims=True)
        j_lo = lo // C
        j_hi = jnp.where(hi > lo, (hi + C - 1) // C, j_lo)

        def body(j, carry):
            slots = (j * C + lax.broadcasted_iota(jnp.int32, (1, C), 1)).astype(F32)
            onehot_t = (scol == slots).astype(BF)
            acc[bl * B:(bl + 1) * B, :] += _dot(onehot_t, y_ref[pl.ds(pl.multiple_of(j * C, C), C), :])
            return carry

        lax.fori_loop(j_lo, j_hi, body, 0)

    @pl.when(e == E - 1)
    def _():
        _residual_epilogue(acc[...], x_ref, m_ref, mn_ref, g_ref, b_ref, xo_ref, h_ref, alpha)


def scatter_big(y, scol, cnt_flat, x, mods, layer, next_layer, row0, per_req, ln_g, ln_b, N, T, alpha):
    E = N_EXPERTS
    cap = EC_CAPACITY_FACTOR * T // E
    D = x.shape[1]
    nblk = T // TOKEN_BLOCK
    tiles = 2 if nblk % 2 == 0 else 1
    tt = tiles * TOKEN_BLOCK
    ntt = T // tt
    return pl.pallas_call(
        functools.partial(_scatter_big_kernel, nblk=nblk, tiles=tiles, alpha=alpha),
        grid_spec=pltpu.PrefetchScalarGridSpec(
            num_scalar_prefetch=1,
            grid=(N, ntt, E),
            in_specs=[
                pl.BlockSpec((None, None, cap, D), lambda n, t, e, c: (e, n, 0, 0)),
                pl.BlockSpec((None, tt, E), lambda n, t, e, c: (n, t, 0)),
                pl.BlockSpec((tt, D), lambda n, t, e, c: (n * ntt + t, 0)),
                pl.BlockSpec((None, None, 6, D), lambda n, t, e, c: (layer, row0 + n * per_req, 0, 0)),
                pl.BlockSpec((None, None, 6, D), lambda n, t, e, c: (next_layer, row0 + n * per_req, 0, 0)),
                pl.BlockSpec((None, 1, D), lambda n, t, e, c: (layer, 0, 0)),
                pl.BlockSpec((None, 1, D), lambda n, t, e, c: (layer, 0, 0)),
            ],
            out_specs=[
                pl.BlockSpec((tt, D), lambda n, t, e, c: (n * ntt + t, 0)),
                pl.BlockSpec((tt, D), lambda n, t, e, c: (n * ntt + t, 0)),
            ],
            scratch_shapes=[pltpu.VMEM((tt, D), F32)],
        ),
        out_shape=[jax.ShapeDtypeStruct((N * T, D), F32), jax.ShapeDtypeStruct((N * T, D), BF)],
        compiler_params=_cparams("parallel", "parallel", "arbitrary"),
    )(cnt_flat, y, scol, x, mods, mods, ln_g, ln_b)


def moe_block(aff, h, x, mods, layer, next_layer, row0, per_req, ln_g, ln_b, w_gate, w_up, w_down, N, T, alpha):
    E = N_EXPERTS
    cap = EC_CAPACITY_FACTOR * T // E
    D = x.shape[1]
    scol, srow, cnt, ahi, amid, alo = route(aff, N, T)
    if T <= TOKEN_BLOCK:
        xs, gate = gather_small(h, srow, ahi, amid, alo, N, T)
    else:
        cnt_flat = jnp.transpose(cnt, (0, 2, 1)).reshape(-1)
        xs, gate = gather_big(h, jnp.transpose(srow, (0, 2, 1, 3)), cnt_flat, ahi, amid, alo, N, T)
    hm = expert_gate_up(xs.reshape(E, N * cap, D), w_gate, w_up, layer)
    y = expert_down(hm, w_down, gate.reshape(E, N * cap, gate.shape[-1]), layer).reshape(E, N, cap, D)
    if T <= TOKEN_BLOCK:
        return scatter_small(y, scol, x, mods, layer, next_layer, row0, per_req, ln_g, ln_b, N, T, alpha)
    return scatter_big(y, scol, cnt_flat, x, mods, layer, next_layer, row0, per_req, ln_g, ln_b, N, T, alpha)


def kernel(x_prompt, x_sample, cache_a_k, cache_a_v, cache_b_k, cache_b_v, cache_c_k, cache_c_v, c, c_ctx, w_mod, b_mod, w_in_even, w_out_even, a_sink, na_rpb, w_in_odd, w_out_odd, diff_lambda, diff_subln, ln_g, ln_b, w_router, w_gate, w_up, w_down):
    NP, TP, D = x_prompt.shape
    NS, TS, _ = x_sample.shape
    depth = w_mod.shape[0]
    L = cache_a_k.shape[2]
    alpha = (2 * depth) ** 0.25
    n_even = (depth + 1) // 2
    n_odd = depth // 2

    rows = 1 + NS
    rpad = -rows % 8
    cond = jnp.concatenate([c_ctx[None], c, jnp.zeros((rpad, D), F32)], axis=0)
    mods = adaln_all(cond, w_mod, b_mod).reshape(depth, rows + rpad, 6, D)

    rope = rope_tables(TS)
    sinks = a_sink.reshape(-1)
    w_out_even_bf = w_out_even.astype(BF)
    w_out_odd_bf = w_out_odd.astype(BF)
    w_router_bf = w_router.astype(BF)
    ln_g4 = ln_g.reshape(depth, 2, 1, D)
    ln_b4 = ln_b.reshape(depth, 2, 1, D)
    ca_k = cache_a_k.reshape(NS, n_even, L, A_KV_HEADS * HEAD_DIM)
    ca_v = cache_a_v.reshape(NS, n_even, L, A_KV_HEADS * HEAD_DIM)
    cb_k = cache_b_k.reshape(NS, n_even, L, B_WIDTH)
    cb_v = cache_b_v.reshape(NS, n_even, L, B_WIDTH)
    cc_k = cache_c_k.reshape(NS, n_odd, L, C_HEADS * 2 * HEAD_DIM)
    cc_v = cache_c_v.reshape(NS, n_odd, L, C_HEADS * C_V_DIM)

    xp = x_prompt.reshape(NP * TP, D)
    xs = x_sample.reshape(NS * TS, D)
    hp = modulate_rows(xp, mods, 0, 0, 0, TP)
    hs = modulate_rows(xs, mods, 0, 1, 1, TS)

    new_a_k, new_a_v, new_b_k, new_b_v, new_c_k, new_c_v = [], [], [], [], [], []
    QA = A_Q_HEADS * HEAD_DIM
    KA = A_KV_HEADS * HEAD_DIM
    for l in range(depth):
        if l % 2 == 0:
            e = l // 2
            b0 = A_WIDTH
            pa = project(hp, w_in_even, e, 0, A_WIDTH, 0, None, TP, F32)
            pqb = project(hp, w_in_even, e, b0, B_WIDTH, 0, None, TP, F32)
            pkb = project(hp, w_in_even, e, b0 + B_WIDTH, B_WIDTH, 0, None, TP, F32)
            pvb = project(hp, w_in_even, e, b0 + 2 * B_WIDTH, B_WIDTH, 0, None, TP, F32)
            new_a_k.append(pa[:, QA:QA + KA].reshape(NP, TP, A_KV_HEADS, HEAD_DIM))
            new_a_v.append(pa[:, QA + KA:QA + 2 * KA].reshape(NP, TP, A_KV_HEADS, HEAD_DIM))
            new_b_k.append(pkb.reshape(NP, TP, B_HEADS, HEAD_DIM))
            new_b_v.append(pvb.reshape(NP, TP, B_HEADS, HEAD_DIM))
            op = [ctx_even_attention(pa, pqb, pkb, pvb, sinks, e, NP, TP)]

            sa = project(hs, w_in_even, e, 0, A_WIDTH, A_Q_HEADS + A_KV_HEADS, rope, TS, BF)
            sqb = project(hs, w_in_even, e, b0, B_WIDTH, 0, rope, TS, BF)
            skb = project(hs, w_in_even, e, b0 + B_WIDTH, B_WIDTH, 0, rope, TS, BF)
            svb = project(hs, w_in_even, e, b0 + 2 * B_WIDTH, B_WIDTH, 0, rope, TS, BF)
            oa = window_attention(sa, ca_k, ca_v, sinks, e, NS, TS)
            bias3 = neighbourhood_bias(na_rpb[e], TS)
            ob = neighbourhood_attention(sqb, skb, svb, cb_k, cb_v, bias3, e, NS, TS)
            os_ = [oa, ob]
            w_out, widx = w_out_even_bf, e
        else:
            o = l // 2
            lambda_init = 0.8 - 0.6 * math.exp(-0.3 * l)
            W = C_HEADS * C_V_DIM
            pq = project(hp, w_in_odd, o, 0, W, 0, None, TP, F32)
            pk = project(hp, w_in_odd, o, W, W, 0, None, TP, F32)
            pv = project(hp, w_in_odd, o, 2 * W, W, 0, None, TP, F32)
            new_c_k.append(pk.reshape(NP, TP, C_HEADS, 2, HEAD_DIM))
            new_c_v.append(pv.reshape(NP, TP, C_HEADS, C_V_DIM))
            op = [diff_attention(pq, pk, pv, diff_lambda, diff_subln, lambda_init, NP, TP, None, o)]

            sq = project(hs, w_in_odd, o, 0, W, 2 * C_HEADS, rope, TS, BF)
            sk = project(hs, w_in_odd, o, W, W, 2 * C_HEADS, rope, TS, BF)
            sv = project(hs, w_in_odd, o, 2 * W, W, 0, rope, TS, BF)
            os_ = [diff_attention(sq, sk, sv, diff_lambda, diff_subln, lambda_init, NS, TS, (cc_k, cc_v), o)]
            w_out, widx = w_out_odd_bf, o

        lg1, lb1 = ln_g4[:, 0], ln_b4[:, 0]
        lg2, lb2 = ln_g4[:, 1], ln_b4[:, 1]
        xp, hp, affp = outproj_ln_router(op, w_out, widx, xp, mods, l, 0, 0, TP, lg1, lb1, w_router_bf, alpha)
        xs, hs, affs = outproj_ln_router(os_, w_out, widx, xs, mods, l, 1, 1, TS, lg1, lb1, w_router_bf, alpha)
        nl = min(l + 1, depth - 1)
        xp, hp = moe_block(affp, hp, xp, mods, l, nl, 0, 0, lg2, lb2, w_gate, w_up, w_down, NP, TP, alpha)
        xs, hs = moe_block(affs, hs, xs, mods, l, nl, 1, 1, lg2, lb2, w_gate, w_up, w_down, NS, TS, alpha)

    return (xp.reshape(NP, TP, D), xs.reshape(NS, TS, D),
            jnp.stack(new_a_k, axis=1), jnp.stack(new_a_v, axis=1),
            jnp.stack(new_b_k, axis=1), jnp.stack(new_b_v, axis=1),
            jnp.stack(new_c_k, axis=1), jnp.stack(new_c_v, axis=1))
```

```python
import functools
import math

import jax
import jax.numpy as jnp
from jax import lax
from jax.experimental import pallas as pl
from jax.experimental.pallas import tpu as pltpu

BF = jnp.bfloat16
F32 = jnp.float32

GRID_W = 64
HEAD_DIM = 128
A_Q_HEADS = 8
A_KV_HEADS = 2
A_GROUP = A_Q_HEADS // A_KV_HEADS
A_BLOCK = 128
B_HEADS = 8
NA_ROWS = 8
NA_COLS = 16
C_HEADS = 8
C_V_DIM = 2 * HEAD_DIM
N_EXPERTS = 16
EC_CAPACITY_FACTOR = 2
ROPE_BASE = 10000.0
LN_EPS = 1e-5
RMS_EPS = 1e-5
ATTN_SCALE = HEAD_DIM ** -0.5
A_WIDTH = (A_Q_HEADS + 2 * A_KV_HEADS) * HEAD_DIM
B_WIDTH = B_HEADS * HEAD_DIM

VMEM_LIMIT_BYTES = 52 * 1024 * 1024
TOKEN_BLOCK = 256
SLOT_CHUNK = 128


def _cparams(*sem):
    return pltpu.CompilerParams(dimension_semantics=sem, vmem_limit_bytes=VMEM_LIMIT_BYTES)


def _nt_dot(a, b):
    return lax.dot_general(a, b, (((1,), (1,)), ((), ())), preferred_element_type=F32)


def _dot(a, b):
    return jnp.dot(a, b, preferred_element_type=F32)


def _layer_norm(z, g, b):
    mu = jnp.mean(z, -1, keepdims=True)
    d = z - mu
    var = jnp.mean(d * d, -1, keepdims=True)
    return d * lax.rsqrt(var + LN_EPS) * g + b


def _adaln_kernel(c_ref, w_ref, b_ref, o_ref):
    c = c_ref[...]
    s = (c * jax.nn.sigmoid(c)).astype(BF)
    o_ref[...] = _dot(s, w_ref[...].astype(BF)) + b_ref[...]


def adaln_all(cond, w_mod, b_mod):
    L, D, D6 = w_mod.shape
    R = cond.shape[0]
    tn = D // 2
    return pl.pallas_call(
        _adaln_kernel,
        grid=(L, D6 // tn),
        in_specs=[
            pl.BlockSpec((R, D), lambda l, j: (0, 0)),
            pl.BlockSpec((None, D, tn), lambda l, j: (l, 0, j)),
            pl.BlockSpec((None, 1, tn), lambda l, j: (l, 0, j)),
        ],
        out_specs=pl.BlockSpec((None, R, tn), lambda l, j: (l, 0, j)),
        out_shape=jax.ShapeDtypeStruct((L, R, D6), F32),
        compiler_params=_cparams("parallel", "parallel"),
    )(cond, w_mod, b_mod.reshape(L, 1, D6))


def _modulate_kernel(x_ref, m_ref, o_ref):
    o_ref[...] = (x_ref[...] * (1.0 + m_ref[1:2, :]) + m_ref[0:1, :]).astype(o_ref.dtype)


def modulate_rows(x, mods, layer, row0, per_req, T):
    M, D = x.shape
    tt = min(T, 512)
    return pl.pallas_call(
        _modulate_kernel,
        grid=(M // tt,),
        in_specs=[
            pl.BlockSpec((tt, D), lambda i: (i, 0)),
            pl.BlockSpec((None, None, 6, D), lambda i: (layer, row0 + ((i * tt) // T) * per_req, 0, 0)),
        ],
        out_specs=pl.BlockSpec((tt, D), lambda i: (i, 0)),
        out_shape=jax.ShapeDtypeStruct((M, D), BF),
        compiler_params=_cparams("parallel"),
    )(x, mods)


def _swap32(x):
    lane = lax.broadcasted_iota(jnp.int32, (1, HEAD_DIM), 1)
    return jnp.where((lane % 64) < 32, pltpu.roll(x, HEAD_DIM - 32, 1), pltpu.roll(x, 32, 1))


def _proj_kernel(*refs, n_rope_heads, heads_per_tile):
    if n_rope_heads:
        h_ref, w_ref, cos_ref, sin_ref, o_ref, wbf = refs
    else:
        h_ref, w_ref, o_ref, wbf = refs
    j = pl.program_id(0)
    i = pl.program_id(1)

    @pl.when(i == 0)
    def _():
        wbf[...] = w_ref[...].astype(BF)

    acc = _dot(h_ref[...], wbf[...])
    for hh in range(heads_per_tile):
        sl = slice(hh * HEAD_DIM, (hh + 1) * HEAD_DIM)
        x = acc[:, sl]
        if n_rope_heads == 0:
            o_ref[:, sl] = x.astype(o_ref.dtype)
        else:
            head = j * heads_per_tile + hh

            @pl.when(head < n_rope_heads)
            def _():
                o_ref[:, sl] = (x * cos_ref[...] + _swap32(x) * sin_ref[...]).astype(o_ref.dtype)

            @pl.when(head >= n_rope_heads)
            def _():
                o_ref[:, sl] = x.astype(o_ref.dtype)


def project(h, w, widx, col0, width, n_rope_heads, rope, T, out_dtype):
    M, D = h.shape
    heads = width // HEAD_DIM
    hpt = max(k for k in range(1, 9) if heads % k == 0 and col0 % (k * HEAD_DIM) == 0)
    tn = hpt * HEAD_DIM
    tm = min(T if n_rope_heads else M, 1024)
    tpb = max(T // tm, 1)
    in_specs = [
        pl.BlockSpec((tm, D), lambda j, i: (i, 0)),
        pl.BlockSpec((None, D, tn), lambda j, i: (widx, 0, col0 // tn + j)),
    ]
    args = [h, w]
    if n_rope_heads:
        in_specs += [pl.BlockSpec((tm, HEAD_DIM), lambda j, i: (i % tpb, 0))] * 2
        args += list(rope)
    return pl.pallas_call(
        functools.partial(_proj_kernel, n_rope_heads=n_rope_heads, heads_per_tile=hpt),
        grid=(width // tn, M // tm),
        in_specs=in_specs,
        out_specs=pl.BlockSpec((tm, tn), lambda j, i: (i, j)),
        out_shape=jax.ShapeDtypeStruct((M, width), out_dtype),
        scratch_shapes=[pltpu.VMEM((D, tn), BF)],
        compiler_params=_cparams("parallel", "arbitrary"),
    )(*args)


def rope_tables(T):
    t = jnp.arange(T)
    row = (t // GRID_W).astype(F32)
    col = (t % GRID_W).astype(F32)
    axis_dim = HEAD_DIM // 2
    inv_freq = ROPE_BASE ** (-jnp.arange(0, axis_dim, 2, dtype=F32) / axis_dim)
    ar = row[:, None] * inv_freq
    ac = col[:, None] * inv_freq
    cos = jnp.concatenate([jnp.cos(ar), jnp.cos(ar), jnp.cos(ac), jnp.cos(ac)], -1)
    sin = jnp.concatenate([-jnp.sin(ar), jnp.sin(ar), -jnp.sin(ac), jnp.sin(ac)], -1)
    return cos, sin


def _softmax_parts(parts, sink=None):
    m = None
    for s in parts:
        mi = jnp.max(s, -1, keepdims=True)
        m = mi if m is None else jnp.maximum(m, mi)
    if sink is not None:
        m = jnp.maximum(m, sink)
    es = [jnp.exp(s - m) for s in parts]
    den = None
    for e in es:
        di = jnp.sum(e, -1, keepdims=True)
        den = di if den is None else den + di
    if sink is not None:
        den = den + jnp.exp(sink - m)
    inv = 1.0 / den
    return [e * inv for e in es]


def _ctx_even_kernel(sink_ref, a_ref, qb_ref, kb_ref, vb_ref, o_ref, *, sink_base):
    T = a_ref.shape[0]
    for kv in range(A_KV_HEADS):
        k = a_ref[:, (A_Q_HEADS + kv) * HEAD_DIM:(A_Q_HEADS + kv + 1) * HEAD_DIM].astype(BF)
        v = a_ref[:, (A_Q_HEADS + A_KV_HEADS + kv) * HEAD_DIM:(A_Q_HEADS + A_KV_HEADS + kv + 1) * HEAD_DIM].astype(BF)
        for g in range(A_GROUP):
            hq = kv * A_GROUP + g
            q = a_ref[:, hq * HEAD_DIM:(hq + 1) * HEAD_DIM].astype(BF)
            s = _nt_dot(q, k) * ATTN_SCALE
            sink = jnp.full((T, 1), sink_ref[sink_base + hq], F32)
            (p,) = _softmax_parts([s], sink)
            o_ref[:, hq * HEAD_DIM:(hq + 1) * HEAD_DIM] = _dot(p.astype(BF), v).astype(o_ref.dtype)
    for hb in range(B_HEADS):
        sl = slice(hb * HEAD_DIM, (hb + 1) * HEAD_DIM)
        s = _nt_dot(qb_ref[:, sl].astype(BF), kb_ref[:, sl].astype(BF)) * ATTN_SCALE
        (p,) = _softmax_parts([s])
        o_ref[:, A_Q_HEADS * HEAD_DIM + hb * HEAD_DIM:A_Q_HEADS * HEAD_DIM + (hb + 1) * HEAD_DIM] = _dot(
            p.astype(BF), vb_ref[:, sl].astype(BF)).astype(o_ref.dtype)


def ctx_even_attention(arr_a, qb, kb, vb, sinks, e, N, T):
    M = N * T
    return pl.pallas_call(
        functools.partial(_ctx_even_kernel, sink_base=e * A_Q_HEADS),
        grid=(N,),
        in_specs=[
            pl.BlockSpec(memory_space=pltpu.SMEM),
            pl.BlockSpec((T, A_WIDTH), lambda n: (n, 0)),
            pl.BlockSpec((T, B_WIDTH), lambda n: (n, 0)),
            pl.BlockSpec((T, B_WIDTH), lambda n: (n, 0)),
            pl.BlockSpec((T, B_WIDTH), lambda n: (n, 0)),
        ],
        out_specs=pl.BlockSpec((T, A_Q_HEADS * HEAD_DIM + B_WIDTH), lambda n: (n, 0)),
        out_shape=jax.ShapeDtypeStruct((M, A_Q_HEADS * HEAD_DIM + B_WIDTH), BF),
        compiler_params=_cparams("parallel"),
    )(sinks, arr_a, qb, kb, vb)


def _win_kernel(sink_ref, q_ref, kvp_ref, kvc_ref, kvn_ref, kc_ref, vc_ref, o_ref, *, sink_base, nb):
    i = pl.program_id(1)
    R = A_GROUP * A_BLOCK
    qq = lax.broadcasted_iota(jnp.int32, (R, 1), 0) % A_BLOCK
    kk = lax.broadcasted_iota(jnp.int32, (1, A_BLOCK), 1)
    ok_prev = jnp.logical_and(kk >= qq, i > 0)
    ok_next = jnp.logical_and(kk <= qq, i < nb - 1)
    for kv in range(A_KV_HEADS):
        ksl = slice(kv * HEAD_DIM, (kv + 1) * HEAD_DIM)
        vsl = slice((A_KV_HEADS + kv) * HEAD_DIM, (A_KV_HEADS + kv + 1) * HEAD_DIM)
        q = jnp.concatenate(
            [q_ref[:, (kv * A_GROUP + g) * HEAD_DIM:(kv * A_GROUP + g + 1) * HEAD_DIM] for g in range(A_GROUP)], axis=0)
        sink = jnp.concatenate(
            [jnp.full((A_BLOCK, 1), sink_ref[sink_base + kv * A_GROUP + g], F32) for g in range(A_GROUP)], axis=0)
        s_p = jnp.where(ok_prev, _nt_dot(q, kvp_ref[:, ksl]) * ATTN_SCALE, -jnp.inf)
        s_c = _nt_dot(q, kvc_ref[:, ksl]) * ATTN_SCALE
        s_n = jnp.where(ok_next, _nt_dot(q, kvn_ref[:, ksl]) * ATTN_SCALE, -jnp.inf)
        s_x = _nt_dot(q, kc_ref[:, ksl].astype(BF)) * ATTN_SCALE
        p_p, p_c, p_n, p_x = _softmax_parts([s_p, s_c, s_n, s_x], sink)
        o = (_dot(p_p.astype(BF), kvp_ref[:, vsl]) + _dot(p_c.astype(BF), kvc_ref[:, vsl])
             + _dot(p_n.astype(BF), kvn_ref[:, vsl]) + _dot(p_x.astype(BF), vc_ref[:, ksl].astype(BF)))
        for g in range(A_GROUP):
            hq = kv * A_GROUP + g
            o_ref[:, hq * HEAD_DIM:(hq + 1) * HEAD_DIM] = o[g * A_BLOCK:(g + 1) * A_BLOCK].astype(o_ref.dtype)


def window_attention(arr_a, cache_k, cache_v, sinks, e, N, T):
    M = N * T
    nb = T // A_BLOCK
    L = cache_k.shape[2]
    QW = A_Q_HEADS * HEAD_DIM
    KVW = 2 * A_KV_HEADS * HEAD_DIM
    kvblk = QW // KVW
    return pl.pallas_call(
        functools.partial(_win_kernel, sink_base=e * A_Q_HEADS, nb=nb),
        grid=(N, nb),
        in_specs=[
            pl.BlockSpec(memory_space=pltpu.SMEM),
            pl.BlockSpec((A_BLOCK, QW), lambda n, i: (n * nb + i, 0)),
            pl.BlockSpec((A_BLOCK, KVW), lambda n, i: (n * nb + jnp.maximum(i - 1, 0), kvblk)),
            pl.BlockSpec((A_BLOCK, KVW), lambda n, i: (n * nb + i, kvblk)),
            pl.BlockSpec((A_BLOCK, KVW), lambda n, i: (n * nb + jnp.minimum(i + 1, nb - 1), kvblk)),
            pl.BlockSpec((None, None, L, A_KV_HEADS * HEAD_DIM), lambda n, i: (n, e, 0, 0)),
            pl.BlockSpec((None, None, L, A_KV_HEADS * HEAD_DIM), lambda n, i: (n, e, 0, 0)),
        ],
        out_specs=pl.BlockSpec((A_BLOCK, QW), lambda n, i: (n * nb + i, 0)),
        out_shape=jax.ShapeDtypeStruct((M, QW), BF),
        compiler_params=_cparams("parallel", "parallel"),
    )(sinks, arr_a, arr_a, arr_a, arr_a, cache_k, cache_v)


NB_Q_ROWS = TOKEN_BLOCK // GRID_W
NB_WIN_BLOCKS = 3


def _toeplitz_kernel(rpb_ref, o_ref):
    K = rpb_ref.shape[1]
    W2 = GRID_W * GRID_W
    col = lax.broadcasted_iota(jnp.int32, (K, W2), 1)
    cq = col // GRID_W
    ck = col % GRID_W
    dc = jnp.clip(ck - cq + (NA_COLS - 1), 0, 2 * NA_COLS - 2)
    onehot = (lax.broadcasted_iota(jnp.int32, (K, W2), 0) == dc).astype(BF)
    t = rpb_ref[...]
    hi = t.astype(BF)
    r1 = t - hi.astype(F32)
    mid = r1.astype(BF)
    lo = (r1 - mid.astype(F32)).astype(BF)
    val = _dot(hi, onehot) + _dot(mid, onehot) + _dot(lo, onehot)
    c0 = jnp.clip(cq[0:1] - NA_COLS // 2, 0, GRID_W - NA_COLS)
    ok = jnp.logical_and(ck[0:1] >= c0, ck[0:1] < c0 + NA_COLS)
    o_ref[...] = jnp.where(ok, val, -jnp.inf)


def toeplitz_bias(rpb):
    G, H, NR, NC = rpb.shape
    R = G * H * NR
    K = 32
    flat = jnp.pad(rpb.reshape(R, NC), ((0, -R % 8), (0, K - NC)))
    out = pl.pallas_call(
        _toeplitz_kernel,
        out_shape=jax.ShapeDtypeStruct((flat.shape[0], GRID_W * GRID_W), F32),
        compiler_params=pltpu.CompilerParams(vmem_limit_bytes=VMEM_LIMIT_BYTES),
    )(flat)
    return out[:R].reshape(G, H, NR, GRID_W, GRID_W)


def neighbourhood_bias(tiles, T):
    rows = T // GRID_W
    nblk = T // TOKEN_BLOCK
    H = tiles.shape[0]
    masked = jnp.full((H, GRID_W, GRID_W), -jnp.inf, F32)
    out = []
    for blk in (0, 1, nblk - 1):
        w0 = min(max(blk - 1, 0), nblk - NB_WIN_BLOCKS) * NB_Q_ROWS
        qrows = []
        for ql in range(NB_Q_ROWS):
            rq = blk * NB_Q_ROWS + ql
            r0 = min(max(rq - NA_ROWS // 2, 0), rows - NA_ROWS)
            krow = []
            for kl in range(NB_WIN_BLOCKS * NB_Q_ROWS):
                rk = w0 + kl
                krow.append(tiles[:, rk - rq + NA_ROWS - 1] if r0 <= rk < r0 + NA_ROWS else masked)
            qrows.append(jnp.concatenate(krow, axis=-1))
        out.append(jnp.concatenate(qrows, axis=-2))
    return jnp.stack(out, axis=0)


def _nbr_kernel(q_ref, k0_ref, k1_ref, k2_ref, v0_ref, v1_ref, v2_ref, kc_ref, vc_ref, bias_ref, o_ref):
    for h in range(B_HEADS):
        sl = slice(h * HEAD_DIM, (h + 1) * HEAD_DIM)
        q = q_ref[:, sl]
        parts = []
        for w, k_ref in enumerate((k0_ref, k1_ref, k2_ref)):
            b = bias_ref[h, :, w * TOKEN_BLOCK:(w + 1) * TOKEN_BLOCK]
            parts.append(_nt_dot(q, k_ref[:, sl]) * ATTN_SCALE + b)
        parts.append(_nt_dot(q, kc_ref[:, sl].astype(BF)) * ATTN_SCALE)
        p0, p1, p2, px = _softmax_parts(parts)
        o = (_dot(p0.astype(BF), v0_ref[:, sl]) + _dot(p1.astype(BF), v1_ref[:, sl])
             + _dot(p2.astype(BF), v2_ref[:, sl]) + _dot(px.astype(BF), vc_ref[:, sl].astype(BF)))
        o_ref[:, sl] = o.astype(o_ref.dtype)


def neighbourhood_attention(qb, kb, vb, cache_k, cache_v, bias3, e, N, T):
    M = N * T
    nblk = T // TOKEN_BLOCK
    L = cache_k.shape[2]

    def win(w):
        return lambda n, i: (n * nblk + jnp.clip(i - 1, 0, nblk - NB_WIN_BLOCKS) + w, 0)

    def case(n, i):
        return (jnp.where(i == 0, 0, jnp.where(i == nblk - 1, 2, 1)), 0, 0, 0)

    blk = pl.BlockSpec((TOKEN_BLOCK, B_WIDTH), lambda n, i: (n * nblk + i, 0))
    return pl.pallas_call(
        _nbr_kernel,
        grid=(N, nblk),
        in_specs=[
            blk,
            pl.BlockSpec((TOKEN_BLOCK, B_WIDTH), win(0)),
            pl.BlockSpec((TOKEN_BLOCK, B_WIDTH), win(1)),
            pl.BlockSpec((TOKEN_BLOCK, B_WIDTH), win(2)),
            pl.BlockSpec((TOKEN_BLOCK, B_WIDTH), win(0)),
            pl.BlockSpec((TOKEN_BLOCK, B_WIDTH), win(1)),
            pl.BlockSpec((TOKEN_BLOCK, B_WIDTH), win(2)),
            pl.BlockSpec((None, None, L, B_WIDTH), lambda n, i: (n, e, 0, 0)),
            pl.BlockSpec((None, None, L, B_WIDTH), lambda n, i: (n, e, 0, 0)),
            pl.BlockSpec((None, B_HEADS, TOKEN_BLOCK, NB_WIN_BLOCKS * TOKEN_BLOCK), case),
        ],
        out_specs=blk,
        out_shape=jax.ShapeDtypeStruct((M, B_WIDTH), BF),
        compiler_params=_cparams("parallel", "arbitrary"),
    )(qb, kb, kb, kb, vb, vb, vb, cache_k, cache_v, bias3)


Q_PRESCALE = ATTN_SCALE * math.log2(math.e)


def _exp2_parts(parts):
    m = None
    for s in parts:
        mi = jnp.max(s, -1, keepdims=True)
        m = mi if m is None else jnp.maximum(m, mi)
    es = [jnp.exp2(s - m) for s in parts]
    den = None
    for e in es:
        di = jnp.sum(e, -1, keepdims=True)
        den = di if den is None else den + di
    return es, 1.0 / den


def _diff_kernel(*refs, has_ctx, lambda_init, q_prescaled):
    if has_ctx:
        lam_ref, g_ref, q_ref, k_ref, v_ref, kc_ref, vc_ref, o_ref = refs
    else:
        lam_ref, g_ref, q_ref, k_ref, v_ref, o_ref = refs
    lp = lam_ref[...]
    lam = (jnp.exp(jnp.sum(lp[0:1] * lp[1:2], -1, keepdims=True))
           - jnp.exp(jnp.sum(lp[2:3] * lp[3:4], -1, keepdims=True)) + lambda_init)
    nums, invs = [], []
    for half in range(2):
        sl = slice(half * HEAD_DIM, (half + 1) * HEAD_DIM)
        if q_prescaled:
            q = q_ref[:, sl]
        else:
            q = (q_ref[:, sl].astype(F32) * Q_PRESCALE).astype(BF)
        parts = [_nt_dot(q, k_ref[:, sl].astype(BF))]
        if has_ctx:
            parts.append(_nt_dot(q, kc_ref[:, sl].astype(BF)))
        es, inv = _exp2_parts(parts)
        nums.append(es)
        invs.append(inv)
    c0 = invs[0]
    c1 = lam * invs[1]
    a = (nums[0][0] * c0 - nums[1][0] * c1).astype(BF)
    o = _dot(a, v_ref[...].astype(BF))
    if has_ctx:
        ac = (nums[0][1] * c0 - nums[1][1] * c1).astype(BF)
        o = o + _dot(ac, vc_ref[...].astype(BF))
    of = o * lax.rsqrt(jnp.mean(o * o, -1, keepdims=True) + RMS_EPS)
    o_ref[...] = ((of * g_ref[...]) * (1.0 - lambda_init)).astype(o_ref.dtype)


def diff_attention(q, k, v, lam_p, subln, lambda_init, N, T, ctx=None, widx=0, q_prescaled=False):
    M = N * T
    tq = min(T, 256)
    nq = T // tq
    W = C_V_DIM
    in_specs = [
        pl.BlockSpec((None, 4, HEAD_DIM), lambda n, h, i: (widx, 0, 0)),
        pl.BlockSpec((None, 1, W), lambda n, h, i: (widx, 0, 0)),
        pl.BlockSpec((tq, W), lambda n, h, i: (n * nq + i, h)),
        pl.BlockSpec((T, W), lambda n, h, i: (n, h)),
        pl.BlockSpec((T, W), lambda n, h, i: (n, h)),
    ]
    args = [lam_p, subln.reshape(subln.shape[0], 1, W), q, k, v]
    if ctx is not None:
        L = ctx[0].shape[2]
        in_specs += [pl.BlockSpec((None, None, L, W), lambda n, h, i: (n, widx, 0, h))] * 2
        args += list(ctx)
    return pl.pallas_call(
        functools.partial(_diff_kernel, has_ctx=ctx is not None, lambda_init=lambda_init, q_prescaled=q_prescaled),
        grid=(N, C_HEADS, nq),
        in_specs=in_specs,
        out_specs=pl.BlockSpec((tq, W), lambda n, h, i: (n * nq + i, h)),
        out_shape=jax.ShapeDtypeStruct((M, C_HEADS * W), BF),
        compiler_params=_cparams("parallel", "parallel", "arbitrary"),
    )(*args)


def _outproj_kernel(*refs, n_in, alpha):
    o_refs = refs[:n_in]
    w_refs = refs[n_in:2 * n_in]
    x_ref, m_ref, g_ref, b_ref, wr_ref, xo_ref, h_ref, aff_ref = refs[2 * n_in:]
    y = None
    for o_r, w_r in zip(o_refs, w_refs):
        t = _dot(o_r[...], w_r[...])
        y = t if y is None else y + t
    z = alpha * x_ref[...] + m_ref[2:3, :] * y
    xn = _layer_norm(z, g_ref[...], b_ref[...])
    xo_ref[...] = xn
    hb = (xn * (1.0 + m_ref[4:5, :]) + m_ref[3:4, :]).astype(BF)
    h_ref[...] = hb
    logits = _dot(hb, wr_ref[...])
    mx = jnp.max(logits, -1, keepdims=True)
    ex = jnp.exp(logits - mx)
    aff_ref[...] = ex / jnp.sum(ex, -1, keepdims=True)


def outproj_ln_router(os_, w_out, widx, x, mods, layer, row0, per_req, T, ln_g, ln_b, w_router, alpha):
    M, D = x.shape
    tm = min(T, 256)
    E = w_router.shape[-1]
    in_specs, args, koff = [], [], 0
    for o in os_:
        in_specs.append(pl.BlockSpec((tm, o.shape[1]), lambda i: (i, 0)))
        args.append(o)
    for o in os_:
        wk = o.shape[1]
        in_specs.append(pl.BlockSpec((None, wk, D), functools.partial(lambda i, kb: (widx, kb, 0), kb=koff // wk)))
        args.append(w_out)
        koff += wk
    in_specs += [
        pl.BlockSpec((tm, D), lambda i: (i, 0)),
        pl.BlockSpec((None, None, 6, D), lambda i: (layer, row0 + ((i * tm) // T) * per_req, 0, 0)),
        pl.BlockSpec((None, 1, D), lambda i: (layer, 0, 0)),
        pl.BlockSpec((None, 1, D), lambda i: (layer, 0, 0)),
        pl.BlockSpec((None, D, E), lambda i: (layer, 0, 0)),
    ]
    args += [x, mods, ln_g, ln_b, w_router]
    return pl.pallas_call(
        functools.partial(_outproj_kernel, n_in=len(os_), alpha=alpha),
        grid=(M // tm,),
        in_specs=in_specs,
        out_specs=[
            pl.BlockSpec((tm, D), lambda i: (i, 0)),
            pl.BlockSpec((tm, D), lambda i: (i, 0)),
            pl.BlockSpec((tm, E), lambda i: (i, 0)),
        ],
        out_shape=[
            jax.ShapeDtypeStruct((M, D), F32),
            jax.ShapeDtypeStruct((M, D), BF),
            jax.ShapeDtypeStruct((M, E), F32),
        ],
        compiler_params=_cparams("parallel"),
    )(*args)


def _route_kernel(aff_ref, scol_ref, srow_ref, cnt_ref, a3_ref, *, cap, nblk):
    E = N_EXPERTS
    B = aff_ref.shape[0] // nblk
    a = aff_ref[...]
    bits = pltpu.bitcast(a, jnp.int32)

    def search(it, ans):
        cand = ans | jnp.left_shift(jnp.int32(1), 30 - it)
        cnt = jnp.sum((bits >= cand).astype(F32), axis=0, keepdims=True)
        return jnp.where(cnt >= cap, cand, ans)

    thr = lax.fori_loop(0, 31, search, jnp.zeros((1, E), jnp.int32))
    gt = bits > thr
    eq = bits == thr
    need = cap - jnp.sum(gt.astype(F32), axis=0, keepdims=True)

    r = lax.broadcasted_iota(jnp.int32, (B, B), 0)
    c = lax.broadcasted_iota(jnp.int32, (B, B), 1)
    lower = (c < r).astype(BF)
    upper = (r < c).astype(BF)
    eye = (lax.broadcasted_iota(jnp.int32, (E, E), 0) == lax.broadcasted_iota(jnp.int32, (E, E), 1)).astype(BF)

    carry = jnp.zeros((1, E), F32)
    sels = []
    for b in range(nblk):
        sl = slice(b * B, (b + 1) * B)
        eqb = eq[sl].astype(BF)
        pref = _dot(lower, eqb) + carry
        carry = carry + jnp.sum(eqb.astype(F32), axis=0, keepdims=True)
        sels.append(jnp.logical_or(gt[sl], jnp.logical_and(eq[sl], pref < need)))

    carry = jnp.zeros((1, E), F32)
    carry_t = jnp.zeros((E, 1), F32)
    for b in range(nblk):
        sl = slice(b * B, (b + 1) * B)
        selb = sels[b].astype(BF)
        cnt_ref[b:b + 1, :] = carry.astype(jnp.int32)
        pos = _dot(lower, selb) + carry
        scol_ref[sl, :] = jnp.where(sels[b], pos, -1.0)
        carry = carry + jnp.sum(selb.astype(F32), axis=0, keepdims=True)
        sel_t = _nt_dot(eye, selb)
        pos_t = _dot(sel_t.astype(BF), upper) + carry_t
        srow_ref[b] = jnp.where(sel_t > 0.5, pos_t, -1.0)
        carry_t = carry_t + jnp.sum(sel_t, axis=1, keepdims=True)
    cnt_ref[nblk:nblk + 1, :] = carry.astype(jnp.int32)

    hi = a.astype(BF)
    r1 = a - hi.astype(F32)
    mid = r1.astype(BF)
    lo = (r1 - mid.astype(F32)).astype(BF)
    lane = lax.broadcasted_iota(jnp.int32, (E, HEAD_DIM), 1)
    row = lax.broadcasted_iota(jnp.int32, (E, HEAD_DIM), 0)
    a3 = (_dot(hi, (lane == row).astype(BF)) + _dot(mid, (lane == row + E).astype(BF))
          + _dot(lo, (lane == row + 2 * E).astype(BF)))
    a3_ref[...] = a3.astype(BF)


def _gate_from_split(g3, expert):
    lane = lax.broadcasted_iota(jnp.int32, g3.shape, 1)
    pick = jnp.logical_and(lane % N_EXPERTS == expert, lane < 3 * N_EXPERTS)
    return jnp.sum(jnp.where(pick, g3, 0.0), axis=1, keepdims=True)


def route(aff, N, T):
    E = N_EXPERTS
    cap = EC_CAPACITY_FACTOR * T // E
    B = min(T, TOKEN_BLOCK)
    nblk = T // B
    return pl.pallas_call(
        functools.partial(_route_kernel, cap=cap, nblk=nblk),
        grid=(N,),
        in_specs=[pl.BlockSpec((T, E), lambda n: (n, 0))],
        out_specs=[
            pl.BlockSpec((None, T, E), lambda n: (n, 0, 0)),
            pl.BlockSpec((None, nblk, E, B), lambda n: (n, 0, 0, 0)),
            pl.BlockSpec((None, nblk + 1, E), lambda n: (n, 0, 0)),
            pl.BlockSpec((None, T, HEAD_DIM), lambda n: (n, 0, 0)),
        ],
        out_shape=[
            jax.ShapeDtypeStruct((N, T, E), F32),
            jax.ShapeDtypeStruct((N, nblk, E, B), F32),
            jax.ShapeDtypeStruct((N, nblk + 1, E), jnp.int32),
            jax.ShapeDtypeStruct((N, T, HEAD_DIM), BF),
        ],
        compiler_params=_cparams("parallel"),
    )(aff)


def _gather_small_kernel(h_ref, srow_ref, a3_ref, x_ref, g_ref, *, cap):
    E = N_EXPERTS
    S = E * cap
    T = h_ref.shape[0]
    sr = srow_ref[...].astype(BF)
    rep_t = (lax.broadcasted_iota(jnp.int32, (S, E), 0) // cap == lax.broadcasted_iota(jnp.int32, (S, E), 1)).astype(BF)
    slot_of_row = (lax.broadcasted_iota(jnp.int32, (S, 1), 0) % cap).astype(F32)
    onehot = (_dot(rep_t, sr) == slot_of_row).astype(BF)
    x = _dot(onehot, h_ref[...])
    for e in range(E):
        x_ref[e] = x[e * cap:(e + 1) * cap].astype(x_ref.dtype)
    gate = _gate_from_split(_dot(onehot, a3_ref[...]), lax.broadcasted_iota(jnp.int32, (S, 1), 0) // cap)
    gb = jnp.broadcast_to(gate, (S, HEAD_DIM))
    for e in range(E):
        g_ref[e] = gb[e * cap:(e + 1) * cap]


def gather_small(h, srow, a3, N, T):
    E = N_EXPERTS
    cap = EC_CAPACITY_FACTOR * T // E
    D = h.shape[1]
    return pl.pallas_call(
        functools.partial(_gather_small_kernel, cap=cap),
        grid=(N,),
        in_specs=[
            pl.BlockSpec((T, D), lambda n: (n, 0)),
            pl.BlockSpec((None, None, E, T), lambda n: (n, 0, 0, 0)),
            pl.BlockSpec((None, T, HEAD_DIM), lambda n: (n, 0, 0)),
        ],
        out_specs=[
            pl.BlockSpec((E, None, cap, D), lambda n: (0, n, 0, 0)),
            pl.BlockSpec((E, None, cap, HEAD_DIM), lambda n: (0, n, 0, 0)),
        ],
        out_shape=[
            jax.ShapeDtypeStruct((E, N, cap, D), BF),
            jax.ShapeDtypeStruct((E, N, cap, HEAD_DIM), F32),
        ],
        compiler_params=_cparams("parallel"),
    )(h, srow, a3)


def _gather_big_kernel(cnt_ref, h_ref, srow_ref, a3_ref, x_ref, g_ref, acc, gacc, *, cap, nblk):
    E = N_EXPERTS
    B = TOKEN_BLOCK
    W = min(SLOT_CHUNK, cap)
    n = pl.program_id(0)
    e = pl.program_id(2)
    base = (n * E + e) * (nblk + 1)
    acc[...] = jnp.zeros_like(acc)
    gacc[...] = jnp.zeros_like(gacc)
    row = lax.broadcasted_iota(jnp.int32, (W, 1), 0)
    for b in range(nblk):
        lo = cnt_ref[base + b]
        hi = cnt_ref[base + b + 1]
        ws0 = (lo // 8) * 8

        def add_window(k, carry, b=b, ws0=ws0):
            ws = pl.multiple_of(ws0 + k * W, 8)
            onehot = (srow_ref[b:b + 1, :] == (ws + row).astype(F32)).astype(BF)
            acc[pl.ds(ws, W), :] += _dot(onehot, h_ref[b * B:(b + 1) * B, :])
            gacc[pl.ds(ws, W), :] += _dot(onehot, a3_ref[b * B:(b + 1) * B, :])
            return carry

        add_window(0, 0)
        lax.fori_loop(1, (hi - ws0 + W - 1) // W, add_window, 0)
    x_ref[...] = acc[0:cap, :].astype(x_ref.dtype)
    g_ref[...] = jnp.broadcast_to(_gate_from_split(gacc[0:cap, :], e), (cap, HEAD_DIM))


def gather_big(h, srow, cnt_flat, a3, N, T):
    E = N_EXPERTS
    cap = EC_CAPACITY_FACTOR * T // E
    D = h.shape[1]
    nblk = T // TOKEN_BLOCK
    nd = 2 if D % 256 == 0 else 1
    Dh = D // nd
    W = min(SLOT_CHUNK, cap)
    return pl.pallas_call(
        functools.partial(_gather_big_kernel, cap=cap, nblk=nblk),
        grid_spec=pltpu.PrefetchScalarGridSpec(
            num_scalar_prefetch=1,
            grid=(N, nd, E),
            in_specs=[
                pl.BlockSpec((T, Dh), lambda n, d, e, c: (n, d)),
                pl.BlockSpec((None, None, nblk, TOKEN_BLOCK), lambda n, d, e, c: (n, e, 0, 0)),
                pl.BlockSpec((None, T, HEAD_DIM), lambda n, d, e, c: (n, 0, 0)),
            ],
            out_specs=[
                pl.BlockSpec((None, None, cap, Dh), lambda n, d, e, c: (e, n, 0, d)),
                pl.BlockSpec((None, None, cap, HEAD_DIM), lambda n, d, e, c: (e, n, 0, d)),
            ],
            scratch_shapes=[pltpu.VMEM((cap + W, Dh), F32), pltpu.VMEM((cap + W, HEAD_DIM), F32)],
        ),
        out_shape=[
            jax.ShapeDtypeStruct((E, N, cap, D), BF),
            jax.ShapeDtypeStruct((E, N, cap, nd * HEAD_DIM), F32),
        ],
        compiler_params=_cparams("parallel", "arbitrary", "arbitrary"),
    )(cnt_flat, h, srow, a3)


def _gate_up_kernel(x_ref, wg_ref, wu_ref, o_ref, wg_bf, wu_bf):
    @pl.when(pl.program_id(2) == 0)
    def _():
        wg_bf[...] = wg_ref[...].astype(BF)
        wu_bf[...] = wu_ref[...].astype(BF)

    x = x_ref[...]
    a = _dot(x, wg_bf[...])
    u = _dot(x, wu_bf[...])
    o_ref[...] = ((a * jax.nn.sigmoid(a)) * u).astype(o_ref.dtype)


def _row_tile(R, limit):
    return max(t for t in range(16, min(R, limit) + 1, 16) if R % t == 0)


def expert_gate_up(x, w_gate, w_up, layer):
    E, R, D = x.shape
    F = w_gate.shape[-1]
    tr = _row_tile(R, 1536)
    tf = min(F, 512)
    return pl.pallas_call(
        _gate_up_kernel,
        grid=(E, F // tf, R // tr),
        in_specs=[
            pl.BlockSpec((None, tr, D), lambda e, f, r: (e, r, 0)),
            pl.BlockSpec((None, None, D, tf), lambda e, f, r: (layer, e, 0, f)),
            pl.BlockSpec((None, None, D, tf), lambda e, f, r: (layer, e, 0, f)),
        ],
        out_specs=pl.BlockSpec((None, tr, tf), lambda e, f, r: (e, r, f)),
        out_shape=jax.ShapeDtypeStruct((E, R, F), BF),
        scratch_shapes=[pltpu.VMEM((D, tf), BF), pltpu.VMEM((D, tf), BF)],
        compiler_params=_cparams("parallel", "parallel", "arbitrary"),
    )(x, w_gate, w_up)


def _down_kernel(h_ref, w_ref, g_ref, o_ref, w_bf):
    @pl.when(pl.program_id(2) == 0)
    def _():
        w_bf[...] = w_ref[...].astype(BF)

    o_ref[...] = (_dot(h_ref[...], w_bf[...]) * g_ref[:, 0:1]).astype(o_ref.dtype)


def expert_down(h, w_down, gate, layer):
    E, R, F = h.shape
    D = w_down.shape[-1]
    tr = _row_tile(R, 1536)
    td = min(D, 512)
    return pl.pallas_call(
        _down_kernel,
        grid=(E, D // td, R // tr),
        in_specs=[
            pl.BlockSpec((None, tr, F), lambda e, d, r: (e, r, 0)),
            pl.BlockSpec((None, None, F, td), lambda e, d, r: (layer, e, 0, d)),
            pl.BlockSpec((None, tr, HEAD_DIM), lambda e, d, r: (e, r, 0)),
        ],
        out_specs=pl.BlockSpec((None, tr, td), lambda e, d, r: (e, r, d)),
        out_shape=jax.ShapeDtypeStruct((E, R, D), BF),
        scratch_shapes=[pltpu.VMEM((F, td), BF)],
        compiler_params=_cparams("parallel", "parallel", "arbitrary"),
    )(h, w_down, gate)


def _residual_epilogue(f, x_ref, m_ref, mn_ref, g_ref, b_ref, xo_ref, h_ref, alpha):
    z = alpha * x_ref[...] + m_ref[5:6, :] * f
    xn = _layer_norm(z, g_ref[...], b_ref[...])
    xo_ref[...] = xn
    h_ref[...] = (xn * (1.0 + mn_ref[1:2, :]) + mn_ref[0:1, :]).astype(h_ref.dtype)


def _scatter_small_kernel(y_ref, scol_ref, x_ref, m_ref, mn_ref, g_ref, b_ref, xo_ref, h_ref, *, cap, alpha):
    E = N_EXPERTS
    S = E * cap
    sc = scol_ref[...].astype(BF)
    rep = (lax.broadcasted_iota(jnp.int32, (E, S), 0) == lax.broadcasted_iota(jnp.int32, (E, S), 1) // cap).astype(BF)
    slot_of_col = (lax.broadcasted_iota(jnp.int32, (1, S), 1) % cap).astype(F32)
    onehot_t = (_dot(sc, rep) == slot_of_col).astype(BF)
    y = jnp.concatenate([y_ref[e] for e in range(E)], axis=0)
    f = _dot(onehot_t, y)
    _residual_epilogue(f, x_ref, m_ref, mn_ref, g_ref, b_ref, xo_ref, h_ref, alpha)


def scatter_small(y, y_row0, scol, x, mods, layer, next_layer, row0, per_req, ln_g, ln_b, N, T, alpha):
    E = N_EXPERTS
    cap = EC_CAPACITY_FACTOR * T // E
    D = x.shape[1]
    assert y_row0 % cap == 0
    return pl.pallas_call(
        functools.partial(_scatter_small_kernel, cap=cap, alpha=alpha),
        grid=(N,),
        in_specs=[
            pl.BlockSpec((E, cap, D), lambda n: (0, y_row0 // cap + n, 0)),
            pl.BlockSpec((None, T, E), lambda n: (n, 0, 0)),
            pl.BlockSpec((T, D), lambda n: (n, 0)),
            pl.BlockSpec((None, None, 6, D), lambda n: (layer, row0 + n * per_req, 0, 0)),
            pl.BlockSpec((None, None, 6, D), lambda n: (next_layer, row0 + n * per_req, 0, 0)),
            pl.BlockSpec((None, 1, D), lambda n: (layer, 0, 0)),
            pl.BlockSpec((None, 1, D), lambda n: (layer, 0, 0)),
        ],
        out_specs=[pl.BlockSpec((T, D), lambda n: (n, 0)), pl.BlockSpec((T, D), lambda n: (n, 0))],
        out_shape=[jax.ShapeDtypeStruct((N * T, D), F32), jax.ShapeDtypeStruct((N * T, D), BF)],
        compiler_params=_cparams("parallel"),
    )(y, scol, x, mods, mods, ln_g, ln_b)


def _scatter_big_kernel(cnt_ref, y_ref, scol_ref, x_ref, m_ref, mn_ref, g_ref, b_ref, xo_ref, h_ref, acc,
                        *, nblk, tiles, alpha):
    E = N_EXPERTS
    C = SLOT_CHUNK
    B = TOKEN_BLOCK
    G = y_ref.shape[0]
    cap = y_ref.shape[1]
    WIN = min(2 * C, cap)
    n = pl.program_id(0)
    tb = pl.program_id(1)
    g = pl.program_id(2)

    @pl.when(g == 0)
    def _():
        acc[...] = jnp.zeros_like(acc)

    lane = lax.broadcasted_iota(jnp.int32, (B, E), 1)
    for bl in range(tiles):
        blk = tb * tiles + bl
        sc = scol_ref[bl * B:(bl + 1) * B, :]
        f = None
        for gi in range(G):
            e = g * G + gi
            base = (n * E + e) * (nblk + 1)
            lo = cnt_ref[base + blk]
            hi = cnt_ref[base + blk + 1]
            scol = jnp.sum(jnp.where(lane == e, sc, 0.0), axis=1, keepdims=True)
            ws = pl.multiple_of(jnp.minimum((lo // C) * C, cap - WIN), C)
            slots = (ws + lax.broadcasted_iota(jnp.int32, (1, WIN), 1)).astype(F32)
            d = _dot((scol == slots).astype(BF), y_ref[gi, pl.ds(ws, WIN), :])
            f = d if f is None else f + d

            def more(j, carry, gi=gi, scol=scol):
                js = pl.multiple_of(j * C, C)
                chunk = (js + lax.broadcasted_iota(jnp.int32, (1, C), 1)).astype(F32)
                acc[bl * B:(bl + 1) * B, :] += _dot((scol == chunk).astype(BF), y_ref[gi, pl.ds(js, C), :])
                return carry

            lax.fori_loop((ws + WIN) // C, (hi + C - 1) // C, more, 0)
        acc[bl * B:(bl + 1) * B, :] += f

    @pl.when(g == pl.num_programs(2) - 1)
    def _():
        _residual_epilogue(acc[...], x_ref, m_ref, mn_ref, g_ref, b_ref, xo_ref, h_ref, alpha)


SCATTER_EXPERT_GROUP = 4


def scatter_big(y, y_row0, scol, cnt_flat, x, mods, layer, next_layer, row0, per_req, ln_g, ln_b, N, T, alpha):
    E = N_EXPERTS
    cap = EC_CAPACITY_FACTOR * T // E
    D = x.shape[1]
    nblk = T // TOKEN_BLOCK
    tiles = 2 if nblk % 2 == 0 else 1
    tt = tiles * TOKEN_BLOCK
    ntt = T // tt
    G = SCATTER_EXPERT_GROUP
    assert y_row0 % cap == 0 and cap % SLOT_CHUNK == 0
    return pl.pallas_call(
        functools.partial(_scatter_big_kernel, nblk=nblk, tiles=tiles, alpha=alpha),
        grid_spec=pltpu.PrefetchScalarGridSpec(
            num_scalar_prefetch=1,
            grid=(N, ntt, E // G),
            in_specs=[
                pl.BlockSpec((G, cap, D), lambda n, t, e, c: (e, y_row0 // cap + n, 0)),
                pl.BlockSpec((None, tt, E), lambda n, t, e, c: (n, t, 0)),
                pl.BlockSpec((tt, D), lambda n, t, e, c: (n * ntt + t, 0)),
                pl.BlockSpec((None, None, 6, D), lambda n, t, e, c: (layer, row0 + n * per_req, 0, 0)),
                pl.BlockSpec((None, None, 6, D), lambda n, t, e, c: (next_layer, row0 + n * per_req, 0, 0)),
                pl.BlockSpec((None, 1, D), lambda n, t, e, c: (layer, 0, 0)),
                pl.BlockSpec((None, 1, D), lambda n, t, e, c: (layer, 0, 0)),
            ],
            out_specs=[
                pl.BlockSpec((tt, D), lambda n, t, e, c: (n * ntt + t, 0)),
                pl.BlockSpec((tt, D), lambda n, t, e, c: (n * ntt + t, 0)),
            ],
            scratch_shapes=[pltpu.VMEM((tt, D), F32)],
        ),
        out_shape=[jax.ShapeDtypeStruct((N * T, D), F32), jax.ShapeDtypeStruct((N * T, D), BF)],
        compiler_params=_cparams("parallel", "parallel", "arbitrary"),
    )(cnt_flat, y, scol, x, mods, mods, ln_g, ln_b)


def moe_layer(groups, mods, layer, next_layer, ln_g, ln_b, w_gate, w_up, w_down, alpha):
    E = N_EXPERTS
    D = groups[0]["x"].shape[1]
    routed, xs_all, gate_all, row0s, rows = [], [], [], [], 0
    for gr in groups:
        N, T = gr["N"], gr["T"]
        cap = EC_CAPACITY_FACTOR * T // E
        scol, srow, cnt, a3 = route(gr["aff"], N, T)
        if T <= TOKEN_BLOCK:
            cnt_flat = None
            xs, gate = gather_small(gr["h"], srow, a3, N, T)
        else:
            cnt_flat = jnp.transpose(cnt, (0, 2, 1)).reshape(-1)
            xs, gate = gather_big(gr["h"], jnp.transpose(srow, (0, 2, 1, 3)), cnt_flat, a3, N, T)
        routed.append((scol, cnt_flat))
        xs_all.append(xs.reshape(E, N * cap, D))
        gate_all.append(gate.reshape(E, N * cap, gate.shape[-1])[:, :, :HEAD_DIM])
        row0s.append(rows)
        rows += N * cap
    hm = expert_gate_up(jnp.concatenate(xs_all, axis=1), w_gate, w_up, layer)
    y = expert_down(hm, w_down, jnp.concatenate(gate_all, axis=1), layer)
    out = []
    for gr, (scol, cnt_flat), y_row0 in zip(groups, routed, row0s):
        tail = (gr["x"], mods, layer, next_layer, gr["row0"], gr["per_req"], ln_g, ln_b, gr["N"], gr["T"], alpha)
        if cnt_flat is None:
            out.append(scatter_small(y, y_row0, scol, *tail))
        else:
            out.append(scatter_big(y, y_row0, scol, cnt_flat, *tail))
    return out


def kernel(x_prompt, x_sample, cache_a_k, cache_a_v, cache_b_k, cache_b_v, cache_c_k, cache_c_v, c, c_ctx, w_mod, b_mod, w_in_even, w_out_even, a_sink, na_rpb, w_in_odd, w_out_odd, diff_lambda, diff_subln, ln_g, ln_b, w_router, w_gate, w_up, w_down):
    NP, TP, D = x_prompt.shape
    NS, TS, _ = x_sample.shape
    depth = w_mod.shape[0]
    L = cache_a_k.shape[2]
    alpha = (2 * depth) ** 0.25
    n_even = (depth + 1) // 2
    n_odd = depth // 2

    rows = 1 + NS
    rpad = -rows % 8
    cond = jnp.concatenate([c_ctx[None], c, jnp.zeros((rpad, D), F32)], axis=0)
    mods = adaln_all(cond, w_mod, b_mod).reshape(depth, rows + rpad, 6, D)

    rope = rope_tables(TS)
    rope_q = (rope[0] * Q_PRESCALE, rope[1] * Q_PRESCALE)
    sinks = a_sink.reshape(-1)
    rpb_tiles = toeplitz_bias(na_rpb)
    w_out_even_bf = w_out_even.astype(BF)
    w_out_odd_bf = w_out_odd.astype(BF)
    w_router_bf = w_router.astype(BF)
    ln_g4 = ln_g.reshape(depth, 2, 1, D)
    ln_b4 = ln_b.reshape(depth, 2, 1, D)
    ca_k = cache_a_k.reshape(NS, n_even, L, A_KV_HEADS * HEAD_DIM)
    ca_v = cache_a_v.reshape(NS, n_even, L, A_KV_HEADS * HEAD_DIM)
    cb_k = cache_b_k.reshape(NS, n_even, L, B_WIDTH)
    cb_v = cache_b_v.reshape(NS, n_even, L, B_WIDTH)
    cc_k = cache_c_k.reshape(NS, n_odd, L, C_HEADS * 2 * HEAD_DIM)
    cc_v = cache_c_v.reshape(NS, n_odd, L, C_HEADS * C_V_DIM)

    xp = x_prompt.reshape(NP * TP, D)
    xs = x_sample.reshape(NS * TS, D)
    hp = modulate_rows(xp, mods, 0, 0, 0, TP)
    hs = modulate_rows(xs, mods, 0, 1, 1, TS)

    new_a_k, new_a_v, new_b_k, new_b_v, new_c_k, new_c_v = [], [], [], [], [], []
    QA = A_Q_HEADS * HEAD_DIM
    KA = A_KV_HEADS * HEAD_DIM
    for l in range(depth):
        if l % 2 == 0:
            e = l // 2
            b0 = A_WIDTH
            pa = project(hp, w_in_even, e, 0, A_WIDTH, 0, None, TP, F32)
            pqb = project(hp, w_in_even, e, b0, B_WIDTH, 0, None, TP, F32)
            pkb = project(hp, w_in_even, e, b0 + B_WIDTH, B_WIDTH, 0, None, TP, F32)
            pvb = project(hp, w_in_even, e, b0 + 2 * B_WIDTH, B_WIDTH, 0, None, TP, F32)
            new_a_k.append(pa[:, QA:QA + KA].reshape(NP, TP, A_KV_HEADS, HEAD_DIM))
            new_a_v.append(pa[:, QA + KA:QA + 2 * KA].reshape(NP, TP, A_KV_HEADS, HEAD_DIM))
            new_b_k.append(pkb.reshape(NP, TP, B_HEADS, HEAD_DIM))
            new_b_v.append(pvb.reshape(NP, TP, B_HEADS, HEAD_DIM))
            op = [ctx_even_attention(pa, pqb, pkb, pvb, sinks, e, NP, TP)]

            sa = project(hs, w_in_even, e, 0, A_WIDTH, A_Q_HEADS + A_KV_HEADS, rope, TS, BF)
            sqb = project(hs, w_in_even, e, b0, B_WIDTH, 0, rope, TS, BF)
            skb = project(hs, w_in_even, e, b0 + B_WIDTH, B_WIDTH, 0, rope, TS, BF)
            svb = project(hs, w_in_even, e, b0 + 2 * B_WIDTH, B_WIDTH, 0, rope, TS, BF)
            oa = window_attention(sa, ca_k, ca_v, sinks, e, NS, TS)
            bias3 = neighbourhood_bias(rpb_tiles[e], TS)
            ob = neighbourhood_attention(sqb, skb, svb, cb_k, cb_v, bias3, e, NS, TS)
            os_ = [oa, ob]
            w_out, widx = w_out_even_bf, e
        else:
            o = l // 2
            lambda_init = 0.8 - 0.6 * math.exp(-0.3 * l)
            W = C_HEADS * C_V_DIM
            pq = project(hp, w_in_odd, o, 0, W, 0, None, TP, F32)
            pk = project(hp, w_in_odd, o, W, W, 0, None, TP, F32)
            pv = project(hp, w_in_odd, o, 2 * W, W, 0, None, TP, F32)
            new_c_k.append(pk.reshape(NP, TP, C_HEADS, 2, HEAD_DIM))
            new_c_v.append(pv.reshape(NP, TP, C_HEADS, C_V_DIM))
            op = [diff_attention(pq, pk, pv, diff_lambda, diff_subln, lambda_init, NP, TP, None, o)]

            sq = project(hs, w_in_odd, o, 0, W, 2 * C_HEADS, rope_q, TS, BF)
            sk = project(hs, w_in_odd, o, W, W, 2 * C_HEADS, rope, TS, BF)
            sv = project(hs, w_in_odd, o, 2 * W, W, 0, rope, TS, BF)
            os_ = [diff_attention(sq, sk, sv, diff_lambda, diff_subln, lambda_init, NS, TS, (cc_k, cc_v), o, True)]
            w_out, widx = w_out_odd_bf, o

        lg1, lb1 = ln_g4[:, 0], ln_b4[:, 0]
        lg2, lb2 = ln_g4[:, 1], ln_b4[:, 1]
        xp, hp, affp = outproj_ln_router(op, w_out, widx, xp, mods, l, 0, 0, TP, lg1, lb1, w_router_bf, alpha)
        xs, hs, affs = outproj_ln_router(os_, w_out, widx, xs, mods, l, 1, 1, TS, lg1, lb1, w_router_bf, alpha)
        nl = min(l + 1, depth - 1)
        groups = [dict(aff=affs, h=hs, x=xs, N=NS, T=TS, row0=1, per_req=1),
                  dict(aff=affp, h=hp, x=xp, N=NP, T=TP, row0=0, per_req=0)]
        (xs, hs), (xp, hp) = moe_layer(groups, mods, l, nl, lg2, lb2, w_gate, w_up, w_down, alpha)

    return (xp.reshape(NP, TP, D), xs.reshape(NS, TS, D),
            jnp.stack(new_a_k, axis=1), jnp.stack(new_a_v, axis=1),
            jnp.stack(new_b_k, axis=1), jnp.stack(new_b_v, axis=1),
            jnp.stack(new_c_k, axis=1), jnp.stack(new_c_v, axis=1))
```

```python
import functools
import math

import jax
import jax.numpy as jnp
from jax import lax
from jax.experimental import pallas as pl
from jax.experimental.pallas import tpu as pltpu

BF = jnp.bfloat16
F32 = jnp.float32

GRID_W = 64
HEAD_DIM = 128
A_Q_HEADS = 8
A_KV_HEADS = 2
A_GROUP = A_Q_HEADS // A_KV_HEADS
A_BLOCK = 128
B_HEADS = 8
NA_ROWS = 8
NA_COLS = 16
C_HEADS = 8
C_V_DIM = 2 * HEAD_DIM
N_EXPERTS = 16
EC_CAPACITY_FACTOR = 2
ROPE_BASE = 10000.0
LN_EPS = 1e-5
RMS_EPS = 1e-5
ATTN_SCALE = HEAD_DIM ** -0.5
A_WIDTH = (A_Q_HEADS + 2 * A_KV_HEADS) * HEAD_DIM
B_WIDTH = B_HEADS * HEAD_DIM

VMEM_LIMIT_BYTES = 52 * 1024 * 1024
TOKEN_BLOCK = 256
SLOT_CHUNK = 128


def _cparams(*sem):
    return pltpu.CompilerParams(dimension_semantics=sem, vmem_limit_bytes=VMEM_LIMIT_BYTES)


def _nt_dot(a, b):
    return lax.dot_general(a, b, (((1,), (1,)), ((), ())), preferred_element_type=F32)


def _dot(a, b):
    return jnp.dot(a, b, preferred_element_type=F32)


def _layer_norm(z, g, b):
    mu = jnp.mean(z, -1, keepdims=True)
    d = z - mu
    var = jnp.mean(d * d, -1, keepdims=True)
    return d * lax.rsqrt(var + LN_EPS) * g + b


def _adaln_kernel(c_ref, w_ref, b_ref, o_ref):
    c = c_ref[...]
    s = (c * jax.nn.sigmoid(c)).astype(BF)
    o_ref[...] = _dot(s, w_ref[...].astype(BF)) + b_ref[...]


def adaln_all(cond, w_mod, b_mod):
    L, D, D6 = w_mod.shape
    R = cond.shape[0]
    tn = D // 2
    return pl.pallas_call(
        _adaln_kernel,
        grid=(L, D6 // tn),
        in_specs=[
            pl.BlockSpec((R, D), lambda l, j: (0, 0)),
            pl.BlockSpec((None, D, tn), lambda l, j: (l, 0, j)),
            pl.BlockSpec((None, 1, tn), lambda l, j: (l, 0, j)),
        ],
        out_specs=pl.BlockSpec((None, R, tn), lambda l, j: (l, 0, j)),
        out_shape=jax.ShapeDtypeStruct((L, R, D6), F32),
        compiler_params=_cparams("parallel", "parallel"),
    )(cond, w_mod, b_mod.reshape(L, 1, D6))


def _modulate_kernel(x_ref, m_ref, o_ref):
    o_ref[...] = (x_ref[...] * (1.0 + m_ref[1:2, :]) + m_ref[0:1, :]).astype(o_ref.dtype)


def modulate_rows(x, mods, layer, row0, per_req, T):
    M, D = x.shape
    tt = min(T, 512)
    return pl.pallas_call(
        _modulate_kernel,
        grid=(M // tt,),
        in_specs=[
            pl.BlockSpec((tt, D), lambda i: (i, 0)),
            pl.BlockSpec((None, None, 6, D), lambda i: (layer, row0 + ((i * tt) // T) * per_req, 0, 0)),
        ],
        out_specs=pl.BlockSpec((tt, D), lambda i: (i, 0)),
        out_shape=jax.ShapeDtypeStruct((M, D), BF),
        compiler_params=_cparams("parallel"),
    )(x, mods)


def _swap32(x):
    lane = lax.broadcasted_iota(jnp.int32, (1, HEAD_DIM), 1)
    return jnp.where((lane % 64) < 32, pltpu.roll(x, HEAD_DIM - 32, 1), pltpu.roll(x, 32, 1))


def _proj_kernel(*refs, n_rope_heads, n_scaled_heads, heads_per_tile):
    if n_rope_heads:
        h_ref, w_ref, cos_ref, sin_ref, o_ref, wbf = refs
    else:
        h_ref, w_ref, o_ref, wbf = refs
    j = pl.program_id(0)
    i = pl.program_id(1)

    @pl.when(i == 0)
    def _():
        wbf[...] = w_ref[...].astype(BF)

    acc = _dot(h_ref[...], wbf[...])
    bounds = sorted({0, min(n_rope_heads, n_scaled_heads), n_rope_heads, n_scaled_heads, 1 << 30})
    for hh in range(heads_per_tile):
        sl = slice(hh * HEAD_DIM, (hh + 1) * HEAD_DIM)
        x = acc[:, sl]
        head = j * heads_per_tile + hh
        for lo, hi in zip(bounds[:-1], bounds[1:]):
            rot = lo < n_rope_heads
            scale = Q_PRESCALE if lo < n_scaled_heads else None

            def emit(x=x, sl=sl, rot=rot, scale=scale):
                y = x * cos_ref[...] + _swap32(x) * sin_ref[...] if rot else x
                o_ref[:, sl] = (y if scale is None else y * scale).astype(o_ref.dtype)

            if len(bounds) == 2:
                emit()
            else:
                pl.when(jnp.logical_and(head >= lo, head < hi))(emit)


def project(h, w, widx, col0, width, n_rope_heads, rope, T, out_dtype, n_scaled_heads=0):
    M, D = h.shape
    heads = width // HEAD_DIM
    hpt = max(k for k in range(1, 9) if heads % k == 0 and col0 % (k * HEAD_DIM) == 0)
    tn = hpt * HEAD_DIM
    tm = min(T if n_rope_heads else M, 1024)
    tpb = max(T // tm, 1)
    in_specs = [
        pl.BlockSpec((tm, D), lambda j, i: (i, 0)),
        pl.BlockSpec((None, D, tn), lambda j, i: (widx, 0, col0 // tn + j)),
    ]
    args = [h, w]
    if n_rope_heads:
        in_specs += [pl.BlockSpec((tm, HEAD_DIM), lambda j, i: (i % tpb, 0))] * 2
        args += list(rope)
    return pl.pallas_call(
        functools.partial(_proj_kernel, n_rope_heads=n_rope_heads, n_scaled_heads=n_scaled_heads, heads_per_tile=hpt),
        grid=(width // tn, M // tm),
        in_specs=in_specs,
        out_specs=pl.BlockSpec((tm, tn), lambda j, i: (i, j)),
        out_shape=jax.ShapeDtypeStruct((M, width), out_dtype),
        scratch_shapes=[pltpu.VMEM((D, tn), BF)],
        compiler_params=_cparams("parallel", "arbitrary"),
    )(*args)


def rope_tables(T):
    t = jnp.arange(T)
    row = (t // GRID_W).astype(F32)
    col = (t % GRID_W).astype(F32)
    axis_dim = HEAD_DIM // 2
    inv_freq = ROPE_BASE ** (-jnp.arange(0, axis_dim, 2, dtype=F32) / axis_dim)
    ar = row[:, None] * inv_freq
    ac = col[:, None] * inv_freq
    cos = jnp.concatenate([jnp.cos(ar), jnp.cos(ar), jnp.cos(ac), jnp.cos(ac)], -1)
    sin = jnp.concatenate([-jnp.sin(ar), jnp.sin(ar), -jnp.sin(ac), jnp.sin(ac)], -1)
    return cos, sin


LOG2E = math.log2(math.e)
Q_PRESCALE = ATTN_SCALE * LOG2E


def _exp2_parts(parts, sink=None):
    m = None
    for s in parts:
        mi = jnp.max(s, -1, keepdims=True)
        m = mi if m is None else jnp.maximum(m, mi)
    if sink is not None:
        m = jnp.maximum(m, sink)
    es = [jnp.exp2(s - m) for s in parts]
    den = None
    for e in es:
        di = jnp.sum(e, -1, keepdims=True)
        den = di if den is None else den + di
    if sink is not None:
        den = den + jnp.exp2(sink - m)
    return es, 1.0 / den


def _softmax_parts(parts, sink=None):
    m = None
    for s in parts:
        mi = jnp.max(s, -1, keepdims=True)
        m = mi if m is None else jnp.maximum(m, mi)
    if sink is not None:
        m = jnp.maximum(m, sink)
    es = [jnp.exp(s - m) for s in parts]
    den = None
    for e in es:
        di = jnp.sum(e, -1, keepdims=True)
        den = di if den is None else den + di
    if sink is not None:
        den = den + jnp.exp(sink - m)
    inv = 1.0 / den
    return [e * inv for e in es]


def _ctx_even_kernel(sink_ref, a_ref, qb_ref, kb_ref, vb_ref, o_ref, *, sink_base):
    T = a_ref.shape[0]
    for kv in range(A_KV_HEADS):
        k = a_ref[:, (A_Q_HEADS + kv) * HEAD_DIM:(A_Q_HEADS + kv + 1) * HEAD_DIM].astype(BF)
        v = a_ref[:, (A_Q_HEADS + A_KV_HEADS + kv) * HEAD_DIM:(A_Q_HEADS + A_KV_HEADS + kv + 1) * HEAD_DIM].astype(BF)
        for g in range(A_GROUP):
            hq = kv * A_GROUP + g
            q = a_ref[:, hq * HEAD_DIM:(hq + 1) * HEAD_DIM].astype(BF)
            s = _nt_dot(q, k) * ATTN_SCALE
            sink = jnp.full((T, 1), sink_ref[sink_base + hq], F32)
            (p,) = _softmax_parts([s], sink)
            o_ref[:, hq * HEAD_DIM:(hq + 1) * HEAD_DIM] = _dot(p.astype(BF), v).astype(o_ref.dtype)
    for hb in range(B_HEADS):
        sl = slice(hb * HEAD_DIM, (hb + 1) * HEAD_DIM)
        s = _nt_dot(qb_ref[:, sl].astype(BF), kb_ref[:, sl].astype(BF)) * ATTN_SCALE
        (p,) = _softmax_parts([s])
        o_ref[:, A_Q_HEADS * HEAD_DIM + hb * HEAD_DIM:A_Q_HEADS * HEAD_DIM + (hb + 1) * HEAD_DIM] = _dot(
            p.astype(BF), vb_ref[:, sl].astype(BF)).astype(o_ref.dtype)


def ctx_even_attention(arr_a, qb, kb, vb, sinks, e, N, T):
    M = N * T
    return pl.pallas_call(
        functools.partial(_ctx_even_kernel, sink_base=e * A_Q_HEADS),
        grid=(N,),
        in_specs=[
            pl.BlockSpec(memory_space=pltpu.SMEM),
            pl.BlockSpec((T, A_WIDTH), lambda n: (n, 0)),
            pl.BlockSpec((T, B_WIDTH), lambda n: (n, 0)),
            pl.BlockSpec((T, B_WIDTH), lambda n: (n, 0)),
            pl.BlockSpec((T, B_WIDTH), lambda n: (n, 0)),
        ],
        out_specs=pl.BlockSpec((T, A_Q_HEADS * HEAD_DIM + B_WIDTH), lambda n: (n, 0)),
        out_shape=jax.ShapeDtypeStruct((M, A_Q_HEADS * HEAD_DIM + B_WIDTH), BF),
        compiler_params=_cparams("parallel"),
    )(sinks, arr_a, qb, kb, vb)


def _win_kernel(sink_ref, q_ref, kvp_ref, kvc_ref, kvn_ref, kc_ref, vc_ref, o_ref, *, sink_base, nb):
    i = pl.program_id(1)
    R = A_GROUP * A_BLOCK
    qq = lax.broadcasted_iota(jnp.int32, (R, 1), 0) % A_BLOCK
    kk = lax.broadcasted_iota(jnp.int32, (1, A_BLOCK), 1)
    ok_prev = jnp.logical_and(kk >= qq, i > 0)
    ok_next = jnp.logical_and(kk <= qq, i < nb - 1)
    for kv in range(A_KV_HEADS):
        ksl = slice(kv * HEAD_DIM, (kv + 1) * HEAD_DIM)
        vsl = slice((A_KV_HEADS + kv) * HEAD_DIM, (A_KV_HEADS + kv + 1) * HEAD_DIM)
        q = jnp.concatenate(
            [q_ref[:, (kv * A_GROUP + g) * HEAD_DIM:(kv * A_GROUP + g + 1) * HEAD_DIM] for g in range(A_GROUP)], axis=0)
        sink = jnp.concatenate(
            [jnp.full((A_BLOCK, 1), sink_ref[sink_base + kv * A_GROUP + g] * LOG2E, F32) for g in range(A_GROUP)],
            axis=0)
        s_p = jnp.where(ok_prev, _nt_dot(q, kvp_ref[:, ksl]), -jnp.inf)
        s_c = _nt_dot(q, kvc_ref[:, ksl])
        s_n = jnp.where(ok_next, _nt_dot(q, kvn_ref[:, ksl]), -jnp.inf)
        s_x = _nt_dot(q, kc_ref[:, ksl].astype(BF))
        (e_p, e_c, e_n, e_x), inv = _exp2_parts([s_p, s_c, s_n, s_x], sink)
        o = (_dot(e_p.astype(BF), kvp_ref[:, vsl]) + _dot(e_c.astype(BF), kvc_ref[:, vsl])
             + _dot(e_n.astype(BF), kvn_ref[:, vsl]) + _dot(e_x.astype(BF), vc_ref[:, ksl].astype(BF))) * inv
        for g in range(A_GROUP):
            hq = kv * A_GROUP + g
            o_ref[:, hq * HEAD_DIM:(hq + 1) * HEAD_DIM] = o[g * A_BLOCK:(g + 1) * A_BLOCK].astype(o_ref.dtype)


def window_attention(arr_a, cache_k, cache_v, sinks, e, N, T):
    M = N * T
    nb = T // A_BLOCK
    L = cache_k.shape[2]
    QW = A_Q_HEADS * HEAD_DIM
    KVW = 2 * A_KV_HEADS * HEAD_DIM
    kvblk = QW // KVW
    return pl.pallas_call(
        functools.partial(_win_kernel, sink_base=e * A_Q_HEADS, nb=nb),
        grid=(N, nb),
        in_specs=[
            pl.BlockSpec(memory_space=pltpu.SMEM),
            pl.BlockSpec((A_BLOCK, QW), lambda n, i: (n * nb + i, 0)),
            pl.BlockSpec((A_BLOCK, KVW), lambda n, i: (n * nb + jnp.maximum(i - 1, 0), kvblk)),
            pl.BlockSpec((A_BLOCK, KVW), lambda n, i: (n * nb + i, kvblk)),
            pl.BlockSpec((A_BLOCK, KVW), lambda n, i: (n * nb + jnp.minimum(i + 1, nb - 1), kvblk)),
            pl.BlockSpec((None, None, L, A_KV_HEADS * HEAD_DIM), lambda n, i: (n, e, 0, 0)),
            pl.BlockSpec((None, None, L, A_KV_HEADS * HEAD_DIM), lambda n, i: (n, e, 0, 0)),
        ],
        out_specs=pl.BlockSpec((A_BLOCK, QW), lambda n, i: (n * nb + i, 0)),
        out_shape=jax.ShapeDtypeStruct((M, QW), BF),
        compiler_params=_cparams("parallel", "parallel"),
    )(sinks, arr_a, arr_a, arr_a, arr_a, cache_k, cache_v)


NB_Q_ROWS = TOKEN_BLOCK // GRID_W
NB_WIN_BLOCKS = 3


def _toeplitz_kernel(rpb_ref, o_ref):
    K = rpb_ref.shape[1]
    W2 = GRID_W * GRID_W
    col = lax.broadcasted_iota(jnp.int32, (K, W2), 1)
    cq = col // GRID_W
    ck = col % GRID_W
    dc = jnp.clip(ck - cq + (NA_COLS - 1), 0, 2 * NA_COLS - 2)
    onehot = (lax.broadcasted_iota(jnp.int32, (K, W2), 0) == dc).astype(BF)
    t = rpb_ref[...]
    hi = t.astype(BF)
    r1 = t - hi.astype(F32)
    mid = r1.astype(BF)
    lo = (r1 - mid.astype(F32)).astype(BF)
    val = _dot(hi, onehot) + _dot(mid, onehot) + _dot(lo, onehot)
    c0 = jnp.clip(cq[0:1] - NA_COLS // 2, 0, GRID_W - NA_COLS)
    ok = jnp.logical_and(ck[0:1] >= c0, ck[0:1] < c0 + NA_COLS)
    o_ref[...] = jnp.where(ok, val * LOG2E, -jnp.inf)


def toeplitz_bias(rpb):
    G, H, NR, NC = rpb.shape
    R = G * H * NR
    K = 32
    flat = jnp.pad(rpb.reshape(R, NC), ((0, -R % 8), (0, K - NC)))
    out = pl.pallas_call(
        _toeplitz_kernel,
        out_shape=jax.ShapeDtypeStruct((flat.shape[0], GRID_W * GRID_W), F32),
        compiler_params=pltpu.CompilerParams(vmem_limit_bytes=VMEM_LIMIT_BYTES),
    )(flat)
    return out[:R].reshape(G, H, NR, GRID_W, GRID_W)


def neighbourhood_bias(tiles, T):
    rows = T // GRID_W
    nblk = T // TOKEN_BLOCK
    H = tiles.shape[0]
    masked = jnp.full((H, GRID_W, GRID_W), -jnp.inf, F32)
    out = []
    for blk in (0, 1, nblk - 1):
        w0 = min(max(blk - 1, 0), nblk - NB_WIN_BLOCKS) * NB_Q_ROWS
        qrows = []
        for ql in range(NB_Q_ROWS):
            rq = blk * NB_Q_ROWS + ql
            r0 = min(max(rq - NA_ROWS // 2, 0), rows - NA_ROWS)
            krow = []
            for kl in range(NB_WIN_BLOCKS * NB_Q_ROWS):
                rk = w0 + kl
                krow.append(tiles[:, rk - rq + NA_ROWS - 1] if r0 <= rk < r0 + NA_ROWS else masked)
            qrows.append(jnp.concatenate(krow, axis=-1))
        out.append(jnp.concatenate(qrows, axis=-2))
    return jnp.stack(out, axis=0)


def _nbr_kernel(q_ref, k0_ref, k1_ref, k2_ref, v0_ref, v1_ref, v2_ref, kc_ref, vc_ref, bias_ref, o_ref):
    for h in range(B_HEADS):
        sl = slice(h * HEAD_DIM, (h + 1) * HEAD_DIM)
        q = q_ref[:, sl]
        parts = []
        for w, k_ref in enumerate((k0_ref, k1_ref, k2_ref)):
            b = bias_ref[h, :, w * TOKEN_BLOCK:(w + 1) * TOKEN_BLOCK]
            parts.append(_nt_dot(q, k_ref[:, sl]) + b)
        parts.append(_nt_dot(q, kc_ref[:, sl].astype(BF)))
        (e0, e1, e2, ex), inv = _exp2_parts(parts)
        o = (_dot(e0.astype(BF), v0_ref[:, sl]) + _dot(e1.astype(BF), v1_ref[:, sl])
             + _dot(e2.astype(BF), v2_ref[:, sl]) + _dot(ex.astype(BF), vc_ref[:, sl].astype(BF))) * inv
        o_ref[:, sl] = o.astype(o_ref.dtype)


def neighbourhood_attention(qb, kb, vb, cache_k, cache_v, bias3, e, N, T):
    M = N * T
    nblk = T // TOKEN_BLOCK
    L = cache_k.shape[2]

    def win(w):
        return lambda n, i: (n * nblk + jnp.clip(i - 1, 0, nblk - NB_WIN_BLOCKS) + w, 0)

    def case(n, i):
        return (jnp.where(i == 0, 0, jnp.where(i == nblk - 1, 2, 1)), 0, 0, 0)

    blk = pl.BlockSpec((TOKEN_BLOCK, B_WIDTH), lambda n, i: (n * nblk + i, 0))
    return pl.pallas_call(
        _nbr_kernel,
        grid=(N, nblk),
        in_specs=[
            blk,
            pl.BlockSpec((TOKEN_BLOCK, B_WIDTH), win(0)),
            pl.BlockSpec((TOKEN_BLOCK, B_WIDTH), win(1)),
            pl.BlockSpec((TOKEN_BLOCK, B_WIDTH), win(2)),
            pl.BlockSpec((TOKEN_BLOCK, B_WIDTH), win(0)),
            pl.BlockSpec((TOKEN_BLOCK, B_WIDTH), win(1)),
            pl.BlockSpec((TOKEN_BLOCK, B_WIDTH), win(2)),
            pl.BlockSpec((None, None, L, B_WIDTH), lambda n, i: (n, e, 0, 0)),
            pl.BlockSpec((None, None, L, B_WIDTH), lambda n, i: (n, e, 0, 0)),
            pl.BlockSpec((None, B_HEADS, TOKEN_BLOCK, NB_WIN_BLOCKS * TOKEN_BLOCK), case),
        ],
        out_specs=blk,
        out_shape=jax.ShapeDtypeStruct((M, B_WIDTH), BF),
        compiler_params=_cparams("parallel", "arbitrary"),
    )(qb, kb, kb, kb, vb, vb, vb, cache_k, cache_v, bias3)


DIFF_ROW_SPLIT = 2


def _diff_kernel(*refs, has_ctx, lambda_init, q_prescaled):
    if has_ctx:
        lam_ref, g_ref, q_ref, k_ref, v_ref, kc_ref, vc_ref, o_ref = refs
    else:
        lam_ref, g_ref, q_ref, k_ref, v_ref, o_ref = refs
    lp = lam_ref[...]
    lam = (jnp.exp(jnp.sum(lp[0:1] * lp[1:2], -1, keepdims=True))
           - jnp.exp(jnp.sum(lp[2:3] * lp[3:4], -1, keepdims=True)) + lambda_init)
    tq = q_ref.shape[0]
    sub = tq // DIFF_ROW_SPLIT
    for r0 in range(0, tq, sub):
        rows = slice(r0, r0 + sub)
        nums, invs = [], []
        for half in range(2):
            sl = slice(half * HEAD_DIM, (half + 1) * HEAD_DIM)
            if q_prescaled:
                q = q_ref[rows, sl]
            else:
                q = (q_ref[rows, sl].astype(F32) * Q_PRESCALE).astype(BF)
            parts = [_nt_dot(q, k_ref[:, sl].astype(BF))]
            if has_ctx:
                parts.append(_nt_dot(q, kc_ref[:, sl].astype(BF)))
            es, inv = _exp2_parts(parts)
            nums.append(es)
            invs.append(inv)
        c0 = invs[0]
        c1 = lam * invs[1]
        a = (nums[0][0] * c0 - nums[1][0] * c1).astype(BF)
        o = _dot(a, v_ref[...].astype(BF))
        if has_ctx:
            ac = (nums[0][1] * c0 - nums[1][1] * c1).astype(BF)
            o = o + _dot(ac, vc_ref[...].astype(BF))
        of = o * lax.rsqrt(jnp.mean(o * o, -1, keepdims=True) + RMS_EPS)
        o_ref[rows, :] = ((of * g_ref[...]) * (1.0 - lambda_init)).astype(o_ref.dtype)


def diff_attention(q, k, v, lam_p, subln, lambda_init, N, T, ctx=None, widx=0, q_prescaled=False):
    M = N * T
    tq = min(T, 256)
    nq = T // tq
    W = C_V_DIM
    in_specs = [
        pl.BlockSpec((None, 4, HEAD_DIM), lambda n, h, i: (widx, 0, 0)),
        pl.BlockSpec((None, 1, W), lambda n, h, i: (widx, 0, 0)),
        pl.BlockSpec((tq, W), lambda n, h, i: (n * nq + i, h)),
        pl.BlockSpec((T, W), lambda n, h, i: (n, h)),
        pl.BlockSpec((T, W), lambda n, h, i: (n, h)),
    ]
    args = [lam_p, subln.reshape(subln.shape[0], 1, W), q, k, v]
    if ctx is not None:
        L = ctx[0].shape[2]
        in_specs += [pl.BlockSpec((None, None, L, W), lambda n, h, i: (n, widx, 0, h))] * 2
        args += list(ctx)
    return pl.pallas_call(
        functools.partial(_diff_kernel, has_ctx=ctx is not None, lambda_init=lambda_init, q_prescaled=q_prescaled),
        grid=(N, C_HEADS, nq),
        in_specs=in_specs,
        out_specs=pl.BlockSpec((tq, W), lambda n, h, i: (n * nq + i, h)),
        out_shape=jax.ShapeDtypeStruct((M, C_HEADS * W), BF),
        compiler_params=_cparams("parallel", "parallel", "arbitrary"),
    )(*args)


def _outproj_kernel(*refs, n_in, alpha):
    o_refs = refs[:n_in]
    w_refs = refs[n_in:2 * n_in]
    x_ref, m_ref, g_ref, b_ref, wr_ref, xo_ref, h_ref, aff_ref = refs[2 * n_in:]
    y = None
    for o_r, w_r in zip(o_refs, w_refs):
        t = _dot(o_r[...], w_r[...])
        y = t if y is None else y + t
    z = alpha * x_ref[...] + m_ref[2:3, :] * y
    xn = _layer_norm(z, g_ref[...], b_ref[...])
    xo_ref[...] = xn
    hb = (xn * (1.0 + m_ref[4:5, :]) + m_ref[3:4, :]).astype(BF)
    h_ref[...] = hb
    logits = _dot(hb, wr_ref[...])
    mx = jnp.max(logits, -1, keepdims=True)
    ex = jnp.exp(logits - mx)
    aff_ref[...] = ex / jnp.sum(ex, -1, keepdims=True)


def outproj_ln_router(os_, w_out, widx, x, mods, layer, row0, per_req, T, ln_g, ln_b, w_router, alpha):
    M, D = x.shape
    tm = min(T if per_req else M, 512)
    E = w_router.shape[-1]
    in_specs, args, koff = [], [], 0
    for o in os_:
        in_specs.append(pl.BlockSpec((tm, o.shape[1]), lambda i: (i, 0)))
        args.append(o)
    for o in os_:
        wk = o.shape[1]
        in_specs.append(pl.BlockSpec((None, wk, D), functools.partial(lambda i, kb: (widx, kb, 0), kb=koff // wk)))
        args.append(w_out)
        koff += wk
    in_specs += [
        pl.BlockSpec((tm, D), lambda i: (i, 0)),
        pl.BlockSpec((None, None, 6, D), lambda i: (layer, row0 + ((i * tm) // T) * per_req, 0, 0)),
        pl.BlockSpec((None, 1, D), lambda i: (layer, 0, 0)),
        pl.BlockSpec((None, 1, D), lambda i: (layer, 0, 0)),
        pl.BlockSpec((None, D, E), lambda i: (layer, 0, 0)),
    ]
    args += [x, mods, ln_g, ln_b, w_router]
    return pl.pallas_call(
        functools.partial(_outproj_kernel, n_in=len(os_), alpha=alpha),
        grid=(M // tm,),
        in_specs=in_specs,
        out_specs=[
            pl.BlockSpec((tm, D), lambda i: (i, 0)),
            pl.BlockSpec((tm, D), lambda i: (i, 0)),
            pl.BlockSpec((tm, E), lambda i: (i, 0)),
        ],
        out_shape=[
            jax.ShapeDtypeStruct((M, D), F32),
            jax.ShapeDtypeStruct((M, D), BF),
            jax.ShapeDtypeStruct((M, E), F32),
        ],
        compiler_params=_cparams("parallel"),
    )(*args)


def _route_kernel(aff_ref, scol_ref, srow_ref, cnt_ref, a3_ref, *, cap, nblk):
    E = N_EXPERTS
    B = aff_ref.shape[0] // nblk
    a = aff_ref[...]
    bits = pltpu.bitcast(a, jnp.int32)

    def search(it, ans):
        cand = ans | jnp.left_shift(jnp.int32(1), 30 - it)
        cnt = jnp.sum((bits >= cand).astype(F32), axis=0, keepdims=True)
        return jnp.where(cnt >= cap, cand, ans)

    thr = lax.fori_loop(0, 31, search, jnp.zeros((1, E), jnp.int32))
    gt = bits > thr
    eq = bits == thr
    need = cap - jnp.sum(gt.astype(F32), axis=0, keepdims=True)

    r = lax.broadcasted_iota(jnp.int32, (B, B), 0)
    c = lax.broadcasted_iota(jnp.int32, (B, B), 1)
    lower = (c < r).astype(BF)
    upper = (r < c).astype(BF)
    eye = (lax.broadcasted_iota(jnp.int32, (E, E), 0) == lax.broadcasted_iota(jnp.int32, (E, E), 1)).astype(BF)

    carry = jnp.zeros((1, E), F32)
    sels = []
    for b in range(nblk):
        sl = slice(b * B, (b + 1) * B)
        eqb = eq[sl].astype(BF)
        pref = _dot(lower, eqb) + carry
        carry = carry + jnp.sum(eqb.astype(F32), axis=0, keepdims=True)
        sels.append(jnp.logical_or(gt[sl], jnp.logical_and(eq[sl], pref < need)))

    carry = jnp.zeros((1, E), F32)
    carry_t = jnp.zeros((E, 1), F32)
    for b in range(nblk):
        sl = slice(b * B, (b + 1) * B)
        selb = sels[b].astype(BF)
        cnt_ref[b:b + 1, :] = carry.astype(jnp.int32)
        pos = _dot(lower, selb) + carry
        scol_ref[sl, :] = jnp.where(sels[b], pos, -1.0)
        carry = carry + jnp.sum(selb.astype(F32), axis=0, keepdims=True)
        sel_t = _nt_dot(eye, selb)
        pos_t = _dot(sel_t.astype(BF), upper) + carry_t
        srow_ref[b] = jnp.where(sel_t > 0.5, pos_t, -1.0)
        carry_t = carry_t + jnp.sum(sel_t, axis=1, keepdims=True)
    cnt_ref[nblk:nblk + 1, :] = carry.astype(jnp.int32)

    hi = a.astype(BF)
    r1 = a - hi.astype(F32)
    mid = r1.astype(BF)
    lo = (r1 - mid.astype(F32)).astype(BF)
    lane = lax.broadcasted_iota(jnp.int32, (E, HEAD_DIM), 1)
    row = lax.broadcasted_iota(jnp.int32, (E, HEAD_DIM), 0)
    a3 = (_dot(hi, (lane == row).astype(BF)) + _dot(mid, (lane == row + E).astype(BF))
          + _dot(lo, (lane == row + 2 * E).astype(BF)))
    a3_ref[...] = a3.astype(BF)


def _gate_from_split(g3, expert):
    lane = lax.broadcasted_iota(jnp.int32, g3.shape, 1)
    pick = jnp.logical_and(lane % N_EXPERTS == expert, lane < 3 * N_EXPERTS)
    return jnp.sum(jnp.where(pick, g3, 0.0), axis=1, keepdims=True)


def route(aff, N, T):
    E = N_EXPERTS
    cap = EC_CAPACITY_FACTOR * T // E
    B = min(T, TOKEN_BLOCK)
    nblk = T // B
    return pl.pallas_call(
        functools.partial(_route_kernel, cap=cap, nblk=nblk),
        grid=(N,),
        in_specs=[pl.BlockSpec((T, E), lambda n: (n, 0))],
        out_specs=[
            pl.BlockSpec((None, T, E), lambda n: (n, 0, 0)),
            pl.BlockSpec((None, nblk, E, B), lambda n: (n, 0, 0, 0)),
            pl.BlockSpec((None, nblk + 1, E), lambda n: (n, 0, 0)),
            pl.BlockSpec((None, T, HEAD_DIM), lambda n: (n, 0, 0)),
        ],
        out_shape=[
            jax.ShapeDtypeStruct((N, T, E), F32),
            jax.ShapeDtypeStruct((N, nblk, E, B), F32),
            jax.ShapeDtypeStruct((N, nblk + 1, E), jnp.int32),
            jax.ShapeDtypeStruct((N, T, HEAD_DIM), BF),
        ],
        compiler_params=_cparams("parallel"),
    )(aff)


def _gather_small_kernel(h_ref, srow_ref, a3_ref, x_all_ref, g_all_ref, x_ref, g_ref, *, cap):
    del x_all_ref, g_all_ref
    E = N_EXPERTS
    S = E * cap
    T = h_ref.shape[0]
    sr = srow_ref[...].astype(BF)
    rep_t = (lax.broadcasted_iota(jnp.int32, (S, E), 0) // cap == lax.broadcasted_iota(jnp.int32, (S, E), 1)).astype(BF)
    slot_of_row = (lax.broadcasted_iota(jnp.int32, (S, 1), 0) % cap).astype(F32)
    onehot = (_dot(rep_t, sr) == slot_of_row).astype(BF)
    x = _dot(onehot, h_ref[...])
    for e in range(E):
        x_ref[e] = x[e * cap:(e + 1) * cap].astype(x_ref.dtype)
    gate = _gate_from_split(_dot(onehot, a3_ref[...]), lax.broadcasted_iota(jnp.int32, (S, 1), 0) // cap)
    gb = jnp.broadcast_to(gate, (S, HEAD_DIM))
    for e in range(E):
        g_ref[e] = gb[e * cap:(e + 1) * cap]


def gather_small(h, srow, a3, x_all, g_all, row0, N, T):
    E = N_EXPERTS
    cap = EC_CAPACITY_FACTOR * T // E
    D = h.shape[1]
    assert row0 % cap == 0
    rb0 = row0 // cap
    return pl.pallas_call(
        functools.partial(_gather_small_kernel, cap=cap),
        grid=(N,),
        in_specs=[
            pl.BlockSpec((T, D), lambda n: (n, 0)),
            pl.BlockSpec((None, None, E, T), lambda n: (n, 0, 0, 0)),
            pl.BlockSpec((None, T, HEAD_DIM), lambda n: (n, 0, 0)),
            pl.BlockSpec(memory_space=pl.ANY),
            pl.BlockSpec(memory_space=pl.ANY),
        ],
        out_specs=[
            pl.BlockSpec((E, cap, D), lambda n: (0, rb0 + n, 0)),
            pl.BlockSpec((E, cap, HEAD_DIM), lambda n: (0, rb0 + n, 0)),
        ],
        out_shape=[
            jax.ShapeDtypeStruct(x_all.shape, x_all.dtype),
            jax.ShapeDtypeStruct(g_all.shape, g_all.dtype),
        ],
        input_output_aliases={3: 0, 4: 1},
        compiler_params=_cparams("parallel"),
    )(h, srow, a3, x_all, g_all)


GATHER_COLS = 512


def _gather_big_kernel(cnt_ref, h_ref, srow_ref, a3_ref, x_ref, g_ref, *, cap, nblk):
    E = N_EXPERTS
    W = min(SLOT_CHUNK, cap)
    n = pl.program_id(0)
    d = pl.program_id(1)
    b = pl.program_id(2)

    @pl.when(b == 0)
    def _():
        x_ref[...] = jnp.zeros_like(x_ref)

    @pl.when(jnp.logical_and(b == 0, d == 0))
    def _():
        g_ref[...] = jnp.zeros_like(g_ref)

    row = lax.broadcasted_iota(jnp.int32, (W, 1), 0)
    starts, ends, pieces = [], [], []
    for e in range(E):
        base = (n * E + e) * (nblk + 1) + b
        ws = pl.multiple_of(jnp.minimum((cnt_ref[base] // 16) * 16, cap - W), 16)
        pieces.append((srow_ref[e:e + 1, :] == (ws + row).astype(F32)).astype(BF))
        starts.append(ws)
        ends.append(cnt_ref[base + 1])
    onehot = jnp.concatenate(pieces, axis=0)
    prod = _dot(onehot, h_ref[...])
    for e in range(E):
        x_ref[e, pl.ds(starts[e], W), :] += prod[e * W:(e + 1) * W].astype(x_ref.dtype)

    @pl.when(d == 0)
    def _():
        split = _dot(onehot, a3_ref[...])
        for e in range(E):
            g_ref[e, pl.ds(starts[e], W), :] += split[e * W:(e + 1) * W]

    for e in range(E):
        def more(k, carry, e=e):
            first = starts[e] + k * W
            ws = pl.multiple_of(jnp.minimum(first, cap - W), 16)
            ids = ws + row
            piece = jnp.logical_and(srow_ref[e:e + 1, :] == ids.astype(F32), ids >= first).astype(BF)
            x_ref[e, pl.ds(ws, W), :] += _dot(piece, h_ref[...]).astype(x_ref.dtype)

            @pl.when(d == 0)
            def _():
                g_ref[e, pl.ds(ws, W), :] += _dot(piece, a3_ref[...])

            return carry

        lax.fori_loop(1, (ends[e] - starts[e] + W - 1) // W, more, 0)

    @pl.when(jnp.logical_and(b == nblk - 1, d == 0))
    def _():
        for e in range(E):
            g_ref[e] = jnp.broadcast_to(_gate_from_split(g_ref[e], e), (cap, HEAD_DIM))


def gather_big(h, srow, cnt_flat, a3, rows_total, N, T):
    E = N_EXPERTS
    cap = EC_CAPACITY_FACTOR * T // E
    D = h.shape[1]
    nblk = T // TOKEN_BLOCK
    Dc = min(D, GATHER_COLS)
    assert cap % 16 == 0
    return pl.pallas_call(
        functools.partial(_gather_big_kernel, cap=cap, nblk=nblk),
        grid_spec=pltpu.PrefetchScalarGridSpec(
            num_scalar_prefetch=1,
            grid=(N, D // Dc, nblk),
            in_specs=[
                pl.BlockSpec((TOKEN_BLOCK, Dc), lambda n, d, b, c: (n * nblk + b, d)),
                pl.BlockSpec((None, None, E, TOKEN_BLOCK), lambda n, d, b, c: (n, b, 0, 0)),
                pl.BlockSpec((None, TOKEN_BLOCK, HEAD_DIM), lambda n, d, b, c: (n, b, 0)),
            ],
            out_specs=[
                pl.BlockSpec((E, cap, Dc), lambda n, d, b, c: (0, n, d)),
                pl.BlockSpec((E, cap, HEAD_DIM), lambda n, d, b, c: (0, n, 0)),
            ],
        ),
        out_shape=[
            jax.ShapeDtypeStruct((E, rows_total, D), BF),
            jax.ShapeDtypeStruct((E, rows_total, HEAD_DIM), F32),
        ],
        compiler_params=_cparams("parallel", "arbitrary", "arbitrary"),
    )(cnt_flat, h, srow, a3)


def _gate_up_kernel(x_ref, wg_ref, wu_ref, o_ref, wg_bf, wu_bf):
    @pl.when(pl.program_id(2) == 0)
    def _():
        wg_bf[...] = wg_ref[...].astype(BF)
        wu_bf[...] = wu_ref[...].astype(BF)

    x = x_ref[...]
    a = _dot(x, wg_bf[...])
    u = _dot(x, wu_bf[...])
    o_ref[...] = ((a * jax.nn.sigmoid(a)) * u).astype(o_ref.dtype)


def _row_tile(R, limit):
    return max(t for t in range(16, min(R, limit) + 1, 16) if R % t == 0)


def expert_gate_up(x, w_gate, w_up, layer):
    E, R, D = x.shape
    F = w_gate.shape[-1]
    tr = _row_tile(R, 1536)
    tf = min(F, 512)
    return pl.pallas_call(
        _gate_up_kernel,
        grid=(E, F // tf, R // tr),
        in_specs=[
            pl.BlockSpec((None, tr, D), lambda e, f, r: (e, r, 0)),
            pl.BlockSpec((None, None, D, tf), lambda e, f, r: (layer, e, 0, f)),
            pl.BlockSpec((None, None, D, tf), lambda e, f, r: (layer, e, 0, f)),
        ],
        out_specs=pl.BlockSpec((None, tr, tf), lambda e, f, r: (e, r, f)),
        out_shape=jax.ShapeDtypeStruct((E, R, F), BF),
        scratch_shapes=[pltpu.VMEM((D, tf), BF), pltpu.VMEM((D, tf), BF)],
        compiler_params=_cparams("parallel", "parallel", "arbitrary"),
    )(x, w_gate, w_up)


def _down_kernel(h_ref, w_ref, g_ref, o_ref, w_bf):
    @pl.when(pl.program_id(2) == 0)
    def _():
        w_bf[...] = w_ref[...].astype(BF)

    o_ref[...] = (_dot(h_ref[...], w_bf[...]) * g_ref[:, 0:1]).astype(o_ref.dtype)


def expert_down(h, w_down, gate, layer):
    E, R, F = h.shape
    D = w_down.shape[-1]
    tr = _row_tile(R, 1536)
    td = min(D, 1024)
    return pl.pallas_call(
        _down_kernel,
        grid=(E, D // td, R // tr),
        in_specs=[
            pl.BlockSpec((None, tr, F), lambda e, d, r: (e, r, 0)),
            pl.BlockSpec((None, None, F, td), lambda e, d, r: (layer, e, 0, d)),
            pl.BlockSpec((None, tr, HEAD_DIM), lambda e, d, r: (e, r, 0)),
        ],
        out_specs=pl.BlockSpec((None, tr, td), lambda e, d, r: (e, r, d)),
        out_shape=jax.ShapeDtypeStruct((E, R, D), BF),
        scratch_shapes=[pltpu.VMEM((F, td), BF)],
        compiler_params=_cparams("parallel", "parallel", "arbitrary"),
    )(h, w_down, gate)


def _residual_epilogue(f, x_ref, m_ref, mn_ref, g_ref, b_ref, xo_ref, h_ref, alpha):
    z = alpha * x_ref[...] + m_ref[5:6, :] * f
    xn = _layer_norm(z, g_ref[...], b_ref[...])
    xo_ref[...] = xn
    h_ref[...] = (xn * (1.0 + mn_ref[1:2, :]) + mn_ref[0:1, :]).astype(h_ref.dtype)


def _scatter_small_kernel(y_ref, scol_ref, x_ref, m_ref, mn_ref, g_ref, b_ref, xo_ref, h_ref, *, cap, alpha):
    E = N_EXPERTS
    S = E * cap
    sc = scol_ref[...].astype(BF)
    rep = (lax.broadcasted_iota(jnp.int32, (E, S), 0) == lax.broadcasted_iota(jnp.int32, (E, S), 1) // cap).astype(BF)
    slot_of_col = (lax.broadcasted_iota(jnp.int32, (1, S), 1) % cap).astype(F32)
    onehot_t = (_dot(sc, rep) == slot_of_col).astype(BF)
    y = jnp.concatenate([y_ref[e] for e in range(E)], axis=0)
    f = _dot(onehot_t, y)
    _residual_epilogue(f, x_ref, m_ref, mn_ref, g_ref, b_ref, xo_ref, h_ref, alpha)


def scatter_small(y, y_row0, scol, x, mods, layer, next_layer, row0, per_req, ln_g, ln_b, N, T, alpha):
    E = N_EXPERTS
    cap = EC_CAPACITY_FACTOR * T // E
    D = x.shape[1]
    assert y_row0 % cap == 0
    return pl.pallas_call(
        functools.partial(_scatter_small_kernel, cap=cap, alpha=alpha),
        grid=(N,),
        in_specs=[
            pl.BlockSpec((E, cap, D), lambda n: (0, y_row0 // cap + n, 0)),
            pl.BlockSpec((None, T, E), lambda n: (n, 0, 0)),
            pl.BlockSpec((T, D), lambda n: (n, 0)),
            pl.BlockSpec((None, None, 6, D), lambda n: (layer, row0 + n * per_req, 0, 0)),
            pl.BlockSpec((None, None, 6, D), lambda n: (next_layer, row0 + n * per_req, 0, 0)),
            pl.BlockSpec((None, 1, D), lambda n: (layer, 0, 0)),
            pl.BlockSpec((None, 1, D), lambda n: (layer, 0, 0)),
        ],
        out_specs=[pl.BlockSpec((T, D), lambda n: (n, 0)), pl.BlockSpec((T, D), lambda n: (n, 0))],
        out_shape=[jax.ShapeDtypeStruct((N * T, D), F32), jax.ShapeDtypeStruct((N * T, D), BF)],
        compiler_params=_cparams("parallel"),
    )(y, scol, x, mods, mods, ln_g, ln_b)


def _scatter_big_kernel(cnt_ref, y_ref, scol_ref, x_ref, m_ref, mn_ref, g_ref, b_ref, xo_ref, h_ref, acc,
                        *, nblk, tiles, alpha):
    E = N_EXPERTS
    C = SLOT_CHUNK
    B = TOKEN_BLOCK
    G = y_ref.shape[0]
    cap = y_ref.shape[1]
    WIN = min(C, cap)
    n = pl.program_id(0)
    tb = pl.program_id(1)
    g = pl.program_id(2)

    @pl.when(g == 0)
    def _():
        acc[...] = jnp.zeros_like(acc)

    lane = lax.broadcasted_iota(jnp.int32, (B, E), 1)
    for bl in range(tiles):
        blk = tb * tiles + bl
        sc = scol_ref[bl * B:(bl + 1) * B, :]
        f = None
        for gi in range(G):
            e = g * G + gi
            base = (n * E + e) * (nblk + 1)
            lo = cnt_ref[base + blk]
            hi = cnt_ref[base + blk + 1]
            scol = jnp.sum(jnp.where(lane == e, sc, 0.0), axis=1, keepdims=True)
            ws = pl.multiple_of(jnp.minimum((lo // 16) * 16, cap - WIN), 16)
            col = lax.broadcasted_iota(jnp.int32, (1, WIN), 1)
            d = _dot((scol == (ws + col).astype(F32)).astype(BF), y_ref[gi, pl.ds(ws, WIN), :])
            f = d if f is None else f + d

            def more(k, carry, gi=gi, scol=scol, ws=ws, col=col):
                first = ws + k * WIN
                ks = pl.multiple_of(jnp.minimum(first, cap - WIN), 16)
                ids = ks + col
                hit = jnp.logical_and(scol == ids.astype(F32), ids >= first)
                acc[bl * B:(bl + 1) * B, :] += _dot(hit.astype(BF), y_ref[gi, pl.ds(ks, WIN), :])
                return carry

            lax.fori_loop(1, (hi - ws + WIN - 1) // WIN, more, 0)
        acc[bl * B:(bl + 1) * B, :] += f

    @pl.when(g == pl.num_programs(2) - 1)
    def _():
        _residual_epilogue(acc[...], x_ref, m_ref, mn_ref, g_ref, b_ref, xo_ref, h_ref, alpha)


SCATTER_EXPERT_GROUP = 4


def scatter_big(y, y_row0, scol, cnt_flat, x, mods, layer, next_layer, row0, per_req, ln_g, ln_b, N, T, alpha):
    E = N_EXPERTS
    cap = EC_CAPACITY_FACTOR * T // E
    D = x.shape[1]
    nblk = T // TOKEN_BLOCK
    tiles = 2 if nblk % 2 == 0 else 1
    tt = tiles * TOKEN_BLOCK
    ntt = T // tt
    G = SCATTER_EXPERT_GROUP
    assert y_row0 % cap == 0 and cap % 16 == 0
    return pl.pallas_call(
        functools.partial(_scatter_big_kernel, nblk=nblk, tiles=tiles, alpha=alpha),
        grid_spec=pltpu.PrefetchScalarGridSpec(
            num_scalar_prefetch=1,
            grid=(N, ntt, E // G),
            in_specs=[
                pl.BlockSpec((G, cap, D), lambda n, t, e, c: (e, y_row0 // cap + n, 0)),
                pl.BlockSpec((None, tt, E), lambda n, t, e, c: (n, t, 0)),
                pl.BlockSpec((tt, D), lambda n, t, e, c: (n * ntt + t, 0)),
                pl.BlockSpec((None, None, 6, D), lambda n, t, e, c: (layer, row0 + n * per_req, 0, 0)),
                pl.BlockSpec((None, None, 6, D), lambda n, t, e, c: (next_layer, row0 + n * per_req, 0, 0)),
                pl.BlockSpec((None, 1, D), lambda n, t, e, c: (layer, 0, 0)),
                pl.BlockSpec((None, 1, D), lambda n, t, e, c: (layer, 0, 0)),
            ],
            out_specs=[
                pl.BlockSpec((tt, D), lambda n, t, e, c: (n * ntt + t, 0)),
                pl.BlockSpec((tt, D), lambda n, t, e, c: (n * ntt + t, 0)),
            ],
            scratch_shapes=[pltpu.VMEM((tt, D), F32)],
        ),
        out_shape=[jax.ShapeDtypeStruct((N * T, D), F32), jax.ShapeDtypeStruct((N * T, D), BF)],
        compiler_params=_cparams("parallel", "parallel", "arbitrary"),
    )(cnt_flat, y, scol, x, mods, mods, ln_g, ln_b)


def moe_layer(groups, mods, layer, next_layer, ln_g, ln_b, w_gate, w_up, w_down, alpha):
    E = N_EXPERTS
    rows_total = sum(gr["N"] * (EC_CAPACITY_FACTOR * gr["T"] // E) for gr in groups)
    routed, row0s, rows, x_all, g_all = [], [], 0, None, None
    for gr in groups:
        N, T = gr["N"], gr["T"]
        scol, srow, cnt, a3 = route(gr["aff"], N, T)
        if T <= TOKEN_BLOCK:
            assert x_all is not None
            cnt_flat = None
            x_all, g_all = gather_small(gr["h"], srow, a3, x_all, g_all, rows, N, T)
        else:
            assert x_all is None
            cnt_flat = jnp.transpose(cnt, (0, 2, 1)).reshape(-1)
            x_all, g_all = gather_big(gr["h"], srow, cnt_flat, a3, rows_total, N, T)
        routed.append((scol, cnt_flat))
        row0s.append(rows)
        rows += N * (EC_CAPACITY_FACTOR * T // E)
    hm = expert_gate_up(x_all, w_gate, w_up, layer)
    y = expert_down(hm, w_down, g_all, layer)
    out = []
    for gr, (scol, cnt_flat), y_row0 in zip(groups, routed, row0s):
        tail = (gr["x"], mods, layer, next_layer, gr["row0"], gr["per_req"], ln_g, ln_b, gr["N"], gr["T"], alpha)
        if cnt_flat is None:
            out.append(scatter_small(y, y_row0, scol, *tail))
        else:
            out.append(scatter_big(y, y_row0, scol, cnt_flat, *tail))
    return out


def kernel(x_prompt, x_sample, cache_a_k, cache_a_v, cache_b_k, cache_b_v, cache_c_k, cache_c_v, c, c_ctx, w_mod, b_mod, w_in_even, w_out_even, a_sink, na_rpb, w_in_odd, w_out_odd, diff_lambda, diff_subln, ln_g, ln_b, w_router, w_gate, w_up, w_down):
    NP, TP, D = x_prompt.shape
    NS, TS, _ = x_sample.shape
    depth = w_mod.shape[0]
    L = cache_a_k.shape[2]
    alpha = (2 * depth) ** 0.25
    n_even = (depth + 1) // 2
    n_odd = depth // 2

    rows = 1 + NS
    rpad = -rows % 8
    cond = jnp.concatenate([c_ctx[None], c, jnp.zeros((rpad, D), F32)], axis=0)
    mods = adaln_all(cond, w_mod, b_mod).reshape(depth, rows + rpad, 6, D)

    rope = rope_tables(TS)
    sinks = a_sink.reshape(-1)
    rpb_tiles = toeplitz_bias(na_rpb)
    w_out_even_bf = w_out_even.astype(BF)
    w_out_odd_bf = w_out_odd.astype(BF)
    w_router_bf = w_router.astype(BF)
    ln_g4 = ln_g.reshape(depth, 2, 1, D)
    ln_b4 = ln_b.reshape(depth, 2, 1, D)
    ca_k = cache_a_k.reshape(NS, n_even, L, A_KV_HEADS * HEAD_DIM)
    ca_v = cache_a_v.reshape(NS, n_even, L, A_KV_HEADS * HEAD_DIM)
    cb_k = cache_b_k.reshape(NS, n_even, L, B_WIDTH)
    cb_v = cache_b_v.reshape(NS, n_even, L, B_WIDTH)
    cc_k = cache_c_k.reshape(NS, n_odd, L, C_HEADS * 2 * HEAD_DIM)
    cc_v = cache_c_v.reshape(NS, n_odd, L, C_HEADS * C_V_DIM)

    xp = x_prompt.reshape(NP * TP, D)
    xs = x_sample.reshape(NS * TS, D)
    hp = modulate_rows(xp, mods, 0, 0, 0, TP)
    hs = modulate_rows(xs, mods, 0, 1, 1, TS)

    new_a_k, new_a_v, new_b_k, new_b_v, new_c_k, new_c_v = [], [], [], [], [], []
    QA = A_Q_HEADS * HEAD_DIM
    KA = A_KV_HEADS * HEAD_DIM
    for l in range(depth):
        if l % 2 == 0:
            e = l // 2
            b0 = A_WIDTH
            pa = project(hp, w_in_even, e, 0, A_WIDTH, 0, None, TP, F32)
            pqb = project(hp, w_in_even, e, b0, B_WIDTH, 0, None, TP, F32)
            pkb = project(hp, w_in_even, e, b0 + B_WIDTH, B_WIDTH, 0, None, TP, F32)
            pvb = project(hp, w_in_even, e, b0 + 2 * B_WIDTH, B_WIDTH, 0, None, TP, F32)
            new_a_k.append(pa[:, QA:QA + KA].reshape(NP, TP, A_KV_HEADS, HEAD_DIM))
            new_a_v.append(pa[:, QA + KA:QA + 2 * KA].reshape(NP, TP, A_KV_HEADS, HEAD_DIM))
            new_b_k.append(pkb.reshape(NP, TP, B_HEADS, HEAD_DIM))
            new_b_v.append(pvb.reshape(NP, TP, B_HEADS, HEAD_DIM))
            op = [ctx_even_attention(pa, pqb, pkb, pvb, sinks, e, NP, TP)]

            sa = project(hs, w_in_even, e, 0, A_WIDTH, A_Q_HEADS + A_KV_HEADS, rope, TS, BF, A_Q_HEADS)
            sqb = project(hs, w_in_even, e, b0, B_WIDTH, 0, rope, TS, BF, B_HEADS)
            skb = project(hs, w_in_even, e, b0 + B_WIDTH, B_WIDTH, 0, rope, TS, BF)
            svb = project(hs, w_in_even, e, b0 + 2 * B_WIDTH, B_WIDTH, 0, rope, TS, BF)
            oa = window_attention(sa, ca_k, ca_v, sinks, e, NS, TS)
            bias3 = neighbourhood_bias(rpb_tiles[e], TS)
            ob = neighbourhood_attention(sqb, skb, svb, cb_k, cb_v, bias3, e, NS, TS)
            os_ = [oa, ob]
            w_out, widx = w_out_even_bf, e
        else:
            o = l // 2
            lambda_init = 0.8 - 0.6 * math.exp(-0.3 * l)
            W = C_HEADS * C_V_DIM
            pq = project(hp, w_in_odd, o, 0, W, 0, None, TP, F32)
            pk = project(hp, w_in_odd, o, W, W, 0, None, TP, F32)
            pv = project(hp, w_in_odd, o, 2 * W, W, 0, None, TP, F32)
            new_c_k.append(pk.reshape(NP, TP, C_HEADS, 2, HEAD_DIM))
            new_c_v.append(pv.reshape(NP, TP, C_HEADS, C_V_DIM))
            op = [diff_attention(pq, pk, pv, diff_lambda, diff_subln, lambda_init, NP, TP, None, o)]

            sq = project(hs, w_in_odd, o, 0, W, 2 * C_HEADS, rope, TS, BF, 2 * C_HEADS)
            sk = project(hs, w_in_odd, o, W, W, 2 * C_HEADS, rope, TS, BF)
            sv = project(hs, w_in_odd, o, 2 * W, W, 0, rope, TS, BF)
            os_ = [diff_attention(sq, sk, sv, diff_lambda, diff_subln, lambda_init, NS, TS, (cc_k, cc_v), o, True)]
            w_out, widx = w_out_odd_bf, o

        lg1, lb1 = ln_g4[:, 0], ln_b4[:, 0]
        lg2, lb2 = ln_g4[:, 1], ln_b4[:, 1]
        xp, hp, affp = outproj_ln_router(op, w_out, widx, xp, mods, l, 0, 0, TP, lg1, lb1, w_router_bf, alpha)
        xs, hs, affs = outproj_ln_router(os_, w_out, widx, xs, mods, l, 1, 1, TS, lg1, lb1, w_router_bf, alpha)
        nl = min(l + 1, depth - 1)
        groups = [dict(aff=affs, h=hs, x=xs, N=NS, T=TS, row0=1, per_req=1),
                  dict(aff=affp, h=hp, x=xp, N=NP, T=TP, row0=0, per_req=0)]
        (xs, hs), (xp, hp) = moe_layer(groups, mods, l, nl, lg2, lb2, w_gate, w_up, w_down, alpha)

    return (xp.reshape(NP, TP, D), xs.reshape(NS, TS, D),
            jnp.stack(new_a_k, axis=1), jnp.stack(new_a_v, axis=1),
            jnp.stack(new_b_k, axis=1), jnp.stack(new_b_v, axis=1),
            jnp.stack(new_c_k, axis=1), jnp.stack(new_c_v, axis=1))
```

```python
import functools
import math

import jax
import jax.numpy as jnp
from jax import lax
from jax.experimental import pallas as pl
from jax.experimental.pallas import tpu as pltpu

BF = jnp.bfloat16
F32 = jnp.float32

GRID_W = 64
HEAD_DIM = 128
A_Q_HEADS = 8
A_KV_HEADS = 2
A_GROUP = A_Q_HEADS // A_KV_HEADS
A_BLOCK = 128
B_HEADS = 8
NA_ROWS = 8
NA_COLS = 16
C_HEADS = 8
C_V_DIM = 2 * HEAD_DIM
N_EXPERTS = 16
EC_CAPACITY_FACTOR = 2
ROPE_BASE = 10000.0
LN_EPS = 1e-5
RMS_EPS = 1e-5
ATTN_SCALE = HEAD_DIM ** -0.5
A_WIDTH = (A_Q_HEADS + 2 * A_KV_HEADS) * HEAD_DIM
B_WIDTH = B_HEADS * HEAD_DIM

VMEM_LIMIT_BYTES = 52 * 1024 * 1024
TOKEN_BLOCK = 256
SLOT_CHUNK = 128


def _cparams(*sem):
    return pltpu.CompilerParams(dimension_semantics=sem, vmem_limit_bytes=VMEM_LIMIT_BYTES)


def _nt_dot(a, b):
    return lax.dot_general(a, b, (((1,), (1,)), ((), ())), preferred_element_type=F32)


def _dot(a, b):
    return jnp.dot(a, b, preferred_element_type=F32)


def _layer_norm(z, g, b):
    mu = jnp.mean(z, -1, keepdims=True)
    d = z - mu
    var = jnp.mean(d * d, -1, keepdims=True)
    return d * lax.rsqrt(var + LN_EPS) * g + b


def _adaln_kernel(c_ref, w_ref, b_ref, o_ref):
    c = c_ref[...]
    s = (c * jax.nn.sigmoid(c)).astype(BF)
    o_ref[...] = _dot(s, w_ref[...].astype(BF)) + b_ref[...]


def adaln_all(cond, w_mod, b_mod):
    L, D, D6 = w_mod.shape
    R = cond.shape[0]
    tn = D // 2
    return pl.pallas_call(
        _adaln_kernel,
        grid=(L, D6 // tn),
        in_specs=[
            pl.BlockSpec((R, D), lambda l, j: (0, 0)),
            pl.BlockSpec((None, D, tn), lambda l, j: (l, 0, j)),
            pl.BlockSpec((None, 1, tn), lambda l, j: (l, 0, j)),
        ],
        out_specs=pl.BlockSpec((None, R, tn), lambda l, j: (l, 0, j)),
        out_shape=jax.ShapeDtypeStruct((L, R, D6), F32),
        compiler_params=_cparams("parallel", "parallel"),
    )(cond, w_mod, b_mod.reshape(L, 1, D6))


def _modulate_kernel(x_ref, m_ref, o_ref):
    o_ref[...] = (x_ref[...] * (1.0 + m_ref[1:2, :]) + m_ref[0:1, :]).astype(o_ref.dtype)


def modulate_rows(x, mods, layer, row0, per_req, T):
    M, D = x.shape
    tt = min(T, 512)
    return pl.pallas_call(
        _modulate_kernel,
        grid=(M // tt,),
        in_specs=[
            pl.BlockSpec((tt, D), lambda i: (i, 0)),
            pl.BlockSpec((None, None, 6, D), lambda i: (layer, row0 + ((i * tt) // T) * per_req, 0, 0)),
        ],
        out_specs=pl.BlockSpec((tt, D), lambda i: (i, 0)),
        out_shape=jax.ShapeDtypeStruct((M, D), BF),
        compiler_params=_cparams("parallel"),
    )(x, mods)


def _swap32(x):
    lane = lax.broadcasted_iota(jnp.int32, (1, HEAD_DIM), 1)
    return jnp.where((lane % 64) < 32, pltpu.roll(x, HEAD_DIM - 32, 1), pltpu.roll(x, 32, 1))


def _proj_kernel(*refs, n_rope_heads, n_scaled_heads, heads_per_tile):
    if n_rope_heads:
        h_ref, w_ref, cos_ref, sin_ref, o_ref, wbf = refs
    else:
        h_ref, w_ref, o_ref, wbf = refs
    j = pl.program_id(0)
    i = pl.program_id(1)

    @pl.when(i == 0)
    def _():
        wbf[...] = w_ref[...].astype(BF)

    acc = _dot(h_ref[...], wbf[...])
    bounds = sorted({0, min(n_rope_heads, n_scaled_heads), n_rope_heads, n_scaled_heads, 1 << 30})
    for hh in range(heads_per_tile):
        sl = slice(hh * HEAD_DIM, (hh + 1) * HEAD_DIM)
        x = acc[:, sl]
        head = j * heads_per_tile + hh
        for lo, hi in zip(bounds[:-1], bounds[1:]):
            rot = lo < n_rope_heads
            scale = Q_PRESCALE if lo < n_scaled_heads else None

            def emit(x=x, sl=sl, rot=rot, scale=scale):
                y = x * cos_ref[...] + _swap32(x) * sin_ref[...] if rot else x
                o_ref[:, sl] = (y if scale is None else y * scale).astype(o_ref.dtype)

            if len(bounds) == 2:
                emit()
            else:
                pl.when(jnp.logical_and(head >= lo, head < hi))(emit)


def project(h, w, widx, col0, width, n_rope_heads, rope, T, out_dtype, n_scaled_heads=0):
    M, D = h.shape
    heads = width // HEAD_DIM
    hpt = max(k for k in range(1, 9) if heads % k == 0 and col0 % (k * HEAD_DIM) == 0)
    tn = hpt * HEAD_DIM
    tm = min(T if n_rope_heads else M, 1024)
    tpb = max(T // tm, 1)
    in_specs = [
        pl.BlockSpec((tm, D), lambda j, i: (i, 0)),
        pl.BlockSpec((None, D, tn), lambda j, i: (widx, 0, col0 // tn + j)),
    ]
    args = [h, w]
    if n_rope_heads:
        in_specs += [pl.BlockSpec((tm, HEAD_DIM), lambda j, i: (i % tpb, 0))] * 2
        args += list(rope)
    return pl.pallas_call(
        functools.partial(_proj_kernel, n_rope_heads=n_rope_heads, n_scaled_heads=n_scaled_heads, heads_per_tile=hpt),
        grid=(width // tn, M // tm),
        in_specs=in_specs,
        out_specs=pl.BlockSpec((tm, tn), lambda j, i: (i, j)),
        out_shape=jax.ShapeDtypeStruct((M, width), out_dtype),
        scratch_shapes=[pltpu.VMEM((D, tn), BF)],
        compiler_params=_cparams("parallel", "arbitrary"),
    )(*args)


def rope_tables(T):
    t = jnp.arange(T)
    row = (t // GRID_W).astype(F32)
    col = (t % GRID_W).astype(F32)
    axis_dim = HEAD_DIM // 2
    inv_freq = ROPE_BASE ** (-jnp.arange(0, axis_dim, 2, dtype=F32) / axis_dim)
    ar = row[:, None] * inv_freq
    ac = col[:, None] * inv_freq
    cos = jnp.concatenate([jnp.cos(ar), jnp.cos(ar), jnp.cos(ac), jnp.cos(ac)], -1)
    sin = jnp.concatenate([-jnp.sin(ar), jnp.sin(ar), -jnp.sin(ac), jnp.sin(ac)], -1)
    return cos, sin


LOG2E = math.log2(math.e)
Q_PRESCALE = ATTN_SCALE * LOG2E


def _exp2_parts(parts, sink=None):
    m = None
    for s in parts:
        mi = jnp.max(s, -1, keepdims=True)
        m = mi if m is None else jnp.maximum(m, mi)
    if sink is not None:
        m = jnp.maximum(m, sink)
    es = [jnp.exp2(s - m) for s in parts]
    den = None
    for e in es:
        di = jnp.sum(e, -1, keepdims=True)
        den = di if den is None else den + di
    if sink is not None:
        den = den + jnp.exp2(sink - m)
    return es, 1.0 / den


def _softmax_parts(parts, sink=None):
    m = None
    for s in parts:
        mi = jnp.max(s, -1, keepdims=True)
        m = mi if m is None else jnp.maximum(m, mi)
    if sink is not None:
        m = jnp.maximum(m, sink)
    es = [jnp.exp(s - m) for s in parts]
    den = None
    for e in es:
        di = jnp.sum(e, -1, keepdims=True)
        den = di if den is None else den + di
    if sink is not None:
        den = den + jnp.exp(sink - m)
    inv = 1.0 / den
    return [e * inv for e in es]


def _ctx_even_kernel(sink_ref, a_ref, qb_ref, kb_ref, vb_ref, o_ref, *, sink_base):
    T = a_ref.shape[0]
    for kv in range(A_KV_HEADS):
        k = a_ref[:, (A_Q_HEADS + kv) * HEAD_DIM:(A_Q_HEADS + kv + 1) * HEAD_DIM].astype(BF)
        v = a_ref[:, (A_Q_HEADS + A_KV_HEADS + kv) * HEAD_DIM:(A_Q_HEADS + A_KV_HEADS + kv + 1) * HEAD_DIM].astype(BF)
        for g in range(A_GROUP):
            hq = kv * A_GROUP + g
            q = a_ref[:, hq * HEAD_DIM:(hq + 1) * HEAD_DIM].astype(BF)
            s = _nt_dot(q, k) * ATTN_SCALE
            sink = jnp.full((T, 1), sink_ref[sink_base + hq], F32)
            (p,) = _softmax_parts([s], sink)
            o_ref[:, hq * HEAD_DIM:(hq + 1) * HEAD_DIM] = _dot(p.astype(BF), v).astype(o_ref.dtype)
    for hb in range(B_HEADS):
        sl = slice(hb * HEAD_DIM, (hb + 1) * HEAD_DIM)
        s = _nt_dot(qb_ref[:, sl].astype(BF), kb_ref[:, sl].astype(BF)) * ATTN_SCALE
        (p,) = _softmax_parts([s])
        o_ref[:, A_Q_HEADS * HEAD_DIM + hb * HEAD_DIM:A_Q_HEADS * HEAD_DIM + (hb + 1) * HEAD_DIM] = _dot(
            p.astype(BF), vb_ref[:, sl].astype(BF)).astype(o_ref.dtype)


def ctx_even_attention(arr_a, qb, kb, vb, sinks, e, N, T):
    M = N * T
    return pl.pallas_call(
        functools.partial(_ctx_even_kernel, sink_base=e * A_Q_HEADS),
        grid=(N,),
        in_specs=[
            pl.BlockSpec(memory_space=pltpu.SMEM),
            pl.BlockSpec((T, A_WIDTH), lambda n: (n, 0)),
            pl.BlockSpec((T, B_WIDTH), lambda n: (n, 0)),
            pl.BlockSpec((T, B_WIDTH), lambda n: (n, 0)),
            pl.BlockSpec((T, B_WIDTH), lambda n: (n, 0)),
        ],
        out_specs=pl.BlockSpec((T, A_Q_HEADS * HEAD_DIM + B_WIDTH), lambda n: (n, 0)),
        out_shape=jax.ShapeDtypeStruct((M, A_Q_HEADS * HEAD_DIM + B_WIDTH), BF),
        compiler_params=_cparams("parallel"),
    )(sinks, arr_a, qb, kb, vb)


def _win_kernel(sink_ref, q_ref, kvp_ref, kvc_ref, kvn_ref, kc_ref, vc_ref, o_ref, *, sink_base, nb):
    i = pl.program_id(1)
    R = A_GROUP * A_BLOCK
    qq = lax.broadcasted_iota(jnp.int32, (R, 1), 0) % A_BLOCK
    kk = lax.broadcasted_iota(jnp.int32, (1, A_BLOCK), 1)
    ok_prev = jnp.logical_and(kk >= qq, i > 0)
    ok_next = jnp.logical_and(kk <= qq, i < nb - 1)
    for kv in range(A_KV_HEADS):
        ksl = slice(kv * HEAD_DIM, (kv + 1) * HEAD_DIM)
        vsl = slice((A_KV_HEADS + kv) * HEAD_DIM, (A_KV_HEADS + kv + 1) * HEAD_DIM)
        q = jnp.concatenate(
            [q_ref[:, (kv * A_GROUP + g) * HEAD_DIM:(kv * A_GROUP + g + 1) * HEAD_DIM] for g in range(A_GROUP)], axis=0)
        sink = jnp.concatenate(
            [jnp.full((A_BLOCK, 1), sink_ref[sink_base + kv * A_GROUP + g] * LOG2E, F32) for g in range(A_GROUP)],
            axis=0)
        s_p = jnp.where(ok_prev, _nt_dot(q, kvp_ref[:, ksl]), -jnp.inf)
        s_c = _nt_dot(q, kvc_ref[:, ksl])
        s_n = jnp.where(ok_next, _nt_dot(q, kvn_ref[:, ksl]), -jnp.inf)
        s_x = _nt_dot(q, kc_ref[:, ksl].astype(BF))
        (e_p, e_c, e_n, e_x), inv = _exp2_parts([s_p, s_c, s_n, s_x], sink)
        o = (_dot(e_p.astype(BF), kvp_ref[:, vsl]) + _dot(e_c.astype(BF), kvc_ref[:, vsl])
             + _dot(e_n.astype(BF), kvn_ref[:, vsl]) + _dot(e_x.astype(BF), vc_ref[:, ksl].astype(BF))) * inv
        for g in range(A_GROUP):
            hq = kv * A_GROUP + g
            o_ref[:, hq * HEAD_DIM:(hq + 1) * HEAD_DIM] = o[g * A_BLOCK:(g + 1) * A_BLOCK].astype(o_ref.dtype)


def window_attention(arr_a, cache_k, cache_v, sinks, e, N, T):
    M = N * T
    nb = T // A_BLOCK
    L = cache_k.shape[2]
    QW = A_Q_HEADS * HEAD_DIM
    KVW = 2 * A_KV_HEADS * HEAD_DIM
    kvblk = QW // KVW
    return pl.pallas_call(
        functools.partial(_win_kernel, sink_base=e * A_Q_HEADS, nb=nb),
        grid=(N, nb),
        in_specs=[
            pl.BlockSpec(memory_space=pltpu.SMEM),
            pl.BlockSpec((A_BLOCK, QW), lambda n, i: (n * nb + i, 0)),
            pl.BlockSpec((A_BLOCK, KVW), lambda n, i: (n * nb + jnp.maximum(i - 1, 0), kvblk)),
            pl.BlockSpec((A_BLOCK, KVW), lambda n, i: (n * nb + i, kvblk)),
            pl.BlockSpec((A_BLOCK, KVW), lambda n, i: (n * nb + jnp.minimum(i + 1, nb - 1), kvblk)),
            pl.BlockSpec((None, None, L, A_KV_HEADS * HEAD_DIM), lambda n, i: (n, e, 0, 0)),
            pl.BlockSpec((None, None, L, A_KV_HEADS * HEAD_DIM), lambda n, i: (n, e, 0, 0)),
        ],
        out_specs=pl.BlockSpec((A_BLOCK, QW), lambda n, i: (n * nb + i, 0)),
        out_shape=jax.ShapeDtypeStruct((M, QW), BF),
        compiler_params=_cparams("parallel", "parallel"),
    )(sinks, arr_a, arr_a, arr_a, arr_a, cache_k, cache_v)


NB_Q_ROWS = TOKEN_BLOCK // GRID_W
NB_WIN_BLOCKS = 3


def _toeplitz_kernel(rpb_ref, o_ref):
    K = rpb_ref.shape[1]
    W2 = GRID_W * GRID_W
    col = lax.broadcasted_iota(jnp.int32, (K, W2), 1)
    cq = col // GRID_W
    ck = col % GRID_W
    dc = jnp.clip(ck - cq + (NA_COLS - 1), 0, 2 * NA_COLS - 2)
    onehot = (lax.broadcasted_iota(jnp.int32, (K, W2), 0) == dc).astype(BF)
    t = rpb_ref[...]
    hi = t.astype(BF)
    r1 = t - hi.astype(F32)
    mid = r1.astype(BF)
    lo = (r1 - mid.astype(F32)).astype(BF)
    val = _dot(hi, onehot) + _dot(mid, onehot) + _dot(lo, onehot)
    c0 = jnp.clip(cq[0:1] - NA_COLS // 2, 0, GRID_W - NA_COLS)
    ok = jnp.logical_and(ck[0:1] >= c0, ck[0:1] < c0 + NA_COLS)
    o_ref[...] = jnp.where(ok, val * LOG2E, -jnp.inf)


def toeplitz_bias(rpb):
    G, H, NR, NC = rpb.shape
    R = G * H * NR
    K = 32
    flat = jnp.pad(rpb.reshape(R, NC), ((0, -R % 8), (0, K - NC)))
    out = pl.pallas_call(
        _toeplitz_kernel,
        out_shape=jax.ShapeDtypeStruct((flat.shape[0], GRID_W * GRID_W), F32),
        compiler_params=pltpu.CompilerParams(vmem_limit_bytes=VMEM_LIMIT_BYTES),
    )(flat)
    return out[:R].reshape(G, H, NR, GRID_W, GRID_W)


def neighbourhood_bias(tiles, T):
    rows = T // GRID_W
    nblk = T // TOKEN_BLOCK
    H = tiles.shape[0]
    masked = jnp.full((H, GRID_W, GRID_W), -jnp.inf, F32)
    out = []
    for blk in (0, 1, nblk - 1):
        w0 = min(max(blk - 1, 0), nblk - NB_WIN_BLOCKS) * NB_Q_ROWS
        qrows = []
        for ql in range(NB_Q_ROWS):
            rq = blk * NB_Q_ROWS + ql
            r0 = min(max(rq - NA_ROWS // 2, 0), rows - NA_ROWS)
            krow = []
            for kl in range(NB_WIN_BLOCKS * NB_Q_ROWS):
                rk = w0 + kl
                krow.append(tiles[:, rk - rq + NA_ROWS - 1] if r0 <= rk < r0 + NA_ROWS else masked)
            qrows.append(jnp.concatenate(krow, axis=-1))
        out.append(jnp.concatenate(qrows, axis=-2))
    return jnp.stack(out, axis=0)


def _nbr_kernel(q_ref, k0_ref, k1_ref, k2_ref, v0_ref, v1_ref, v2_ref, kc_ref, vc_ref, bias_ref, o_ref):
    for h in range(B_HEADS):
        sl = slice(h * HEAD_DIM, (h + 1) * HEAD_DIM)
        q = q_ref[:, sl]
        parts = []
        for w, k_ref in enumerate((k0_ref, k1_ref, k2_ref)):
            b = bias_ref[h, :, w * TOKEN_BLOCK:(w + 1) * TOKEN_BLOCK]
            parts.append(_nt_dot(q, k_ref[:, sl]) + b)
        parts.append(_nt_dot(q, kc_ref[:, sl].astype(BF)))
        (e0, e1, e2, ex), inv = _exp2_parts(parts)
        o = (_dot(e0.astype(BF), v0_ref[:, sl]) + _dot(e1.astype(BF), v1_ref[:, sl])
             + _dot(e2.astype(BF), v2_ref[:, sl]) + _dot(ex.astype(BF), vc_ref[:, sl].astype(BF))) * inv
        o_ref[:, sl] = o.astype(o_ref.dtype)


def neighbourhood_attention(qb, kb, vb, cache_k, cache_v, bias3, e, N, T):
    M = N * T
    nblk = T // TOKEN_BLOCK
    L = cache_k.shape[2]

    def win(w):
        return lambda n, i: (n * nblk + jnp.clip(i - 1, 0, nblk - NB_WIN_BLOCKS) + w, 0)

    def case(n, i):
        return (jnp.where(i == 0, 0, jnp.where(i == nblk - 1, 2, 1)), 0, 0, 0)

    blk = pl.BlockSpec((TOKEN_BLOCK, B_WIDTH), lambda n, i: (n * nblk + i, 0))
    return pl.pallas_call(
        _nbr_kernel,
        grid=(N, nblk),
        in_specs=[
            blk,
            pl.BlockSpec((TOKEN_BLOCK, B_WIDTH), win(0)),
            pl.BlockSpec((TOKEN_BLOCK, B_WIDTH), win(1)),
            pl.BlockSpec((TOKEN_BLOCK, B_WIDTH), win(2)),
            pl.BlockSpec((TOKEN_BLOCK, B_WIDTH), win(0)),
            pl.BlockSpec((TOKEN_BLOCK, B_WIDTH), win(1)),
            pl.BlockSpec((TOKEN_BLOCK, B_WIDTH), win(2)),
            pl.BlockSpec((None, None, L, B_WIDTH), lambda n, i: (n, e, 0, 0)),
            pl.BlockSpec((None, None, L, B_WIDTH), lambda n, i: (n, e, 0, 0)),
            pl.BlockSpec((None, B_HEADS, TOKEN_BLOCK, NB_WIN_BLOCKS * TOKEN_BLOCK), case),
        ],
        out_specs=blk,
        out_shape=jax.ShapeDtypeStruct((M, B_WIDTH), BF),
        compiler_params=_cparams("parallel", "arbitrary"),
    )(qb, kb, kb, kb, vb, vb, vb, cache_k, cache_v, bias3)


DIFF_ROW_SPLIT = 1


def _diff_kernel(*refs, has_ctx, lambda_init, q_prescaled):
    if has_ctx:
        lam_ref, g_ref, q_ref, k_ref, v_ref, kc_ref, vc_ref, o_ref = refs
    else:
        lam_ref, g_ref, q_ref, k_ref, v_ref, o_ref = refs
    lp = lam_ref[...]
    lam = (jnp.exp(jnp.sum(lp[0:1] * lp[1:2], -1, keepdims=True))
           - jnp.exp(jnp.sum(lp[2:3] * lp[3:4], -1, keepdims=True)) + lambda_init)
    tq = q_ref.shape[0]
    sub = tq // DIFF_ROW_SPLIT
    for r0 in range(0, tq, sub):
        rows = slice(r0, r0 + sub)
        nums, invs = [], []
        for half in range(2):
            sl = slice(half * HEAD_DIM, (half + 1) * HEAD_DIM)
            if q_prescaled:
                q = q_ref[rows, sl]
            else:
                q = (q_ref[rows, sl].astype(F32) * Q_PRESCALE).astype(BF)
            parts = [_nt_dot(q, k_ref[:, sl].astype(BF))]
            if has_ctx:
                parts.append(_nt_dot(q, kc_ref[:, sl].astype(BF)))
            es, inv = _exp2_parts(parts)
            nums.append(es)
            invs.append(inv)
        c0 = invs[0]
        c1 = lam * invs[1]
        a = (nums[0][0] * c0 - nums[1][0] * c1).astype(BF)
        o = _dot(a, v_ref[...].astype(BF))
        if has_ctx:
            ac = (nums[0][1] * c0 - nums[1][1] * c1).astype(BF)
            o = o + _dot(ac, vc_ref[...].astype(BF))
        of = o * lax.rsqrt(jnp.mean(o * o, -1, keepdims=True) + RMS_EPS)
        o_ref[rows, :] = ((of * g_ref[...]) * (1.0 - lambda_init)).astype(o_ref.dtype)


def _diff_lambda(lam_ref, lambda_init):
    lp = lam_ref[...]
    return (jnp.exp(jnp.sum(lp[0:1] * lp[1:2], -1, keepdims=True))
            - jnp.exp(jnp.sum(lp[2:3] * lp[3:4], -1, keepdims=True)) + lambda_init)


def _diff_pipe_kernel(lam_ref, g_ref, q_ref, k_ref, v_ref, kc_ref, vc_ref, o_ref,
                      el0, el1, ec0, ec1, cf0, cf1, *, lambda_init):
    i = pl.program_id(2)
    tq = q_ref.shape[0]

    @pl.when(i == 0)
    def _():
        for ref in (el0, el1, ec0, ec1, cf0, cf1):
            ref[...] = jnp.zeros_like(ref)

    def step(e_lat, e_ctx, coef, p_lat, p_ctx, p_coef):
        lam = _diff_lambda(lam_ref, lambda_init)
        den = []
        for half in range(2):
            sl = slice(half * HEAD_DIM, (half + 1) * HEAD_DIM)
            q = q_ref[:, sl]
            (el, ec), inv = _exp2_parts([_nt_dot(q, k_ref[:, sl]), _nt_dot(q, kc_ref[:, sl].astype(BF))])
            e_lat[half] = el
            e_ctx[half] = ec
            den.append(inv)
        coef[0] = jnp.broadcast_to(den[0], (tq, HEAD_DIM))
        coef[1] = jnp.broadcast_to(den[1] * lam, (tq, HEAD_DIM))
        c0 = p_coef[0][:, 0:1]
        c1 = p_coef[1][:, 0:1]
        a = (p_lat[0] * c0 - p_lat[1] * c1).astype(BF)
        ac = (p_ctx[0] * c0 - p_ctx[1] * c1).astype(BF)
        o = _dot(a, v_ref[...]) + _dot(ac, vc_ref[...].astype(BF))
        of = o * lax.rsqrt(jnp.mean(o * o, -1, keepdims=True) + RMS_EPS)
        o_ref[...] = ((of * g_ref[...]) * (1.0 - lambda_init)).astype(o_ref.dtype)

    @pl.when(i % 2 == 0)
    def _():
        step(el0, ec0, cf0, el1, ec1, cf1)

    @pl.when(i % 2 == 1)
    def _():
        step(el1, ec1, cf1, el0, ec0, cf0)


def diff_attention_pipelined(q, k, v, lam_p, subln, lambda_init, N, T, ctx, widx):
    M = N * T
    tq = 256
    nq = T // tq
    W = C_V_DIM
    L = ctx[0].shape[2]
    return pl.pallas_call(
        functools.partial(_diff_pipe_kernel, lambda_init=lambda_init),
        grid=(N, C_HEADS, nq + 1),
        in_specs=[
            pl.BlockSpec((None, 4, HEAD_DIM), lambda n, h, i: (widx, 0, 0)),
            pl.BlockSpec((None, 1, W), lambda n, h, i: (widx, 0, 0)),
            pl.BlockSpec((tq, W), lambda n, h, i: (n * nq + jnp.minimum(i, nq - 1), h)),
            pl.BlockSpec((T, W), lambda n, h, i: (n, h)),
            pl.BlockSpec((T, W), lambda n, h, i: (n, h)),
            pl.BlockSpec((None, None, L, W), lambda n, h, i: (n, widx, 0, h)),
            pl.BlockSpec((None, None, L, W), lambda n, h, i: (n, widx, 0, h)),
        ],
        out_specs=pl.BlockSpec((tq, W), lambda n, h, i: (n * nq + jnp.maximum(i - 1, 0), h)),
        out_shape=jax.ShapeDtypeStruct((M, C_HEADS * W), BF),
        scratch_shapes=[pltpu.VMEM((2, tq, T), F32)] * 2 + [pltpu.VMEM((2, tq, L), F32)] * 2
        + [pltpu.VMEM((2, tq, HEAD_DIM), F32)] * 2,
        compiler_params=_cparams("parallel", "parallel", "arbitrary"),
    )(lam_p, subln.reshape(subln.shape[0], 1, W), q, k, v, *ctx)


def diff_attention(q, k, v, lam_p, subln, lambda_init, N, T, ctx=None, widx=0, q_prescaled=False):
    M = N * T
    tq = min(T, 256)
    nq = T // tq
    W = C_V_DIM
    in_specs = [
        pl.BlockSpec((None, 4, HEAD_DIM), lambda n, h, i: (widx, 0, 0)),
        pl.BlockSpec((None, 1, W), lambda n, h, i: (widx, 0, 0)),
        pl.BlockSpec((tq, W), lambda n, h, i: (n * nq + i, h)),
        pl.BlockSpec((T, W), lambda n, h, i: (n, h)),
        pl.BlockSpec((T, W), lambda n, h, i: (n, h)),
    ]
    args = [lam_p, subln.reshape(subln.shape[0], 1, W), q, k, v]
    if ctx is not None:
        L = ctx[0].shape[2]
        in_specs += [pl.BlockSpec((None, None, L, W), lambda n, h, i: (n, widx, 0, h))] * 2
        args += list(ctx)
    return pl.pallas_call(
        functools.partial(_diff_kernel, has_ctx=ctx is not None, lambda_init=lambda_init, q_prescaled=q_prescaled),
        grid=(N, C_HEADS, nq),
        in_specs=in_specs,
        out_specs=pl.BlockSpec((tq, W), lambda n, h, i: (n * nq + i, h)),
        out_shape=jax.ShapeDtypeStruct((M, C_HEADS * W), BF),
        compiler_params=_cparams("parallel", "parallel", "arbitrary"),
    )(*args)


def _outproj_kernel(*refs, n_in, alpha):
    o_refs = refs[:n_in]
    w_refs = refs[n_in:2 * n_in]
    x_ref, m_ref, g_ref, b_ref, wr_ref, xo_ref, h_ref, aff_ref = refs[2 * n_in:]
    y = None
    for o_r, w_r in zip(o_refs, w_refs):
        t = _dot(o_r[...], w_r[...])
        y = t if y is None else y + t
    z = alpha * x_ref[...] + m_ref[2:3, :] * y
    xn = _layer_norm(z, g_ref[...], b_ref[...])
    xo_ref[...] = xn
    hb = (xn * (1.0 + m_ref[4:5, :]) + m_ref[3:4, :]).astype(BF)
    h_ref[...] = hb
    logits = _dot(hb, wr_ref[...])
    mx = jnp.max(logits, -1, keepdims=True)
    ex = jnp.exp(logits - mx)
    aff_ref[...] = ex / jnp.sum(ex, -1, keepdims=True)


def outproj_ln_router(os_, w_out, widx, x, mods, layer, row0, per_req, T, ln_g, ln_b, w_router, alpha):
    M, D = x.shape
    tm = min(T, 256)
    E = w_router.shape[-1]
    in_specs, args, koff = [], [], 0
    for o in os_:
        in_specs.append(pl.BlockSpec((tm, o.shape[1]), lambda i: (i, 0)))
        args.append(o)
    for o in os_:
        wk = o.shape[1]
        in_specs.append(pl.BlockSpec((None, wk, D), functools.partial(lambda i, kb: (widx, kb, 0), kb=koff // wk)))
        args.append(w_out)
        koff += wk
    in_specs += [
        pl.BlockSpec((tm, D), lambda i: (i, 0)),
        pl.BlockSpec((None, None, 6, D), lambda i: (layer, row0 + ((i * tm) // T) * per_req, 0, 0)),
        pl.BlockSpec((None, 1, D), lambda i: (layer, 0, 0)),
        pl.BlockSpec((None, 1, D), lambda i: (layer, 0, 0)),
        pl.BlockSpec((None, D, E), lambda i: (layer, 0, 0)),
    ]
    args += [x, mods, ln_g, ln_b, w_router]
    return pl.pallas_call(
        functools.partial(_outproj_kernel, n_in=len(os_), alpha=alpha),
        grid=(M // tm,),
        in_specs=in_specs,
        out_specs=[
            pl.BlockSpec((tm, D), lambda i: (i, 0)),
            pl.BlockSpec((tm, D), lambda i: (i, 0)),
            pl.BlockSpec((tm, E), lambda i: (i, 0)),
        ],
        out_shape=[
            jax.ShapeDtypeStruct((M, D), F32),
            jax.ShapeDtypeStruct((M, D), BF),
            jax.ShapeDtypeStruct((M, E), F32),
        ],
        compiler_params=_cparams("parallel"),
    )(*args)


def _route_kernel(aff_ref, scol_ref, srow_ref, cnt_ref, a3_ref, *, cap, nblk):
    E = N_EXPERTS
    B = aff_ref.shape[0] // nblk
    a = aff_ref[...]
    bits = pltpu.bitcast(a, jnp.int32)

    def search(it, ans):
        cand = ans | jnp.left_shift(jnp.int32(1), 30 - it)
        cnt = jnp.sum((bits >= cand).astype(F32), axis=0, keepdims=True)
        return jnp.where(cnt >= cap, cand, ans)

    thr = lax.fori_loop(0, 31, search, jnp.zeros((1, E), jnp.int32))
    gt = bits > thr
    eq = bits == thr
    need = cap - jnp.sum(gt.astype(F32), axis=0, keepdims=True)

    r = lax.broadcasted_iota(jnp.int32, (B, B), 0)
    c = lax.broadcasted_iota(jnp.int32, (B, B), 1)
    lower = (c < r).astype(BF)
    upper = (r < c).astype(BF)
    eye = (lax.broadcasted_iota(jnp.int32, (E, E), 0) == lax.broadcasted_iota(jnp.int32, (E, E), 1)).astype(BF)

    carry = jnp.zeros((1, E), F32)
    sels = []
    for b in range(nblk):
        sl = slice(b * B, (b + 1) * B)
        eqb = eq[sl].astype(BF)
        pref = _dot(lower, eqb) + carry
        carry = carry + jnp.sum(eqb.astype(F32), axis=0, keepdims=True)
        sels.append(jnp.logical_or(gt[sl], jnp.logical_and(eq[sl], pref < need)))

    carry = jnp.zeros((1, E), F32)
    carry_t = jnp.zeros((E, 1), F32)
    for b in range(nblk):
        sl = slice(b * B, (b + 1) * B)
        selb = sels[b].astype(BF)
        cnt_ref[b:b + 1, :] = carry.astype(jnp.int32)
        pos = _dot(lower, selb) + carry
        scol_ref[sl, :] = jnp.where(sels[b], pos, -1.0)
        carry = carry + jnp.sum(selb.astype(F32), axis=0, keepdims=True)
        sel_t = _nt_dot(eye, selb)
        pos_t = _dot(sel_t.astype(BF), upper) + carry_t
        srow_ref[b] = jnp.where(sel_t > 0.5, pos_t, -1.0)
        carry_t = carry_t + jnp.sum(sel_t, axis=1, keepdims=True)
    cnt_ref[nblk:nblk + 1, :] = carry.astype(jnp.int32)

    hi = a.astype(BF)
    r1 = a - hi.astype(F32)
    mid = r1.astype(BF)
    lo = (r1 - mid.astype(F32)).astype(BF)
    lane = lax.broadcasted_iota(jnp.int32, (E, HEAD_DIM), 1)
    row = lax.broadcasted_iota(jnp.int32, (E, HEAD_DIM), 0)
    a3 = (_dot(hi, (lane == row).astype(BF)) + _dot(mid, (lane == row + E).astype(BF))
          + _dot(lo, (lane == row + 2 * E).astype(BF)))
    a3_ref[...] = a3.astype(BF)


def _gate_from_split(g3, expert):
    lane = lax.broadcasted_iota(jnp.int32, g3.shape, 1)
    pick = jnp.logical_and(lane % N_EXPERTS == expert, lane < 3 * N_EXPERTS)
    return jnp.sum(jnp.where(pick, g3, 0.0), axis=1, keepdims=True)


def route(aff, N, T):
    E = N_EXPERTS
    cap = EC_CAPACITY_FACTOR * T // E
    B = min(T, TOKEN_BLOCK)
    nblk = T // B
    return pl.pallas_call(
        functools.partial(_route_kernel, cap=cap, nblk=nblk),
        grid=(N,),
        in_specs=[pl.BlockSpec((T, E), lambda n: (n, 0))],
        out_specs=[
            pl.BlockSpec((None, T, E), lambda n: (n, 0, 0)),
            pl.BlockSpec((None, nblk, E, B), lambda n: (n, 0, 0, 0)),
            pl.BlockSpec((None, nblk + 1, E), lambda n: (n, 0, 0)),
            pl.BlockSpec((None, T, HEAD_DIM), lambda n: (n, 0, 0)),
        ],
        out_shape=[
            jax.ShapeDtypeStruct((N, T, E), F32),
            jax.ShapeDtypeStruct((N, nblk, E, B), F32),
            jax.ShapeDtypeStruct((N, nblk + 1, E), jnp.int32),
            jax.ShapeDtypeStruct((N, T, HEAD_DIM), BF),
        ],
        compiler_params=_cparams("parallel"),
    )(aff)


def _gather_small_kernel(h_ref, srow_ref, a3_ref, x_all_ref, g_all_ref, x_ref, g_ref, *, cap):
    del x_all_ref, g_all_ref
    E = N_EXPERTS
    S = E * cap
    T = h_ref.shape[0]
    sr = srow_ref[...].astype(BF)
    rep_t = (lax.broadcasted_iota(jnp.int32, (S, E), 0) // cap == lax.broadcasted_iota(jnp.int32, (S, E), 1)).astype(BF)
    slot_of_row = (lax.broadcasted_iota(jnp.int32, (S, 1), 0) % cap).astype(F32)
    onehot = (_dot(rep_t, sr) == slot_of_row).astype(BF)
    x = _dot(onehot, h_ref[...])
    for e in range(E):
        x_ref[e] = x[e * cap:(e + 1) * cap].astype(x_ref.dtype)
    gate = _gate_from_split(_dot(onehot, a3_ref[...]), lax.broadcasted_iota(jnp.int32, (S, 1), 0) // cap)
    gb = jnp.broadcast_to(gate, (S, HEAD_DIM))
    for e in range(E):
        g_ref[e] = gb[e * cap:(e + 1) * cap]


def gather_small(h, srow, a3, x_all, g_all, row0, N, T):
    E = N_EXPERTS
    cap = EC_CAPACITY_FACTOR * T // E
    D = h.shape[1]
    assert row0 % cap == 0
    rb0 = row0 // cap
    return pl.pallas_call(
        functools.partial(_gather_small_kernel, cap=cap),
        grid=(N,),
        in_specs=[
            pl.BlockSpec((T, D), lambda n: (n, 0)),
            pl.BlockSpec((None, None, E, T), lambda n: (n, 0, 0, 0)),
            pl.BlockSpec((None, T, HEAD_DIM), lambda n: (n, 0, 0)),
            pl.BlockSpec(memory_space=pl.ANY),
            pl.BlockSpec(memory_space=pl.ANY),
        ],
        out_specs=[
            pl.BlockSpec((E, cap, D), lambda n: (0, rb0 + n, 0)),
            pl.BlockSpec((E, cap, HEAD_DIM), lambda n: (0, rb0 + n, 0)),
        ],
        out_shape=[
            jax.ShapeDtypeStruct(x_all.shape, x_all.dtype),
            jax.ShapeDtypeStruct(g_all.shape, g_all.dtype),
        ],
        input_output_aliases={3: 0, 4: 1},
        compiler_params=_cparams("parallel"),
    )(h, srow, a3, x_all, g_all)


GATHER_COLS = 512


def _gather_big_kernel(cnt_ref, h_ref, srow_ref, a3_ref, x_ref, g_ref, *, cap, nblk):
    E = N_EXPERTS
    W = min(SLOT_CHUNK, cap)
    n = pl.program_id(0)
    d = pl.program_id(1)
    b = pl.program_id(2)

    @pl.when(b == 0)
    def _():
        x_ref[...] = jnp.zeros_like(x_ref)

    @pl.when(jnp.logical_and(b == 0, d == 0))
    def _():
        g_ref[...] = jnp.zeros_like(g_ref)

    row = lax.broadcasted_iota(jnp.int32, (W, 1), 0)
    starts, ends, pieces = [], [], []
    for e in range(E):
        base = (n * E + e) * (nblk + 1) + b
        ws = pl.multiple_of(jnp.minimum((cnt_ref[base] // 16) * 16, cap - W), 16)
        pieces.append((srow_ref[e:e + 1, :] == (ws + row).astype(F32)).astype(BF))
        starts.append(ws)
        ends.append(cnt_ref[base + 1])
    onehot = jnp.concatenate(pieces, axis=0)
    prod = _dot(onehot, h_ref[...])
    for e in range(E):
        x_ref[e, pl.ds(starts[e], W), :] += prod[e * W:(e + 1) * W].astype(x_ref.dtype)

    @pl.when(d == 0)
    def _():
        split = _dot(onehot, a3_ref[...])
        for e in range(E):
            g_ref[e, pl.ds(starts[e], W), :] += split[e * W:(e + 1) * W]

    for e in range(E):
        def more(k, carry, e=e):
            first = starts[e] + k * W
            ws = pl.multiple_of(jnp.minimum(first, cap - W), 16)
            ids = ws + row
            piece = jnp.logical_and(srow_ref[e:e + 1, :] == ids.astype(F32), ids >= first).astype(BF)
            x_ref[e, pl.ds(ws, W), :] += _dot(piece, h_ref[...]).astype(x_ref.dtype)

            @pl.when(d == 0)
            def _():
                g_ref[e, pl.ds(ws, W), :] += _dot(piece, a3_ref[...])

            return carry

        lax.fori_loop(1, (ends[e] - starts[e] + W - 1) // W, more, 0)

    @pl.when(jnp.logical_and(b == nblk - 1, d == 0))
    def _():
        for e in range(E):
            g_ref[e] = jnp.broadcast_to(_gate_from_split(g_ref[e], e), (cap, HEAD_DIM))


def gather_big(h, srow, cnt_flat, a3, rows_total, N, T):
    E = N_EXPERTS
    cap = EC_CAPACITY_FACTOR * T // E
    D = h.shape[1]
    nblk = T // TOKEN_BLOCK
    Dc = min(D, GATHER_COLS)
    assert cap % 16 == 0
    return pl.pallas_call(
        functools.partial(_gather_big_kernel, cap=cap, nblk=nblk),
        grid_spec=pltpu.PrefetchScalarGridSpec(
            num_scalar_prefetch=1,
            grid=(N, D // Dc, nblk),
            in_specs=[
                pl.BlockSpec((TOKEN_BLOCK, Dc), lambda n, d, b, c: (n * nblk + b, d)),
                pl.BlockSpec((None, None, E, TOKEN_BLOCK), lambda n, d, b, c: (n, b, 0, 0)),
                pl.BlockSpec((None, TOKEN_BLOCK, HEAD_DIM), lambda n, d, b, c: (n, b, 0)),
            ],
            out_specs=[
                pl.BlockSpec((E, cap, Dc), lambda n, d, b, c: (0, n, d)),
                pl.BlockSpec((E, cap, HEAD_DIM), lambda n, d, b, c: (0, n, 0)),
            ],
        ),
        out_shape=[
            jax.ShapeDtypeStruct((E, rows_total, D), BF),
            jax.ShapeDtypeStruct((E, rows_total, HEAD_DIM), F32),
        ],
        compiler_params=_cparams("parallel", "arbitrary", "arbitrary"),
    )(cnt_flat, h, srow, a3)


def _gate_up_kernel(x_ref, wg_ref, wu_ref, o_ref, wg_bf, wu_bf):
    @pl.when(pl.program_id(2) == 0)
    def _():
        wg_bf[...] = wg_ref[...].astype(BF)
        wu_bf[...] = wu_ref[...].astype(BF)

    x = x_ref[...]
    a = _dot(x, wg_bf[...])
    u = _dot(x, wu_bf[...])
    o_ref[...] = ((a * jax.nn.sigmoid(a)) * u).astype(o_ref.dtype)


def _row_tile(R, limit):
    return max(t for t in range(16, min(R, limit) + 1, 16) if R % t == 0)


def expert_gate_up(x, w_gate, w_up, layer):
    E, R, D = x.shape
    F = w_gate.shape[-1]
    tr = _row_tile(R, 1536)
    tf = min(F, 512)
    return pl.pallas_call(
        _gate_up_kernel,
        grid=(E, F // tf, R // tr),
        in_specs=[
            pl.BlockSpec((None, tr, D), lambda e, f, r: (e, r, 0)),
            pl.BlockSpec((None, None, D, tf), lambda e, f, r: (layer, e, 0, f)),
            pl.BlockSpec((None, None, D, tf), lambda e, f, r: (layer, e, 0, f)),
        ],
        out_specs=pl.BlockSpec((None, tr, tf), lambda e, f, r: (e, r, f)),
        out_shape=jax.ShapeDtypeStruct((E, R, F), BF),
        scratch_shapes=[pltpu.VMEM((D, tf), BF), pltpu.VMEM((D, tf), BF)],
        compiler_params=_cparams("parallel", "parallel", "arbitrary"),
    )(x, w_gate, w_up)


def _down_kernel(h_ref, w_ref, g_ref, o_ref, w_bf):
    @pl.when(pl.program_id(2) == 0)
    def _():
        w_bf[...] = w_ref[...].astype(BF)

    o_ref[...] = (_dot(h_ref[...], w_bf[...]) * g_ref[:, 0:1]).astype(o_ref.dtype)


def expert_down(h, w_down, gate, layer):
    E, R, F = h.shape
    D = w_down.shape[-1]
    tr = _row_tile(R, 1536)
    td = min(D, 1024)
    return pl.pallas_call(
        _down_kernel,
        grid=(E, D // td, R // tr),
        in_specs=[
            pl.BlockSpec((None, tr, F), lambda e, d, r: (e, r, 0)),
            pl.BlockSpec((None, None, F, td), lambda e, d, r: (layer, e, 0, d)),
            pl.BlockSpec((None, tr, HEAD_DIM), lambda e, d, r: (e, r, 0)),
        ],
        out_specs=pl.BlockSpec((None, tr, td), lambda e, d, r: (e, r, d)),
        out_shape=jax.ShapeDtypeStruct((E, R, D), BF),
        scratch_shapes=[pltpu.VMEM((F, td), BF)],
        compiler_params=_cparams("parallel", "parallel", "arbitrary"),
    )(h, w_down, gate)


def _residual_epilogue(f, x_ref, m_ref, mn_ref, g_ref, b_ref, xo_ref, h_ref, alpha):
    z = alpha * x_ref[...] + m_ref[5:6, :] * f
    xn = _layer_norm(z, g_ref[...], b_ref[...])
    xo_ref[...] = xn
    h_ref[...] = (xn * (1.0 + mn_ref[1:2, :]) + mn_ref[0:1, :]).astype(h_ref.dtype)


def _scatter_small_kernel(y_ref, scol_ref, x_ref, m_ref, mn_ref, g_ref, b_ref, xo_ref, h_ref, *, cap, alpha):
    E = N_EXPERTS
    S = E * cap
    sc = scol_ref[...].astype(BF)
    rep = (lax.broadcasted_iota(jnp.int32, (E, S), 0) == lax.broadcasted_iota(jnp.int32, (E, S), 1) // cap).astype(BF)
    slot_of_col = (lax.broadcasted_iota(jnp.int32, (1, S), 1) % cap).astype(F32)
    onehot_t = (_dot(sc, rep) == slot_of_col).astype(BF)
    y = jnp.concatenate([y_ref[e] for e in range(E)], axis=0)
    f = _dot(onehot_t, y)
    _residual_epilogue(f, x_ref, m_ref, mn_ref, g_ref, b_ref, xo_ref, h_ref, alpha)


def scatter_small(y, y_row0, scol, x, mods, layer, next_layer, row0, per_req, ln_g, ln_b, N, T, alpha):
    E = N_EXPERTS
    cap = EC_CAPACITY_FACTOR * T // E
    D = x.shape[1]
    assert y_row0 % cap == 0
    return pl.pallas_call(
        functools.partial(_scatter_small_kernel, cap=cap, alpha=alpha),
        grid=(N,),
        in_specs=[
            pl.BlockSpec((E, cap, D), lambda n: (0, y_row0 // cap + n, 0)),
            pl.BlockSpec((None, T, E), lambda n: (n, 0, 0)),
            pl.BlockSpec((T, D), lambda n: (n, 0)),
            pl.BlockSpec((None, None, 6, D), lambda n: (layer, row0 + n * per_req, 0, 0)),
            pl.BlockSpec((None, None, 6, D), lambda n: (next_layer, row0 + n * per_req, 0, 0)),
            pl.BlockSpec((None, 1, D), lambda n: (layer, 0, 0)),
            pl.BlockSpec((None, 1, D), lambda n: (layer, 0, 0)),
        ],
        out_specs=[pl.BlockSpec((T, D), lambda n: (n, 0)), pl.BlockSpec((T, D), lambda n: (n, 0))],
        out_shape=[jax.ShapeDtypeStruct((N * T, D), F32), jax.ShapeDtypeStruct((N * T, D), BF)],
        compiler_params=_cparams("parallel"),
    )(y, scol, x, mods, mods, ln_g, ln_b)


def _scatter_big_kernel(cnt_ref, y_ref, scol_ref, x_ref, m_ref, mn_ref, g_ref, b_ref, xo_ref, h_ref, acc,
                        *, nblk, tiles, alpha):
    E = N_EXPERTS
    C = SLOT_CHUNK
    B = TOKEN_BLOCK
    G = y_ref.shape[0]
    cap = y_ref.shape[1]
    WIN = min(C, cap)
    n = pl.program_id(0)
    tb = pl.program_id(1)
    g = pl.program_id(2)

    @pl.when(g == 0)
    def _():
        acc[...] = jnp.zeros_like(acc)

    lane = lax.broadcasted_iota(jnp.int32, (B, E), 1)
    for bl in range(tiles):
        blk = tb * tiles + bl
        sc = scol_ref[bl * B:(bl + 1) * B, :]
        hits, wins = [], []
        for gi in range(G):
            e = g * G + gi
            base = (n * E + e) * (nblk + 1)
            lo = cnt_ref[base + blk]
            hi = cnt_ref[base + blk + 1]
            scol = jnp.sum(jnp.where(lane == e, sc, 0.0), axis=1, keepdims=True)
            ws = pl.multiple_of(jnp.minimum((lo // 16) * 16, cap - WIN), 16)
            col = lax.broadcasted_iota(jnp.int32, (1, WIN), 1)
            hits.append((scol == (ws + col).astype(F32)).astype(BF))
            wins.append(y_ref[gi, pl.ds(ws, WIN), :])

            def more(k, carry, gi=gi, scol=scol, ws=ws, col=col):
                first = ws + k * WIN
                ks = pl.multiple_of(jnp.minimum(first, cap - WIN), 16)
                ids = ks + col
                hit = jnp.logical_and(scol == ids.astype(F32), ids >= first)
                acc[bl * B:(bl + 1) * B, :] += _dot(hit.astype(BF), y_ref[gi, pl.ds(ks, WIN), :])
                return carry

            lax.fori_loop(1, (hi - ws + WIN - 1) // WIN, more, 0)
        f = None
        for p in range(0, G, 2):
            d = _dot(jnp.concatenate(hits[p:p + 2], axis=1), jnp.concatenate(wins[p:p + 2], axis=0))
            f = d if f is None else f + d
        acc[bl * B:(bl + 1) * B, :] += f

    @pl.when(g == pl.num_programs(2) - 1)
    def _():
        _residual_epilogue(acc[...], x_ref, m_ref, mn_ref, g_ref, b_ref, xo_ref, h_ref, alpha)


SCATTER_EXPERT_GROUP = 4


def scatter_big(y, y_row0, scol, cnt_flat, x, mods, layer, next_layer, row0, per_req, ln_g, ln_b, N, T, alpha):
    E = N_EXPERTS
    cap = EC_CAPACITY_FACTOR * T // E
    D = x.shape[1]
    nblk = T // TOKEN_BLOCK
    tiles = 2 if nblk % 2 == 0 else 1
    tt = tiles * TOKEN_BLOCK
    ntt = T // tt
    G = SCATTER_EXPERT_GROUP
    assert y_row0 % cap == 0 and cap % 16 == 0
    return pl.pallas_call(
        functools.partial(_scatter_big_kernel, nblk=nblk, tiles=tiles, alpha=alpha),
        grid_spec=pltpu.PrefetchScalarGridSpec(
            num_scalar_prefetch=1,
            grid=(N, ntt, E // G),
            in_specs=[
                pl.BlockSpec((G, cap, D), lambda n, t, e, c: (e, y_row0 // cap + n, 0)),
                pl.BlockSpec((None, tt, E), lambda n, t, e, c: (n, t, 0)),
                pl.BlockSpec((tt, D), lambda n, t, e, c: (n * ntt + t, 0)),
                pl.BlockSpec((None, None, 6, D), lambda n, t, e, c: (layer, row0 + n * per_req, 0, 0)),
                pl.BlockSpec((None, None, 6, D), lambda n, t, e, c: (next_layer, row0 + n * per_req, 0, 0)),
                pl.BlockSpec((None, 1, D), lambda n, t, e, c: (layer, 0, 0)),
                pl.BlockSpec((None, 1, D), lambda n, t, e, c: (layer, 0, 0)),
            ],
            out_specs=[
                pl.BlockSpec((tt, D), lambda n, t, e, c: (n * ntt + t, 0)),
                pl.BlockSpec((tt, D), lambda n, t, e, c: (n * ntt + t, 0)),
            ],
            scratch_shapes=[pltpu.VMEM((tt, D), F32)],
        ),
        out_shape=[jax.ShapeDtypeStruct((N * T, D), F32), jax.ShapeDtypeStruct((N * T, D), BF)],
        compiler_params=_cparams("parallel", "parallel", "arbitrary"),
    )(cnt_flat, y, scol, x, mods, mods, ln_g, ln_b)


def moe_layer(groups, mods, layer, next_layer, ln_g, ln_b, w_gate, w_up, w_down, alpha):
    E = N_EXPERTS
    rows_total = sum(gr["N"] * (EC_CAPACITY_FACTOR * gr["T"] // E) for gr in groups)
    routed, row0s, rows, x_all, g_all = [], [], 0, None, None
    for gr in groups:
        N, T = gr["N"], gr["T"]
        scol, srow, cnt, a3 = route(gr["aff"], N, T)
        if T <= TOKEN_BLOCK:
            assert x_all is not None
            cnt_flat = None
            x_all, g_all = gather_small(gr["h"], srow, a3, x_all, g_all, rows, N, T)
        else:
            assert x_all is None
            cnt_flat = jnp.transpose(cnt, (0, 2, 1)).reshape(-1)
            x_all, g_all = gather_big(gr["h"], srow, cnt_flat, a3, rows_total, N, T)
        routed.append((scol, cnt_flat))
        row0s.append(rows)
        rows += N * (EC_CAPACITY_FACTOR * T // E)
    hm = expert_gate_up(x_all, w_gate, w_up, layer)
    y = expert_down(hm, w_down, g_all, layer)
    out = []
    for gr, (scol, cnt_flat), y_row0 in zip(groups, routed, row0s):
        tail = (gr["x"], mods, layer, next_layer, gr["row0"], gr["per_req"], ln_g, ln_b, gr["N"], gr["T"], alpha)
        if cnt_flat is None:
            out.append(scatter_small(y, y_row0, scol, *tail))
        else:
            out.append(scatter_big(y, y_row0, scol, cnt_flat, *tail))
    return out


def kernel(x_prompt, x_sample, cache_a_k, cache_a_v, cache_b_k, cache_b_v, cache_c_k, cache_c_v, c, c_ctx, w_mod, b_mod, w_in_even, w_out_even, a_sink, na_rpb, w_in_odd, w_out_odd, diff_lambda, diff_subln, ln_g, ln_b, w_router, w_gate, w_up, w_down):
    NP, TP, D = x_prompt.shape
    NS, TS, _ = x_sample.shape
    depth = w_mod.shape[0]
    L = cache_a_k.shape[2]
    alpha = (2 * depth) ** 0.25
    n_even = (depth + 1) // 2
    n_odd = depth // 2

    rows = 1 + NS
    rpad = -rows % 8
    cond = jnp.concatenate([c_ctx[None], c, jnp.zeros((rpad, D), F32)], axis=0)
    mods = adaln_all(cond, w_mod, b_mod).reshape(depth, rows + rpad, 6, D)

    rope = rope_tables(TS)
    sinks = a_sink.reshape(-1)
    rpb_tiles = toeplitz_bias(na_rpb)
    w_out_even_bf = w_out_even.astype(BF)
    w_out_odd_bf = w_out_odd.astype(BF)
    w_router_bf = w_router.astype(BF)
    ln_g4 = ln_g.reshape(depth, 2, 1, D)
    ln_b4 = ln_b.reshape(depth, 2, 1, D)
    ca_k = cache_a_k.reshape(NS, n_even, L, A_KV_HEADS * HEAD_DIM)
    ca_v = cache_a_v.reshape(NS, n_even, L, A_KV_HEADS * HEAD_DIM)
    cb_k = cache_b_k.reshape(NS, n_even, L, B_WIDTH)
    cb_v = cache_b_v.reshape(NS, n_even, L, B_WIDTH)
    cc_k = cache_c_k.reshape(NS, n_odd, L, C_HEADS * 2 * HEAD_DIM)
    cc_v = cache_c_v.reshape(NS, n_odd, L, C_HEADS * C_V_DIM)

    xp = x_prompt.reshape(NP * TP, D)
    xs = x_sample.reshape(NS * TS, D)
    hp = modulate_rows(xp, mods, 0, 0, 0, TP)
    hs = modulate_rows(xs, mods, 0, 1, 1, TS)

    new_a_k, new_a_v, new_b_k, new_b_v, new_c_k, new_c_v = [], [], [], [], [], []
    QA = A_Q_HEADS * HEAD_DIM
    KA = A_KV_HEADS * HEAD_DIM
    for l in range(depth):
        if l % 2 == 0:
            e = l // 2
            b0 = A_WIDTH
            pa = project(hp, w_in_even, e, 0, A_WIDTH, 0, None, TP, F32)
            pqb = project(hp, w_in_even, e, b0, B_WIDTH, 0, None, TP, F32)
            pkb = project(hp, w_in_even, e, b0 + B_WIDTH, B_WIDTH, 0, None, TP, F32)
            pvb = project(hp, w_in_even, e, b0 + 2 * B_WIDTH, B_WIDTH, 0, None, TP, F32)
            new_a_k.append(pa[:, QA:QA + KA].reshape(NP, TP, A_KV_HEADS, HEAD_DIM))
            new_a_v.append(pa[:, QA + KA:QA + 2 * KA].reshape(NP, TP, A_KV_HEADS, HEAD_DIM))
            new_b_k.append(pkb.reshape(NP, TP, B_HEADS, HEAD_DIM))
            new_b_v.append(pvb.reshape(NP, TP, B_HEADS, HEAD_DIM))
            op = [ctx_even_attention(pa, pqb, pkb, pvb, sinks, e, NP, TP)]

            sa = project(hs, w_in_even, e, 0, A_WIDTH, A_Q_HEADS + A_KV_HEADS, rope, TS, BF, A_Q_HEADS)
            sqb = project(hs, w_in_even, e, b0, B_WIDTH, 0, rope, TS, BF, B_HEADS)
            skb = project(hs, w_in_even, e, b0 + B_WIDTH, B_WIDTH, 0, rope, TS, BF)
            svb = project(hs, w_in_even, e, b0 + 2 * B_WIDTH, B_WIDTH, 0, rope, TS, BF)
            oa = window_attention(sa, ca_k, ca_v, sinks, e, NS, TS)
            bias3 = neighbourhood_bias(rpb_tiles[e], TS)
            ob = neighbourhood_attention(sqb, skb, svb, cb_k, cb_v, bias3, e, NS, TS)
            os_ = [oa, ob]
            w_out, widx = w_out_even_bf, e
        else:
            o = l // 2
            lambda_init = 0.8 - 0.6 * math.exp(-0.3 * l)
            W = C_HEADS * C_V_DIM
            pq = project(hp, w_in_odd, o, 0, W, 0, None, TP, F32)
            pk = project(hp, w_in_odd, o, W, W, 0, None, TP, F32)
            pv = project(hp, w_in_odd, o, 2 * W, W, 0, None, TP, F32)
            new_c_k.append(pk.reshape(NP, TP, C_HEADS, 2, HEAD_DIM))
            new_c_v.append(pv.reshape(NP, TP, C_HEADS, C_V_DIM))
            op = [diff_attention(pq, pk, pv, diff_lambda, diff_subln, lambda_init, NP, TP, None, o)]

            sq = project(hs, w_in_odd, o, 0, W, 2 * C_HEADS, rope, TS, BF, 2 * C_HEADS)
            sk = project(hs, w_in_odd, o, W, W, 2 * C_HEADS, rope, TS, BF)
            sv = project(hs, w_in_odd, o, 2 * W, W, 0, rope, TS, BF)
            os_ = [diff_attention_pipelined(sq, sk, sv, diff_lambda, diff_subln, lambda_init, NS, TS, (cc_k, cc_v), o)]
            w_out, widx = w_out_odd_bf, o

        lg1, lb1 = ln_g4[:, 0], ln_b4[:, 0]
        lg2, lb2 = ln_g4[:, 1], ln_b4[:, 1]
        xp, hp, affp = outproj_ln_router(op, w_out, widx, xp, mods, l, 0, 0, TP, lg1, lb1, w_router_bf, alpha)
        xs, hs, affs = outproj_ln_router(os_, w_out, widx, xs, mods, l, 1, 1, TS, lg1, lb1, w_router_bf, alpha)
        nl = min(l + 1, depth - 1)
        groups = [dict(aff=affs, h=hs, x=xs, N=NS, T=TS, row0=1, per_req=1),
                  dict(aff=affp, h=hp, x=xp, N=NP, T=TP, row0=0, per_req=0)]
        (xs, hs), (xp, hp) = moe_layer(groups, mods, l, nl, lg2, lb2, w_gate, w_up, w_down, alpha)

    return (xp.reshape(NP, TP, D), xs.reshape(NS, TS, D),
            jnp.stack(new_a_k, axis=1), jnp.stack(new_a_v, axis=1),
            jnp.stack(new_b_k, axis=1), jnp.stack(new_b_v, axis=1),
            jnp.stack(new_c_k, axis=1), jnp.stack(new_c_v, axis=1))
```

```python
import functools
import math

import jax
import jax.numpy as jnp
from jax import lax
from jax.experimental import pallas as pl
from jax.experimental.pallas import tpu as pltpu

BF = jnp.bfloat16
F32 = jnp.float32

GRID_W = 64
HEAD_DIM = 128
A_Q_HEADS = 8
A_KV_HEADS = 2
A_GROUP = A_Q_HEADS // A_KV_HEADS
A_BLOCK = 128
B_HEADS = 8
NA_ROWS = 8
NA_COLS = 16
C_HEADS = 8
C_V_DIM = 2 * HEAD_DIM
N_EXPERTS = 16
EC_CAPACITY_FACTOR = 2
ROPE_BASE = 10000.0
LN_EPS = 1e-5
RMS_EPS = 1e-5
ATTN_SCALE = HEAD_DIM ** -0.5
A_WIDTH = (A_Q_HEADS + 2 * A_KV_HEADS) * HEAD_DIM
B_WIDTH = B_HEADS * HEAD_DIM

VMEM_LIMIT_BYTES = 52 * 1024 * 1024
TOKEN_BLOCK = 256
SLOT_CHUNK = 128


def _cparams(*sem):
    return pltpu.CompilerParams(dimension_semantics=sem, vmem_limit_bytes=VMEM_LIMIT_BYTES)


def _nt_dot(a, b):
    return lax.dot_general(a, b, (((1,), (1,)), ((), ())), preferred_element_type=F32)


def _dot(a, b):
    return jnp.dot(a, b, preferred_element_type=F32)


def _layer_norm(z, g, b):
    mu = jnp.mean(z, -1, keepdims=True)
    d = z - mu
    var = jnp.mean(d * d, -1, keepdims=True)
    return d * lax.rsqrt(var + LN_EPS) * g + b


def _adaln_kernel(c_ref, w_ref, b_ref, o_ref):
    c = c_ref[...]
    s = (c * jax.nn.sigmoid(c)).astype(BF)
    o_ref[...] = _dot(s, w_ref[...].astype(BF)) + b_ref[...]


def adaln_all(cond, w_mod, b_mod):
    L, D, D6 = w_mod.shape
    R = cond.shape[0]
    tn = D // 2
    return pl.pallas_call(
        _adaln_kernel,
        grid=(L, D6 // tn),
        in_specs=[
            pl.BlockSpec((R, D), lambda l, j: (0, 0)),
            pl.BlockSpec((None, D, tn), lambda l, j: (l, 0, j)),
            pl.BlockSpec((None, 1, tn), lambda l, j: (l, 0, j)),
        ],
        out_specs=pl.BlockSpec((None, R, tn), lambda l, j: (l, 0, j)),
        out_shape=jax.ShapeDtypeStruct((L, R, D6), F32),
        compiler_params=_cparams("parallel", "parallel"),
    )(cond, w_mod, b_mod.reshape(L, 1, D6))


def _modulate_kernel(x_ref, m_ref, o_ref):
    o_ref[...] = (x_ref[...] * (1.0 + m_ref[1:2, :]) + m_ref[0:1, :]).astype(o_ref.dtype)


def modulate_rows(x, mods, layer, row0, per_req, T):
    M, D = x.shape
    tt = min(T, 512)
    return pl.pallas_call(
        _modulate_kernel,
        grid=(M // tt,),
        in_specs=[
            pl.BlockSpec((tt, D), lambda i: (i, 0)),
            pl.BlockSpec((None, None, 6, D), lambda i: (layer, row0 + ((i * tt) // T) * per_req, 0, 0)),
        ],
        out_specs=pl.BlockSpec((tt, D), lambda i: (i, 0)),
        out_shape=jax.ShapeDtypeStruct((M, D), BF),
        compiler_params=_cparams("parallel"),
    )(x, mods)


def _swap32(x):
    lane = lax.broadcasted_iota(jnp.int32, (1, HEAD_DIM), 1)
    return jnp.where((lane % 64) < 32, pltpu.roll(x, HEAD_DIM - 32, 1), pltpu.roll(x, 32, 1))


def _proj_kernel(*refs, n_rope_heads, n_scaled_heads, heads_per_tile):
    if n_rope_heads:
        h_ref, w_ref, cos_ref, sin_ref, o_ref, wbf = refs
    else:
        h_ref, w_ref, o_ref, wbf = refs
    j = pl.program_id(0)
    i = pl.program_id(1)

    @pl.when(i == 0)
    def _():
        wbf[...] = w_ref[...].astype(BF)

    acc = _dot(h_ref[...], wbf[...])
    bounds = sorted({0, min(n_rope_heads, n_scaled_heads), n_rope_heads, n_scaled_heads, 1 << 30})
    for hh in range(heads_per_tile):
        sl = slice(hh * HEAD_DIM, (hh + 1) * HEAD_DIM)
        x = acc[:, sl]
        head = j * heads_per_tile + hh
        for lo, hi in zip(bounds[:-1], bounds[1:]):
            rot = lo < n_rope_heads
            scale = Q_PRESCALE if lo < n_scaled_heads else None

            def emit(x=x, sl=sl, rot=rot, scale=scale):
                y = x * cos_ref[...] + _swap32(x) * sin_ref[...] if rot else x
                o_ref[:, sl] = (y if scale is None else y * scale).astype(o_ref.dtype)

            if len(bounds) == 2:
                emit()
            else:
                pl.when(jnp.logical_and(head >= lo, head < hi))(emit)


def project(h, w, widx, col0, width, n_rope_heads, rope, T, out_dtype, n_scaled_heads=0):
    M, D = h.shape
    heads = width // HEAD_DIM
    hpt = max(k for k in range(1, 9) if heads % k == 0 and col0 % (k * HEAD_DIM) == 0)
    tn = hpt * HEAD_DIM
    tm = min(T if n_rope_heads else M, 1024)
    tpb = max(T // tm, 1)
    in_specs = [
        pl.BlockSpec((tm, D), lambda j, i: (i, 0)),
        pl.BlockSpec((None, D, tn), lambda j, i: (widx, 0, col0 // tn + j)),
    ]
    args = [h, w]
    if n_rope_heads:
        in_specs += [pl.BlockSpec((tm, HEAD_DIM), lambda j, i: (i % tpb, 0))] * 2
        args += list(rope)
    return pl.pallas_call(
        functools.partial(_proj_kernel, n_rope_heads=n_rope_heads, n_scaled_heads=n_scaled_heads, heads_per_tile=hpt),
        grid=(width // tn, M // tm),
        in_specs=in_specs,
        out_specs=pl.BlockSpec((tm, tn), lambda j, i: (i, j)),
        out_shape=jax.ShapeDtypeStruct((M, width), out_dtype),
        scratch_shapes=[pltpu.VMEM((D, tn), BF)],
        compiler_params=_cparams("parallel", "arbitrary"),
    )(*args)


def rope_tables(T):
    t = jnp.arange(T)
    row = (t // GRID_W).astype(F32)
    col = (t % GRID_W).astype(F32)
    axis_dim = HEAD_DIM // 2
    inv_freq = ROPE_BASE ** (-jnp.arange(0, axis_dim, 2, dtype=F32) / axis_dim)
    ar = row[:, None] * inv_freq
    ac = col[:, None] * inv_freq
    cos = jnp.concatenate([jnp.cos(ar), jnp.cos(ar), jnp.cos(ac), jnp.cos(ac)], -1)
    sin = jnp.concatenate([-jnp.sin(ar), jnp.sin(ar), -jnp.sin(ac), jnp.sin(ac)], -1)
    return cos, sin


LOG2E = math.log2(math.e)
Q_PRESCALE = ATTN_SCALE * LOG2E


def _exp2_parts(parts, sink=None):
    m = None
    for s in parts:
        mi = jnp.max(s, -1, keepdims=True)
        m = mi if m is None else jnp.maximum(m, mi)
    if sink is not None:
        m = jnp.maximum(m, sink)
    es = [jnp.exp2(s - m) for s in parts]
    den = None
    for e in es:
        di = jnp.sum(e, -1, keepdims=True)
        den = di if den is None else den + di
    if sink is not None:
        den = den + jnp.exp2(sink - m)
    return es, 1.0 / den


def _softmax_parts(parts, sink=None):
    m = None
    for s in parts:
        mi = jnp.max(s, -1, keepdims=True)
        m = mi if m is None else jnp.maximum(m, mi)
    if sink is not None:
        m = jnp.maximum(m, sink)
    es = [jnp.exp(s - m) for s in parts]
    den = None
    for e in es:
        di = jnp.sum(e, -1, keepdims=True)
        den = di if den is None else den + di
    if sink is not None:
        den = den + jnp.exp(sink - m)
    inv = 1.0 / den
    return [e * inv for e in es]


def _ctx_even_kernel(sink_ref, a_ref, qb_ref, kb_ref, vb_ref, o_ref, *, sink_base):
    T = a_ref.shape[0]
    for kv in range(A_KV_HEADS):
        k = a_ref[:, (A_Q_HEADS + kv) * HEAD_DIM:(A_Q_HEADS + kv + 1) * HEAD_DIM].astype(BF)
        v = a_ref[:, (A_Q_HEADS + A_KV_HEADS + kv) * HEAD_DIM:(A_Q_HEADS + A_KV_HEADS + kv + 1) * HEAD_DIM].astype(BF)
        for g in range(A_GROUP):
            hq = kv * A_GROUP + g
            q = a_ref[:, hq * HEAD_DIM:(hq + 1) * HEAD_DIM].astype(BF)
            s = _nt_dot(q, k) * ATTN_SCALE
            sink = jnp.full((T, 1), sink_ref[sink_base + hq], F32)
            (p,) = _softmax_parts([s], sink)
            o_ref[:, hq * HEAD_DIM:(hq + 1) * HEAD_DIM] = _dot(p.astype(BF), v).astype(o_ref.dtype)
    for hb in range(B_HEADS):
        sl = slice(hb * HEAD_DIM, (hb + 1) * HEAD_DIM)
        s = _nt_dot(qb_ref[:, sl].astype(BF), kb_ref[:, sl].astype(BF)) * ATTN_SCALE
        (p,) = _softmax_parts([s])
        o_ref[:, A_Q_HEADS * HEAD_DIM + hb * HEAD_DIM:A_Q_HEADS * HEAD_DIM + (hb + 1) * HEAD_DIM] = _dot(
            p.astype(BF), vb_ref[:, sl].astype(BF)).astype(o_ref.dtype)


def ctx_even_attention(arr_a, qb, kb, vb, sinks, e, N, T):
    M = N * T
    return pl.pallas_call(
        functools.partial(_ctx_even_kernel, sink_base=e * A_Q_HEADS),
        grid=(N,),
        in_specs=[
            pl.BlockSpec(memory_space=pltpu.SMEM),
            pl.BlockSpec((T, A_WIDTH), lambda n: (n, 0)),
            pl.BlockSpec((T, B_WIDTH), lambda n: (n, 0)),
            pl.BlockSpec((T, B_WIDTH), lambda n: (n, 0)),
            pl.BlockSpec((T, B_WIDTH), lambda n: (n, 0)),
        ],
        out_specs=pl.BlockSpec((T, A_Q_HEADS * HEAD_DIM + B_WIDTH), lambda n: (n, 0)),
        out_shape=jax.ShapeDtypeStruct((M, A_Q_HEADS * HEAD_DIM + B_WIDTH), BF),
        compiler_params=_cparams("parallel"),
    )(sinks, arr_a, qb, kb, vb)


def _win_kernel(sink_ref, q_ref, kvp_ref, kvc_ref, kvn_ref, kc_ref, vc_ref, o_ref, *, sink_base, nb):
    i = pl.program_id(1)
    R = A_GROUP * A_BLOCK
    qq = lax.broadcasted_iota(jnp.int32, (R, 1), 0) % A_BLOCK
    kk = lax.broadcasted_iota(jnp.int32, (1, A_BLOCK), 1)
    ok_prev = jnp.logical_and(kk >= qq, i > 0)
    ok_next = jnp.logical_and(kk <= qq, i < nb - 1)
    for kv in range(A_KV_HEADS):
        ksl = slice(kv * HEAD_DIM, (kv + 1) * HEAD_DIM)
        vsl = slice((A_KV_HEADS + kv) * HEAD_DIM, (A_KV_HEADS + kv + 1) * HEAD_DIM)
        q = jnp.concatenate(
            [q_ref[:, (kv * A_GROUP + g) * HEAD_DIM:(kv * A_GROUP + g + 1) * HEAD_DIM] for g in range(A_GROUP)], axis=0)
        sink = jnp.concatenate(
            [jnp.full((A_BLOCK, 1), sink_ref[sink_base + kv * A_GROUP + g] * LOG2E, F32) for g in range(A_GROUP)],
            axis=0)
        s_p = jnp.where(ok_prev, _nt_dot(q, kvp_ref[:, ksl]), -jnp.inf)
        s_c = _nt_dot(q, kvc_ref[:, ksl])
        s_n = jnp.where(ok_next, _nt_dot(q, kvn_ref[:, ksl]), -jnp.inf)
        s_x = _nt_dot(q, kc_ref[:, ksl].astype(BF))
        (e_p, e_c, e_n, e_x), inv = _exp2_parts([s_p, s_c, s_n, s_x], sink)
        o = (_dot(e_p.astype(BF), kvp_ref[:, vsl]) + _dot(e_c.astype(BF), kvc_ref[:, vsl])
             + _dot(e_n.astype(BF), kvn_ref[:, vsl]) + _dot(e_x.astype(BF), vc_ref[:, ksl].astype(BF))) * inv
        for g in range(A_GROUP):
            hq = kv * A_GROUP + g
            o_ref[:, hq * HEAD_DIM:(hq + 1) * HEAD_DIM] = o[g * A_BLOCK:(g + 1) * A_BLOCK].astype(o_ref.dtype)


def window_attention(arr_a, cache_k, cache_v, sinks, e, N, T):
    M = N * T
    nb = T // A_BLOCK
    L = cache_k.shape[2]
    QW = A_Q_HEADS * HEAD_DIM
    KVW = 2 * A_KV_HEADS * HEAD_DIM
    kvblk = QW // KVW
    return pl.pallas_call(
        functools.partial(_win_kernel, sink_base=e * A_Q_HEADS, nb=nb),
        grid=(N, nb),
        in_specs=[
            pl.BlockSpec(memory_space=pltpu.SMEM),
            pl.BlockSpec((A_BLOCK, QW), lambda n, i: (n * nb + i, 0)),
            pl.BlockSpec((A_BLOCK, KVW), lambda n, i: (n * nb + jnp.maximum(i - 1, 0), kvblk)),
            pl.BlockSpec((A_BLOCK, KVW), lambda n, i: (n * nb + i, kvblk)),
            pl.BlockSpec((A_BLOCK, KVW), lambda n, i: (n * nb + jnp.minimum(i + 1, nb - 1), kvblk)),
            pl.BlockSpec((None, None, L, A_KV_HEADS * HEAD_DIM), lambda n, i: (n, e, 0, 0)),
            pl.BlockSpec((None, None, L, A_KV_HEADS * HEAD_DIM), lambda n, i: (n, e, 0, 0)),
        ],
        out_specs=pl.BlockSpec((A_BLOCK, QW), lambda n, i: (n * nb + i, 0)),
        out_shape=jax.ShapeDtypeStruct((M, QW), BF),
        compiler_params=_cparams("parallel", "parallel"),
    )(sinks, arr_a, arr_a, arr_a, arr_a, cache_k, cache_v)


NB_Q_ROWS = TOKEN_BLOCK // GRID_W
NB_WIN_BLOCKS = 3


def _toeplitz_kernel(rpb_ref, o_ref):
    K = rpb_ref.shape[1]
    W2 = GRID_W * GRID_W
    col = lax.broadcasted_iota(jnp.int32, (K, W2), 1)
    cq = col // GRID_W
    ck = col % GRID_W
    dc = jnp.clip(ck - cq + (NA_COLS - 1), 0, 2 * NA_COLS - 2)
    onehot = (lax.broadcasted_iota(jnp.int32, (K, W2), 0) == dc).astype(BF)
    t = rpb_ref[...]
    hi = t.astype(BF)
    r1 = t - hi.astype(F32)
    mid = r1.astype(BF)
    lo = (r1 - mid.astype(F32)).astype(BF)
    val = _dot(hi, onehot) + _dot(mid, onehot) + _dot(lo, onehot)
    c0 = jnp.clip(cq[0:1] - NA_COLS // 2, 0, GRID_W - NA_COLS)
    ok = jnp.logical_and(ck[0:1] >= c0, ck[0:1] < c0 + NA_COLS)
    o_ref[...] = jnp.where(ok, val * LOG2E, -jnp.inf)


def toeplitz_bias(rpb):
    G, H, NR, NC = rpb.shape
    R = G * H * NR
    K = 32
    flat = jnp.pad(rpb.reshape(R, NC), ((0, -R % 8), (0, K - NC)))
    out = pl.pallas_call(
        _toeplitz_kernel,
        out_shape=jax.ShapeDtypeStruct((flat.shape[0], GRID_W * GRID_W), F32),
        compiler_params=pltpu.CompilerParams(vmem_limit_bytes=VMEM_LIMIT_BYTES),
    )(flat)
    return out[:R].reshape(G, H, NR, GRID_W, GRID_W)


def neighbourhood_bias(tiles, T):
    rows = T // GRID_W
    nblk = T // TOKEN_BLOCK
    H = tiles.shape[0]
    masked = jnp.full((H, GRID_W, GRID_W), -jnp.inf, F32)
    out = []
    for blk in (0, 1, nblk - 1):
        w0 = min(max(blk - 1, 0), nblk - NB_WIN_BLOCKS) * NB_Q_ROWS
        qrows = []
        for ql in range(NB_Q_ROWS):
            rq = blk * NB_Q_ROWS + ql
            r0 = min(max(rq - NA_ROWS // 2, 0), rows - NA_ROWS)
            krow = []
            for kl in range(NB_WIN_BLOCKS * NB_Q_ROWS):
                rk = w0 + kl
                krow.append(tiles[:, rk - rq + NA_ROWS - 1] if r0 <= rk < r0 + NA_ROWS else masked)
            qrows.append(jnp.concatenate(krow, axis=-1))
        out.append(jnp.concatenate(qrows, axis=-2))
    return jnp.stack(out, axis=0)


def _nbr_kernel(q_ref, k0_ref, k1_ref, k2_ref, v0_ref, v1_ref, v2_ref, kc_ref, vc_ref, bias_ref, o_ref):
    for h in range(B_HEADS):
        sl = slice(h * HEAD_DIM, (h + 1) * HEAD_DIM)
        q = q_ref[:, sl]
        parts = []
        for w, k_ref in enumerate((k0_ref, k1_ref, k2_ref)):
            b = bias_ref[h, :, w * TOKEN_BLOCK:(w + 1) * TOKEN_BLOCK]
            parts.append(_nt_dot(q, k_ref[:, sl]) + b)
        parts.append(_nt_dot(q, kc_ref[:, sl].astype(BF)))
        (e0, e1, e2, ex), inv = _exp2_parts(parts)
        o = (_dot(e0.astype(BF), v0_ref[:, sl]) + _dot(e1.astype(BF), v1_ref[:, sl])
             + _dot(e2.astype(BF), v2_ref[:, sl]) + _dot(ex.astype(BF), vc_ref[:, sl].astype(BF))) * inv
        o_ref[:, sl] = o.astype(o_ref.dtype)


def neighbourhood_attention(qb, kb, vb, cache_k, cache_v, bias3, e, N, T):
    M = N * T
    nblk = T // TOKEN_BLOCK
    L = cache_k.shape[2]

    def win(w):
        return lambda n, i: (n * nblk + jnp.clip(i - 1, 0, nblk - NB_WIN_BLOCKS) + w, 0)

    def case(n, i):
        return (jnp.where(i == 0, 0, jnp.where(i == nblk - 1, 2, 1)), 0, 0, 0)

    blk = pl.BlockSpec((TOKEN_BLOCK, B_WIDTH), lambda n, i: (n * nblk + i, 0))
    return pl.pallas_call(
        _nbr_kernel,
        grid=(N, nblk),
        in_specs=[
            blk,
            pl.BlockSpec((TOKEN_BLOCK, B_WIDTH), win(0)),
            pl.BlockSpec((TOKEN_BLOCK, B_WIDTH), win(1)),
            pl.BlockSpec((TOKEN_BLOCK, B_WIDTH), win(2)),
            pl.BlockSpec((TOKEN_BLOCK, B_WIDTH), win(0)),
            pl.BlockSpec((TOKEN_BLOCK, B_WIDTH), win(1)),
            pl.BlockSpec((TOKEN_BLOCK, B_WIDTH), win(2)),
            pl.BlockSpec((None, None, L, B_WIDTH), lambda n, i: (n, e, 0, 0)),
            pl.BlockSpec((None, None, L, B_WIDTH), lambda n, i: (n, e, 0, 0)),
            pl.BlockSpec((None, B_HEADS, TOKEN_BLOCK, NB_WIN_BLOCKS * TOKEN_BLOCK), case),
        ],
        out_specs=blk,
        out_shape=jax.ShapeDtypeStruct((M, B_WIDTH), BF),
        compiler_params=_cparams("parallel", "arbitrary"),
    )(qb, kb, kb, kb, vb, vb, vb, cache_k, cache_v, bias3)


DIFF_ROW_SPLIT = 1


def _diff_kernel(*refs, has_ctx, lambda_init, q_prescaled):
    if has_ctx:
        lam_ref, g_ref, q_ref, k_ref, v_ref, kc_ref, vc_ref, o_ref = refs
    else:
        lam_ref, g_ref, q_ref, k_ref, v_ref, o_ref = refs
    lp = lam_ref[...]
    lam = (jnp.exp(jnp.sum(lp[0:1] * lp[1:2], -1, keepdims=True))
           - jnp.exp(jnp.sum(lp[2:3] * lp[3:4], -1, keepdims=True)) + lambda_init)
    tq = q_ref.shape[0]
    sub = tq // DIFF_ROW_SPLIT
    for r0 in range(0, tq, sub):
        rows = slice(r0, r0 + sub)
        nums, invs = [], []
        for half in range(2):
            sl = slice(half * HEAD_DIM, (half + 1) * HEAD_DIM)
            if q_prescaled:
                q = q_ref[rows, sl]
            else:
                q = (q_ref[rows, sl].astype(F32) * Q_PRESCALE).astype(BF)
            parts = [_nt_dot(q, k_ref[:, sl].astype(BF))]
            if has_ctx:
                parts.append(_nt_dot(q, kc_ref[:, sl].astype(BF)))
            es, inv = _exp2_parts(parts)
            nums.append(es)
            invs.append(inv)
        c0 = invs[0]
        c1 = lam * invs[1]
        a = (nums[0][0] * c0 - nums[1][0] * c1).astype(BF)
        o = _dot(a, v_ref[...].astype(BF))
        if has_ctx:
            ac = (nums[0][1] * c0 - nums[1][1] * c1).astype(BF)
            o = o + _dot(ac, vc_ref[...].astype(BF))
        of = o * lax.rsqrt(jnp.mean(o * o, -1, keepdims=True) + RMS_EPS)
        o_ref[rows, :] = ((of * g_ref[...]) * (1.0 - lambda_init)).astype(o_ref.dtype)


def _diff_lambda(lam_ref, lambda_init):
    lp = lam_ref[...]
    return (jnp.exp(jnp.sum(lp[0:1] * lp[1:2], -1, keepdims=True))
            - jnp.exp(jnp.sum(lp[2:3] * lp[3:4], -1, keepdims=True)) + lambda_init)


def _diff_pipe_kernel(lam_ref, g_ref, q_ref, k_ref, v_ref, kc_ref, vc_ref, o_ref,
                      el0, el1, ec0, ec1, cf0, cf1, *, lambda_init):
    i = pl.program_id(2)
    tq = q_ref.shape[0]

    @pl.when(i == 0)
    def _():
        for ref in (el0, el1, ec0, ec1, cf0, cf1):
            ref[...] = jnp.zeros_like(ref)

    def step(e_lat, e_ctx, coef, p_lat, p_ctx, p_coef):
        lam = _diff_lambda(lam_ref, lambda_init)

        def numerators(half):
            sl = slice(half * HEAD_DIM, (half + 1) * HEAD_DIM)
            q = q_ref[:, sl]
            (el, ec), inv = _exp2_parts([_nt_dot(q, k_ref[:, sl]), _nt_dot(q, kc_ref[:, sl].astype(BF))])
            e_lat[half] = el
            e_ctx[half] = ec
            coef[half] = jnp.broadcast_to(inv * lam if half else inv, (tq, HEAD_DIM))

        numerators(0)
        c0 = p_coef[0][:, 0:1]
        c1 = p_coef[1][:, 0:1]
        a = (p_lat[0] * c0 - p_lat[1] * c1).astype(BF)
        ac = (p_ctx[0] * c0 - p_ctx[1] * c1).astype(BF)
        o = _dot(a, v_ref[...]) + _dot(ac, vc_ref[...].astype(BF))
        of = o * lax.rsqrt(jnp.mean(o * o, -1, keepdims=True) + RMS_EPS)
        o_ref[...] = ((of * g_ref[...]) * (1.0 - lambda_init)).astype(o_ref.dtype)
        numerators(1)

    @pl.when(i % 2 == 0)
    def _():
        step(el0, ec0, cf0, el1, ec1, cf1)

    @pl.when(i % 2 == 1)
    def _():
        step(el1, ec1, cf1, el0, ec0, cf0)


def diff_attention_pipelined(q, k, v, lam_p, subln, lambda_init, N, T, ctx, widx):
    M = N * T
    tq = 256
    nq = T // tq
    W = C_V_DIM
    L = ctx[0].shape[2]
    return pl.pallas_call(
        functools.partial(_diff_pipe_kernel, lambda_init=lambda_init),
        grid=(N, C_HEADS, nq + 1),
        in_specs=[
            pl.BlockSpec((None, 4, HEAD_DIM), lambda n, h, i: (widx, 0, 0)),
            pl.BlockSpec((None, 1, W), lambda n, h, i: (widx, 0, 0)),
            pl.BlockSpec((tq, W), lambda n, h, i: (n * nq + jnp.minimum(i, nq - 1), h)),
            pl.BlockSpec((T, W), lambda n, h, i: (n, h)),
            pl.BlockSpec((T, W), lambda n, h, i: (n, h)),
            pl.BlockSpec((None, None, L, W), lambda n, h, i: (n, widx, 0, h)),
            pl.BlockSpec((None, None, L, W), lambda n, h, i: (n, widx, 0, h)),
        ],
        out_specs=pl.BlockSpec((tq, W), lambda n, h, i: (n * nq + jnp.maximum(i - 1, 0), h)),
        out_shape=jax.ShapeDtypeStruct((M, C_HEADS * W), BF),
        scratch_shapes=[pltpu.VMEM((2, tq, T), F32)] * 2 + [pltpu.VMEM((2, tq, L), F32)] * 2
        + [pltpu.VMEM((2, tq, HEAD_DIM), F32)] * 2,
        compiler_params=_cparams("parallel", "parallel", "arbitrary"),
    )(lam_p, subln.reshape(subln.shape[0], 1, W), q, k, v, *ctx)


def diff_attention(q, k, v, lam_p, subln, lambda_init, N, T, ctx=None, widx=0, q_prescaled=False):
    M = N * T
    tq = min(T, 256)
    nq = T // tq
    W = C_V_DIM
    in_specs = [
        pl.BlockSpec((None, 4, HEAD_DIM), lambda n, h, i: (widx, 0, 0)),
        pl.BlockSpec((None, 1, W), lambda n, h, i: (widx, 0, 0)),
        pl.BlockSpec((tq, W), lambda n, h, i: (n * nq + i, h)),
        pl.BlockSpec((T, W), lambda n, h, i: (n, h)),
        pl.BlockSpec((T, W), lambda n, h, i: (n, h)),
    ]
    args = [lam_p, subln.reshape(subln.shape[0], 1, W), q, k, v]
    if ctx is not None:
        L = ctx[0].shape[2]
        in_specs += [pl.BlockSpec((None, None, L, W), lambda n, h, i: (n, widx, 0, h))] * 2
        args += list(ctx)
    return pl.pallas_call(
        functools.partial(_diff_kernel, has_ctx=ctx is not None, lambda_init=lambda_init, q_prescaled=q_prescaled),
        grid=(N, C_HEADS, nq),
        in_specs=in_specs,
        out_specs=pl.BlockSpec((tq, W), lambda n, h, i: (n * nq + i, h)),
        out_shape=jax.ShapeDtypeStruct((M, C_HEADS * W), BF),
        compiler_params=_cparams("parallel", "parallel", "arbitrary"),
    )(*args)


def _outproj_kernel(*refs, n_in, alpha):
    o_refs = refs[:n_in]
    w_refs = refs[n_in:2 * n_in]
    x_ref, m_ref, g_ref, b_ref, wr_ref, xo_ref, h_ref, aff_ref = refs[2 * n_in:]
    y = None
    for o_r, w_r in zip(o_refs, w_refs):
        t = _dot(o_r[...], w_r[...])
        y = t if y is None else y + t
    z = alpha * x_ref[...] + m_ref[2:3, :] * y
    xn = _layer_norm(z, g_ref[...], b_ref[...])
    xo_ref[...] = xn
    hb = (xn * (1.0 + m_ref[4:5, :]) + m_ref[3:4, :]).astype(BF)
    h_ref[...] = hb
    logits = _dot(hb, wr_ref[...])
    mx = jnp.max(logits, -1, keepdims=True)
    ex = jnp.exp(logits - mx)
    aff_ref[...] = ex / jnp.sum(ex, -1, keepdims=True)


def outproj_ln_router(os_, w_out, widx, x, mods, layer, row0, per_req, T, ln_g, ln_b, w_router, alpha):
    M, D = x.shape
    tm = min(T, 256)
    E = w_router.shape[-1]
    in_specs, args, koff = [], [], 0
    for o in os_:
        in_specs.append(pl.BlockSpec((tm, o.shape[1]), lambda i: (i, 0)))
        args.append(o)
    for o in os_:
        wk = o.shape[1]
        in_specs.append(pl.BlockSpec((None, wk, D), functools.partial(lambda i, kb: (widx, kb, 0), kb=koff // wk)))
        args.append(w_out)
        koff += wk
    in_specs += [
        pl.BlockSpec((tm, D), lambda i: (i, 0)),
        pl.BlockSpec((None, None, 6, D), lambda i: (layer, row0 + ((i * tm) // T) * per_req, 0, 0)),
        pl.BlockSpec((None, 1, D), lambda i: (layer, 0, 0)),
        pl.BlockSpec((None, 1, D), lambda i: (layer, 0, 0)),
        pl.BlockSpec((None, D, E), lambda i: (layer, 0, 0)),
    ]
    args += [x, mods, ln_g, ln_b, w_router]
    return pl.pallas_call(
        functools.partial(_outproj_kernel, n_in=len(os_), alpha=alpha),
        grid=(M // tm,),
        in_specs=in_specs,
        out_specs=[
            pl.BlockSpec((tm, D), lambda i: (i, 0)),
            pl.BlockSpec((tm, D), lambda i: (i, 0)),
            pl.BlockSpec((tm, E), lambda i: (i, 0)),
        ],
        out_shape=[
            jax.ShapeDtypeStruct((M, D), F32),
            jax.ShapeDtypeStruct((M, D), BF),
            jax.ShapeDtypeStruct((M, E), F32),
        ],
        compiler_params=_cparams("parallel"),
    )(*args)


def _route_kernel(aff_ref, scol_ref, srow_ref, cnt_ref, a3_ref, *, cap, nblk):
    E = N_EXPERTS
    B = aff_ref.shape[0] // nblk
    a = aff_ref[...]
    bits = pltpu.bitcast(a, jnp.int32)

    def search(it, ans):
        cand = ans | jnp.left_shift(jnp.int32(1), 30 - it)
        cnt = jnp.sum((bits >= cand).astype(F32), axis=0, keepdims=True)
        return jnp.where(cnt >= cap, cand, ans)

    thr = lax.fori_loop(0, 31, search, jnp.zeros((1, E), jnp.int32))
    gt = bits > thr
    eq = bits == thr
    need = cap - jnp.sum(gt.astype(F32), axis=0, keepdims=True)

    r = lax.broadcasted_iota(jnp.int32, (B, B), 0)
    c = lax.broadcasted_iota(jnp.int32, (B, B), 1)
    lower = (c < r).astype(BF)
    upper = (r < c).astype(BF)
    eye = (lax.broadcasted_iota(jnp.int32, (E, E), 0) == lax.broadcasted_iota(jnp.int32, (E, E), 1)).astype(BF)

    carry = jnp.zeros((1, E), F32)
    sels = []
    for b in range(nblk):
        sl = slice(b * B, (b + 1) * B)
        eqb = eq[sl].astype(BF)
        pref = _dot(lower, eqb) + carry
        carry = carry + jnp.sum(eqb.astype(F32), axis=0, keepdims=True)
        sels.append(jnp.logical_or(gt[sl], jnp.logical_and(eq[sl], pref < need)))

    carry = jnp.zeros((1, E), F32)
    carry_t = jnp.zeros((E, 1), F32)
    for b in range(nblk):
        sl = slice(b * B, (b + 1) * B)
        selb = sels[b].astype(BF)
        cnt_ref[b:b + 1, :] = carry.astype(jnp.int32)
        pos = _dot(lower, selb) + carry
        scol_ref[sl, :] = jnp.where(sels[b], pos, -1.0)
        carry = carry + jnp.sum(selb.astype(F32), axis=0, keepdims=True)
        sel_t = _nt_dot(eye, selb)
        pos_t = _dot(sel_t.astype(BF), upper) + carry_t
        srow_ref[b] = jnp.where(sel_t > 0.5, pos_t, -1.0)
        carry_t = carry_t + jnp.sum(sel_t, axis=1, keepdims=True)
    cnt_ref[nblk:nblk + 1, :] = carry.astype(jnp.int32)

    hi = a.astype(BF)
    r1 = a - hi.astype(F32)
    mid = r1.astype(BF)
    lo = (r1 - mid.astype(F32)).astype(BF)
    lane = lax.broadcasted_iota(jnp.int32, (E, HEAD_DIM), 1)
    row = lax.broadcasted_iota(jnp.int32, (E, HEAD_DIM), 0)
    a3 = (_dot(hi, (lane == row).astype(BF)) + _dot(mid, (lane == row + E).astype(BF))
          + _dot(lo, (lane == row + 2 * E).astype(BF)))
    a3_ref[...] = a3.astype(BF)


def _gate_from_split(g3, expert):
    lane = lax.broadcasted_iota(jnp.int32, g3.shape, 1)
    pick = jnp.logical_and(lane % N_EXPERTS == expert, lane < 3 * N_EXPERTS)
    return jnp.sum(jnp.where(pick, g3, 0.0), axis=1, keepdims=True)


def route(aff, N, T):
    E = N_EXPERTS
    cap = EC_CAPACITY_FACTOR * T // E
    B = min(T, TOKEN_BLOCK)
    nblk = T // B
    return pl.pallas_call(
        functools.partial(_route_kernel, cap=cap, nblk=nblk),
        grid=(N,),
        in_specs=[pl.BlockSpec((T, E), lambda n: (n, 0))],
        out_specs=[
            pl.BlockSpec((None, T, E), lambda n: (n, 0, 0)),
            pl.BlockSpec((None, nblk, E, B), lambda n: (n, 0, 0, 0)),
            pl.BlockSpec((None, nblk + 1, E), lambda n: (n, 0, 0)),
            pl.BlockSpec((None, T, HEAD_DIM), lambda n: (n, 0, 0)),
        ],
        out_shape=[
            jax.ShapeDtypeStruct((N, T, E), F32),
            jax.ShapeDtypeStruct((N, nblk, E, B), F32),
            jax.ShapeDtypeStruct((N, nblk + 1, E), jnp.int32),
            jax.ShapeDtypeStruct((N, T, HEAD_DIM), BF),
        ],
        compiler_params=_cparams("parallel"),
    )(aff)


def _gather_small_kernel(h_ref, srow_ref, a3_ref, x_all_ref, g_all_ref, x_ref, g_ref, *, cap):
    del x_all_ref, g_all_ref
    E = N_EXPERTS
    S = E * cap
    T = h_ref.shape[0]
    sr = srow_ref[...].astype(BF)
    rep_t = (lax.broadcasted_iota(jnp.int32, (S, E), 0) // cap == lax.broadcasted_iota(jnp.int32, (S, E), 1)).astype(BF)
    slot_of_row = (lax.broadcasted_iota(jnp.int32, (S, 1), 0) % cap).astype(F32)
    onehot = (_dot(rep_t, sr) == slot_of_row).astype(BF)
    x = _dot(onehot, h_ref[...])
    for e in range(E):
        x_ref[e] = x[e * cap:(e + 1) * cap].astype(x_ref.dtype)
    gate = _gate_from_split(_dot(onehot, a3_ref[...]), lax.broadcasted_iota(jnp.int32, (S, 1), 0) // cap)
    gb = jnp.broadcast_to(gate, (S, HEAD_DIM))
    for e in range(E):
        g_ref[e] = gb[e * cap:(e + 1) * cap]


def gather_small(h, srow, a3, x_all, g_all, row0, N, T):
    E = N_EXPERTS
    cap = EC_CAPACITY_FACTOR * T // E
    D = h.shape[1]
    assert row0 % cap == 0
    rb0 = row0 // cap
    return pl.pallas_call(
        functools.partial(_gather_small_kernel, cap=cap),
        grid=(N,),
        in_specs=[
            pl.BlockSpec((T, D), lambda n: (n, 0)),
            pl.BlockSpec((None, None, E, T), lambda n: (n, 0, 0, 0)),
            pl.BlockSpec((None, T, HEAD_DIM), lambda n: (n, 0, 0)),
            pl.BlockSpec(memory_space=pl.ANY),
            pl.BlockSpec(memory_space=pl.ANY),
        ],
        out_specs=[
            pl.BlockSpec((E, cap, D), lambda n: (0, rb0 + n, 0)),
            pl.BlockSpec((E, cap, HEAD_DIM), lambda n: (0, rb0 + n, 0)),
        ],
        out_shape=[
            jax.ShapeDtypeStruct(x_all.shape, x_all.dtype),
            jax.ShapeDtypeStruct(g_all.shape, g_all.dtype),
        ],
        input_output_aliases={3: 0, 4: 1},
        compiler_params=_cparams("parallel"),
    )(h, srow, a3, x_all, g_all)


GATHER_COLS = 512


def _gather_big_kernel(cnt_ref, h_ref, srow_ref, a3_ref, x_ref, g_ref, *, cap, nblk):
    E = N_EXPERTS
    W = min(SLOT_CHUNK, cap)
    n = pl.program_id(0)
    d = pl.program_id(1)
    b = pl.program_id(2)

    @pl.when(b == 0)
    def _():
        x_ref[...] = jnp.zeros_like(x_ref)

    @pl.when(jnp.logical_and(b == 0, d == 0))
    def _():
        g_ref[...] = jnp.zeros_like(g_ref)

    row = lax.broadcasted_iota(jnp.int32, (W, 1), 0)
    starts, ends, pieces = [], [], []
    for e in range(E):
        base = (n * E + e) * (nblk + 1) + b
        ws = pl.multiple_of(jnp.minimum((cnt_ref[base] // 16) * 16, cap - W), 16)
        pieces.append((srow_ref[e:e + 1, :] == (ws + row).astype(F32)).astype(BF))
        starts.append(ws)
        ends.append(cnt_ref[base + 1])
    onehot = jnp.concatenate(pieces, axis=0)
    prod = _dot(onehot, h_ref[...])
    for e in range(E):
        x_ref[e, pl.ds(starts[e], W), :] += prod[e * W:(e + 1) * W].astype(x_ref.dtype)

    @pl.when(d == 0)
    def _():
        split = _dot(onehot, a3_ref[...])
        for e in range(E):
            g_ref[e, pl.ds(starts[e], W), :] += split[e * W:(e + 1) * W]

    for e in range(E):
        def more(k, carry, e=e):
            first = starts[e] + k * W
            ws = pl.multiple_of(jnp.minimum(first, cap - W), 16)
            ids = ws + row
            piece = jnp.logical_and(srow_ref[e:e + 1, :] == ids.astype(F32), ids >= first).astype(BF)
            x_ref[e, pl.ds(ws, W), :] += _dot(piece, h_ref[...]).astype(x_ref.dtype)

            @pl.when(d == 0)
            def _():
                g_ref[e, pl.ds(ws, W), :] += _dot(piece, a3_ref[...])

            return carry

        lax.fori_loop(1, (ends[e] - starts[e] + W - 1) // W, more, 0)

    @pl.when(jnp.logical_and(b == nblk - 1, d == 0))
    def _():
        for e in range(E):
            g_ref[e] = jnp.broadcast_to(_gate_from_split(g_ref[e], e), (cap, HEAD_DIM))


def gather_big(h, srow, cnt_flat, a3, rows_total, N, T):
    E = N_EXPERTS
    cap = EC_CAPACITY_FACTOR * T // E
    D = h.shape[1]
    nblk = T // TOKEN_BLOCK
    Dc = min(D, GATHER_COLS)
    assert cap % 16 == 0
    return pl.pallas_call(
        functools.partial(_gather_big_kernel, cap=cap, nblk=nblk),
        grid_spec=pltpu.PrefetchScalarGridSpec(
            num_scalar_prefetch=1,
            grid=(N, D // Dc, nblk),
            in_specs=[
                pl.BlockSpec((TOKEN_BLOCK, Dc), lambda n, d, b, c: (n * nblk + b, d)),
                pl.BlockSpec((None, None, E, TOKEN_BLOCK), lambda n, d, b, c: (n, b, 0, 0)),
                pl.BlockSpec((None, TOKEN_BLOCK, HEAD_DIM), lambda n, d, b, c: (n, b, 0)),
            ],
            out_specs=[
                pl.BlockSpec((E, cap, Dc), lambda n, d, b, c: (0, n, d)),
                pl.BlockSpec((E, cap, HEAD_DIM), lambda n, d, b, c: (0, n, 0)),
            ],
        ),
        out_shape=[
            jax.ShapeDtypeStruct((E, rows_total, D), BF),
            jax.ShapeDtypeStruct((E, rows_total, HEAD_DIM), F32),
        ],
        compiler_params=_cparams("parallel", "arbitrary", "arbitrary"),
    )(cnt_flat, h, srow, a3)


def _gate_up_kernel(x_ref, wg_ref, wu_ref, o_ref):
    x = x_ref[...]
    a = _dot(x, wg_ref[...].astype(BF))
    u = _dot(x, wu_ref[...].astype(BF))
    o_ref[...] = ((a * jax.nn.sigmoid(a)) * u).astype(o_ref.dtype)


def expert_gate_up(x, w_gate, w_up, layer):
    E, R, D = x.shape
    F = w_gate.shape[-1]
    tf = min(F, 256)
    return pl.pallas_call(
        _gate_up_kernel,
        grid=(E, F // tf),
        in_specs=[
            pl.BlockSpec((None, R, D), lambda e, f: (e, 0, 0)),
            pl.BlockSpec((None, None, D, tf), lambda e, f: (layer, e, 0, f)),
            pl.BlockSpec((None, None, D, tf), lambda e, f: (layer, e, 0, f)),
        ],
        out_specs=pl.BlockSpec((None, R, tf), lambda e, f: (e, 0, f)),
        out_shape=jax.ShapeDtypeStruct((E, R, F), BF),
        compiler_params=_cparams("parallel", "arbitrary"),
    )(x, w_gate, w_up)


def _down_kernel(h_ref, w_ref, g_ref, o_ref):
    o_ref[...] = (_dot(h_ref[...], w_ref[...].astype(BF)) * g_ref[:, 0:1]).astype(o_ref.dtype)


def expert_down(h, w_down, gate, layer):
    E, R, F = h.shape
    D = w_down.shape[-1]
    td = min(D, 512)
    return pl.pallas_call(
        _down_kernel,
        grid=(E, D // td),
        in_specs=[
            pl.BlockSpec((None, R, F), lambda e, d: (e, 0, 0)),
            pl.BlockSpec((None, None, F, td), lambda e, d: (layer, e, 0, d)),
            pl.BlockSpec((None, R, HEAD_DIM), lambda e, d: (e, 0, 0)),
        ],
        out_specs=pl.BlockSpec((None, R, td), lambda e, d: (e, 0, d)),
        out_shape=jax.ShapeDtypeStruct((E, R, D), BF),
        compiler_params=_cparams("parallel", "arbitrary"),
    )(h, w_down, gate)


def _residual_epilogue(f, x_ref, m_ref, mn_ref, g_ref, b_ref, xo_ref, h_ref, alpha):
    z = alpha * x_ref[...] + m_ref[5:6, :] * f
    xn = _layer_norm(z, g_ref[...], b_ref[...])
    xo_ref[...] = xn
    h_ref[...] = (xn * (1.0 + mn_ref[1:2, :]) + mn_ref[0:1, :]).astype(h_ref.dtype)


def _scatter_small_kernel(y_ref, scol_ref, x_ref, m_ref, mn_ref, g_ref, b_ref, xo_ref, h_ref, *, cap, alpha):
    E = N_EXPERTS
    S = E * cap
    sc = scol_ref[...].astype(BF)
    rep = (lax.broadcasted_iota(jnp.int32, (E, S), 0) == lax.broadcasted_iota(jnp.int32, (E, S), 1) // cap).astype(BF)
    slot_of_col = (lax.broadcasted_iota(jnp.int32, (1, S), 1) % cap).astype(F32)
    onehot_t = (_dot(sc, rep) == slot_of_col).astype(BF)
    y = jnp.concatenate([y_ref[e] for e in range(E)], axis=0)
    f = _dot(onehot_t, y)
    _residual_epilogue(f, x_ref, m_ref, mn_ref, g_ref, b_ref, xo_ref, h_ref, alpha)


def scatter_small(y, y_row0, scol, x, mods, layer, next_layer, row0, per_req, ln_g, ln_b, N, T, alpha):
    E = N_EXPERTS
    cap = EC_CAPACITY_FACTOR * T // E
    D = x.shape[1]
    assert y_row0 % cap == 0
    return pl.pallas_call(
        functools.partial(_scatter_small_kernel, cap=cap, alpha=alpha),
        grid=(N,),
        in_specs=[
            pl.BlockSpec((E, cap, D), lambda n: (0, y_row0 // cap + n, 0)),
            pl.BlockSpec((None, T, E), lambda n: (n, 0, 0)),
            pl.BlockSpec((T, D), lambda n: (n, 0)),
            pl.BlockSpec((None, None, 6, D), lambda n: (layer, row0 + n * per_req, 0, 0)),
            pl.BlockSpec((None, None, 6, D), lambda n: (next_layer, row0 + n * per_req, 0, 0)),
            pl.BlockSpec((None, 1, D), lambda n: (layer, 0, 0)),
            pl.BlockSpec((None, 1, D), lambda n: (layer, 0, 0)),
        ],
        out_specs=[pl.BlockSpec((T, D), lambda n: (n, 0)), pl.BlockSpec((T, D), lambda n: (n, 0))],
        out_shape=[jax.ShapeDtypeStruct((N * T, D), F32), jax.ShapeDtypeStruct((N * T, D), BF)],
        compiler_params=_cparams("parallel"),
    )(y, scol, x, mods, mods, ln_g, ln_b)


def _scatter_big_kernel(cnt_ref, y_ref, scol_ref, x_ref, m_ref, mn_ref, g_ref, b_ref, xo_ref, h_ref,
                        *, nblk, tiles, alpha):
    acc = xo_ref
    E = N_EXPERTS
    C = SLOT_CHUNK
    B = TOKEN_BLOCK
    G = y_ref.shape[0]
    cap = y_ref.shape[1]
    WIN = min(C, cap)
    n = pl.program_id(0)
    tb = pl.program_id(1)
    g = pl.program_id(2)

    @pl.when(g == 0)
    def _():
        acc[...] = jnp.zeros_like(acc)

    lane = lax.broadcasted_iota(jnp.int32, (B, E), 1)
    for bl in range(tiles):
        blk = tb * tiles + bl
        sc = scol_ref[bl * B:(bl + 1) * B, :]
        hits, wins = [], []
        for gi in range(G):
            e = g * G + gi
            base = (n * E + e) * (nblk + 1)
            lo = cnt_ref[base + blk]
            hi = cnt_ref[base + blk + 1]
            scol = jnp.sum(jnp.where(lane == e, sc, 0.0), axis=1, keepdims=True)
            ws = pl.multiple_of(jnp.minimum((lo // 16) * 16, cap - WIN), 16)
            col = lax.broadcasted_iota(jnp.int32, (1, WIN), 1)
            hits.append((scol == (ws + col).astype(F32)).astype(BF))
            wins.append(y_ref[gi, pl.ds(ws, WIN), :])

            def more(k, carry, gi=gi, scol=scol, ws=ws, col=col):
                first = ws + k * WIN
                ks = pl.multiple_of(jnp.minimum(first, cap - WIN), 16)
                ids = ks + col
                hit = jnp.logical_and(scol == ids.astype(F32), ids >= first)
                acc[bl * B:(bl + 1) * B, :] += _dot(hit.astype(BF), y_ref[gi, pl.ds(ks, WIN), :])
                return carry

            lax.fori_loop(1, (hi - ws + WIN - 1) // WIN, more, 0)
        f = None
        for p in range(0, G, 2):
            d = _dot(jnp.concatenate(hits[p:p + 2], axis=1), jnp.concatenate(wins[p:p + 2], axis=0))
            f = d if f is None else f + d
        acc[bl * B:(bl + 1) * B, :] += f

    @pl.when(g == pl.num_programs(2) - 1)
    def _():
        for r in range(0, tiles * B, B):
            rows = slice(r, r + B)
            z = alpha * x_ref[rows, :] + m_ref[5:6, :] * acc[rows, :]
            xn = _layer_norm(z, g_ref[...], b_ref[...])
            xo_ref[rows, :] = xn
            h_ref[rows, :] = (xn * (1.0 + mn_ref[1:2, :]) + mn_ref[0:1, :]).astype(h_ref.dtype)


SCATTER_EXPERT_GROUP = 2


def scatter_big(y, y_row0, scol, cnt_flat, x, mods, layer, next_layer, row0, per_req, ln_g, ln_b, N, T, alpha):
    E = N_EXPERTS
    cap = EC_CAPACITY_FACTOR * T // E
    D = x.shape[1]
    nblk = T // TOKEN_BLOCK
    tiles = max(t for t in (1, 2, 4) if nblk % t == 0)
    tt = tiles * TOKEN_BLOCK
    ntt = T // tt
    G = SCATTER_EXPERT_GROUP
    assert y_row0 % cap == 0 and cap % 16 == 0
    return pl.pallas_call(
        functools.partial(_scatter_big_kernel, nblk=nblk, tiles=tiles, alpha=alpha),
        grid_spec=pltpu.PrefetchScalarGridSpec(
            num_scalar_prefetch=1,
            grid=(N, ntt, E // G),
            in_specs=[
                pl.BlockSpec((G, cap, D), lambda n, t, e, c: (e, y_row0 // cap + n, 0)),
                pl.BlockSpec((None, tt, E), lambda n, t, e, c: (n, t, 0)),
                pl.BlockSpec((tt, D), lambda n, t, e, c: (n * ntt + t, 0), pipeline_mode=pl.Buffered(1)),
                pl.BlockSpec((None, None, 6, D), lambda n, t, e, c: (layer, row0 + n * per_req, 0, 0)),
                pl.BlockSpec((None, None, 6, D), lambda n, t, e, c: (next_layer, row0 + n * per_req, 0, 0)),
                pl.BlockSpec((None, 1, D), lambda n, t, e, c: (layer, 0, 0)),
                pl.BlockSpec((None, 1, D), lambda n, t, e, c: (layer, 0, 0)),
            ],
            out_specs=[
                pl.BlockSpec((tt, D), lambda n, t, e, c: (n * ntt + t, 0)),
                pl.BlockSpec((tt, D), lambda n, t, e, c: (n * ntt + t, 0)),
            ],
        ),
        out_shape=[jax.ShapeDtypeStruct((N * T, D), F32), jax.ShapeDtypeStruct((N * T, D), BF)],
        compiler_params=_cparams("parallel", "parallel", "arbitrary"),
    )(cnt_flat, y, scol, x, mods, mods, ln_g, ln_b)


def moe_layer(groups, mods, layer, next_layer, ln_g, ln_b, w_gate, w_up, w_down, alpha):
    E = N_EXPERTS
    rows_total = sum(gr["N"] * (EC_CAPACITY_FACTOR * gr["T"] // E) for gr in groups)
    routed, row0s, rows, x_all, g_all = [], [], 0, None, None
    for gr in groups:
        N, T = gr["N"], gr["T"]
        scol, srow, cnt, a3 = route(gr["aff"], N, T)
        if T <= TOKEN_BLOCK:
            assert x_all is not None
            cnt_flat = None
            x_all, g_all = gather_small(gr["h"], srow, a3, x_all, g_all, rows, N, T)
        else:
            assert x_all is None
            cnt_flat = jnp.transpose(cnt, (0, 2, 1)).reshape(-1)
            x_all, g_all = gather_big(gr["h"], srow, cnt_flat, a3, rows_total, N, T)
        routed.append((scol, cnt_flat))
        row0s.append(rows)
        rows += N * (EC_CAPACITY_FACTOR * T // E)
    hm = expert_gate_up(x_all, w_gate, w_up, layer)
    y = expert_down(hm, w_down, g_all, layer)
    out = []
    for gr, (scol, cnt_flat), y_row0 in zip(groups, routed, row0s):
        tail = (gr["x"], mods, layer, next_layer, gr["row0"], gr["per_req"], ln_g, ln_b, gr["N"], gr["T"], alpha)
        if cnt_flat is None:
            out.append(scatter_small(y, y_row0, scol, *tail))
        else:
            out.append(scatter_big(y, y_row0, scol, cnt_flat, *tail))
    return out


def kernel(x_prompt, x_sample, cache_a_k, cache_a_v, cache_b_k, cache_b_v, cache_c_k, cache_c_v, c, c_ctx, w_mod, b_mod, w_in_even, w_out_even, a_sink, na_rpb, w_in_odd, w_out_odd, diff_lambda, diff_subln, ln_g, ln_b, w_router, w_gate, w_up, w_down):
    NP, TP, D = x_prompt.shape
    NS, TS, _ = x_sample.shape
    depth = w_mod.shape[0]
    L = cache_a_k.shape[2]
    alpha = (2 * depth) ** 0.25
    n_even = (depth + 1) // 2
    n_odd = depth // 2

    rows = 1 + NS
    rpad = -rows % 8
    cond = jnp.concatenate([c_ctx[None], c, jnp.zeros((rpad, D), F32)], axis=0)
    mods = adaln_all(cond, w_mod, b_mod).reshape(depth, rows + rpad, 6, D)

    rope = rope_tables(TS)
    sinks = a_sink.reshape(-1)
    rpb_tiles = toeplitz_bias(na_rpb)
    w_out_even_bf = w_out_even.astype(BF)
    w_out_odd_bf = w_out_odd.astype(BF)
    w_router_bf = w_router.astype(BF)
    ln_g4 = ln_g.reshape(depth, 2, 1, D)
    ln_b4 = ln_b.reshape(depth, 2, 1, D)
    ca_k = cache_a_k.reshape(NS, n_even, L, A_KV_HEADS * HEAD_DIM)
    ca_v = cache_a_v.reshape(NS, n_even, L, A_KV_HEADS * HEAD_DIM)
    cb_k = cache_b_k.reshape(NS, n_even, L, B_WIDTH)
    cb_v = cache_b_v.reshape(NS, n_even, L, B_WIDTH)
    cc_k = cache_c_k.reshape(NS, n_odd, L, C_HEADS * 2 * HEAD_DIM)
    cc_v = cache_c_v.reshape(NS, n_odd, L, C_HEADS * C_V_DIM)

    xp = x_prompt.reshape(NP * TP, D)
    xs = x_sample.reshape(NS * TS, D)
    hp = modulate_rows(xp, mods, 0, 0, 0, TP)
    hs = modulate_rows(xs, mods, 0, 1, 1, TS)

    new_a_k, new_a_v, new_b_k, new_b_v, new_c_k, new_c_v = [], [], [], [], [], []
    QA = A_Q_HEADS * HEAD_DIM
    KA = A_KV_HEADS * HEAD_DIM
    for l in range(depth):
        if l % 2 == 0:
            e = l // 2
            b0 = A_WIDTH
            pa = project(hp, w_in_even, e, 0, A_WIDTH, 0, None, TP, F32)
            pqb = project(hp, w_in_even, e, b0, B_WIDTH, 0, None, TP, F32)
            pkb = project(hp, w_in_even, e, b0 + B_WIDTH, B_WIDTH, 0, None, TP, F32)
            pvb = project(hp, w_in_even, e, b0 + 2 * B_WIDTH, B_WIDTH, 0, None, TP, F32)
            new_a_k.append(pa[:, QA:QA + KA].reshape(NP, TP, A_KV_HEADS, HEAD_DIM))
            new_a_v.append(pa[:, QA + KA:QA + 2 * KA].reshape(NP, TP, A_KV_HEADS, HEAD_DIM))
            new_b_k.append(pkb.reshape(NP, TP, B_HEADS, HEAD_DIM))
            new_b_v.append(pvb.reshape(NP, TP, B_HEADS, HEAD_DIM))
            op = [ctx_even_attention(pa, pqb, pkb, pvb, sinks, e, NP, TP)]

            sa = project(hs, w_in_even, e, 0, A_WIDTH, A_Q_HEADS + A_KV_HEADS, rope, TS, BF, A_Q_HEADS)
            sqb = project(hs, w_in_even, e, b0, B_WIDTH, 0, rope, TS, BF, B_HEADS)
            skb = project(hs, w_in_even, e, b0 + B_WIDTH, B_WIDTH, 0, rope, TS, BF)
            svb = project(hs, w_in_even, e, b0 + 2 * B_WIDTH, B_WIDTH, 0, rope, TS, BF)
            oa = window_attention(sa, ca_k, ca_v, sinks, e, NS, TS)
            bias3 = neighbourhood_bias(rpb_tiles[e], TS)
            ob = neighbourhood_attention(sqb, skb, svb, cb_k, cb_v, bias3, e, NS, TS)
            os_ = [oa, ob]
            w_out, widx = w_out_even_bf, e
        else:
            o = l // 2
            lambda_init = 0.8 - 0.6 * math.exp(-0.3 * l)
            W = C_HEADS * C_V_DIM
            pq = project(hp, w_in_odd, o, 0, W, 0, None, TP, F32)
            pk = project(hp, w_in_odd, o, W, W, 0, None, TP, F32)
            pv = project(hp, w_in_odd, o, 2 * W, W, 0, None, TP, F32)
            new_c_k.append(pk.reshape(NP, TP, C_HEADS, 2, HEAD_DIM))
            new_c_v.append(pv.reshape(NP, TP, C_HEADS, C_V_DIM))
            op = [diff_attention(pq, pk, pv, diff_lambda, diff_subln, lambda_init, NP, TP, None, o)]

            sq = project(hs, w_in_odd, o, 0, W, 2 * C_HEADS, rope, TS, BF, 2 * C_HEADS)
            sk = project(hs, w_in_odd, o, W, W, 2 * C_HEADS, rope, TS, BF)
            sv = project(hs, w_in_odd, o, 2 * W, W, 0, rope, TS, BF)
            os_ = [diff_attention_pipelined(sq, sk, sv, diff_lambda, diff_subln, lambda_init, NS, TS, (cc_k, cc_v), o)]
            w_out, widx = w_out_odd_bf, o

        lg1, lb1 = ln_g4[:, 0], ln_b4[:, 0]
        lg2, lb2 = ln_g4[:, 1], ln_b4[:, 1]
        xp, hp, affp = outproj_ln_router(op, w_out, widx, xp, mods, l, 0, 0, TP, lg1, lb1, w_router_bf, alpha)
        xs, hs, affs = outproj_ln_router(os_, w_out, widx, xs, mods, l, 1, 1, TS, lg1, lb1, w_router_bf, alpha)
        nl = min(l + 1, depth - 1)
        groups = [dict(aff=affs, h=hs, x=xs, N=NS, T=TS, row0=1, per_req=1),
                  dict(aff=affp, h=hp, x=xp, N=NP, T=TP, row0=0, per_req=0)]
        (xs, hs), (xp, hp) = moe_layer(groups, mods, l, nl, lg2, lb2, w_gate, w_up, w_down, alpha)

    return (xp.reshape(NP, TP, D), xs.reshape(NS, TS, D),
            jnp.stack(new_a_k, axis=1), jnp.stack(new_a_v, axis=1),
            jnp.stack(new_b_k, axis=1), jnp.stack(new_b_v, axis=1),
            jnp.stack(new_c_k, axis=1), jnp.stack(new_c_v, axis=1))
```

```python
import functools
import math

import jax
import jax.numpy as jnp
from jax import lax
from jax.experimental import pallas as pl
from jax.experimental.pallas import tpu as pltpu

BF = jnp.bfloat16
F32 = jnp.float32

GRID_W = 64
HEAD_DIM = 128
A_Q_HEADS = 8
A_KV_HEADS = 2
A_GROUP = A_Q_HEADS // A_KV_HEADS
A_BLOCK = 128
B_HEADS = 8
NA_ROWS = 8
NA_COLS = 16
C_HEADS = 8
C_V_DIM = 2 * HEAD_DIM
N_EXPERTS = 16
EC_CAPACITY_FACTOR = 2
ROPE_BASE = 10000.0
LN_EPS = 1e-5
RMS_EPS = 1e-5
ATTN_SCALE = HEAD_DIM ** -0.5
A_WIDTH = (A_Q_HEADS + 2 * A_KV_HEADS) * HEAD_DIM
B_WIDTH = B_HEADS * HEAD_DIM

VMEM_LIMIT_BYTES = 52 * 1024 * 1024
TOKEN_BLOCK = 256
SLOT_CHUNK = 128


def _cparams(*sem):
    return pltpu.CompilerParams(dimension_semantics=sem, vmem_limit_bytes=VMEM_LIMIT_BYTES)


def _nt_dot(a, b):
    return lax.dot_general(a, b, (((1,), (1,)), ((), ())), preferred_element_type=F32)


def _dot(a, b):
    return jnp.dot(a, b, preferred_element_type=F32)


def _layer_norm(z, g, b):
    mu = jnp.mean(z, -1, keepdims=True)
    d = z - mu
    var = jnp.mean(d * d, -1, keepdims=True)
    return d * lax.rsqrt(var + LN_EPS) * g + b


def _adaln_kernel(c_ref, w_ref, b_ref, o_ref):
    c = c_ref[...]
    s = (c * jax.nn.sigmoid(c)).astype(BF)
    o_ref[...] = _dot(s, w_ref[...].astype(BF)) + b_ref[...]


def adaln_all(cond, w_mod, b_mod):
    L, D, D6 = w_mod.shape
    R = cond.shape[0]
    tn = D // 2
    return pl.pallas_call(
        _adaln_kernel,
        grid=(L, D6 // tn),
        in_specs=[
            pl.BlockSpec((R, D), lambda l, j: (0, 0)),
            pl.BlockSpec((None, D, tn), lambda l, j: (l, 0, j)),
            pl.BlockSpec((None, 1, tn), lambda l, j: (l, 0, j)),
        ],
        out_specs=pl.BlockSpec((None, R, tn), lambda l, j: (l, 0, j)),
        out_shape=jax.ShapeDtypeStruct((L, R, D6), F32),
        compiler_params=_cparams("parallel", "parallel"),
    )(cond, w_mod, b_mod.reshape(L, 1, D6))


def _modulate_kernel(x_ref, m_ref, o_ref):
    o_ref[...] = (x_ref[...] * (1.0 + m_ref[1:2, :]) + m_ref[0:1, :]).astype(o_ref.dtype)


def modulate_rows(x, mods, layer, row0, per_req, T):
    M, D = x.shape
    tt = min(T, 512)
    return pl.pallas_call(
        _modulate_kernel,
        grid=(M // tt,),
        in_specs=[
            pl.BlockSpec((tt, D), lambda i: (i, 0)),
            pl.BlockSpec((None, None, 6, D), lambda i: (layer, row0 + ((i * tt) // T) * per_req, 0, 0)),
        ],
        out_specs=pl.BlockSpec((tt, D), lambda i: (i, 0)),
        out_shape=jax.ShapeDtypeStruct((M, D), BF),
        compiler_params=_cparams("parallel"),
    )(x, mods)


def _swap32(x):
    lane = lax.broadcasted_iota(jnp.int32, (1, HEAD_DIM), 1)
    return jnp.where((lane % 64) < 32, pltpu.roll(x, HEAD_DIM - 32, 1), pltpu.roll(x, 32, 1))


def _proj_kernel(*refs, n_rope_heads, n_scaled_heads, heads_per_tile):
    if n_rope_heads:
        h_ref, w_ref, cos_ref, sin_ref, o_ref, wbf = refs
    else:
        h_ref, w_ref, o_ref, wbf = refs
    j = pl.program_id(0)
    i = pl.program_id(1)

    @pl.when(i == 0)
    def _():
        wbf[...] = w_ref[...].astype(BF)

    acc = _dot(h_ref[...], wbf[...])
    bounds = sorted({0, min(n_rope_heads, n_scaled_heads), n_rope_heads, n_scaled_heads, 1 << 30})
    for hh in range(heads_per_tile):
        sl = slice(hh * HEAD_DIM, (hh + 1) * HEAD_DIM)
        x = acc[:, sl]
        head = j * heads_per_tile + hh
        for lo, hi in zip(bounds[:-1], bounds[1:]):
            rot = lo < n_rope_heads
            scale = Q_PRESCALE if lo < n_scaled_heads else None

            def emit(x=x, sl=sl, rot=rot, scale=scale):
                y = x * cos_ref[...] + _swap32(x) * sin_ref[...] if rot else x
                o_ref[:, sl] = (y if scale is None else y * scale).astype(o_ref.dtype)

            if len(bounds) == 2:
                emit()
            else:
                pl.when(jnp.logical_and(head >= lo, head < hi))(emit)


def project(h, w, widx, col0, width, n_rope_heads, rope, T, out_dtype, n_scaled_heads=0):
    M, D = h.shape
    heads = width // HEAD_DIM
    hpt = max(k for k in range(1, 9) if heads % k == 0 and col0 % (k * HEAD_DIM) == 0)
    tn = hpt * HEAD_DIM
    tm = min(T if n_rope_heads else M, 1024)
    tpb = max(T // tm, 1)
    in_specs = [
        pl.BlockSpec((tm, D), lambda j, i: (i, 0)),
        pl.BlockSpec((None, D, tn), lambda j, i: (widx, 0, col0 // tn + j)),
    ]
    args = [h, w]
    if n_rope_heads:
        in_specs += [pl.BlockSpec((tm, HEAD_DIM), lambda j, i: (i % tpb, 0))] * 2
        args += list(rope)
    return pl.pallas_call(
        functools.partial(_proj_kernel, n_rope_heads=n_rope_heads, n_scaled_heads=n_scaled_heads, heads_per_tile=hpt),
        grid=(width // tn, M // tm),
        in_specs=in_specs,
        out_specs=pl.BlockSpec((tm, tn), lambda j, i: (i, j)),
        out_shape=jax.ShapeDtypeStruct((M, width), out_dtype),
        scratch_shapes=[pltpu.VMEM((D, tn), BF)],
        compiler_params=_cparams("parallel", "arbitrary"),
    )(*args)


def rope_tables(T):
    t = jnp.arange(T)
    row = (t // GRID_W).astype(F32)
    col = (t % GRID_W).astype(F32)
    axis_dim = HEAD_DIM // 2
    inv_freq = ROPE_BASE ** (-jnp.arange(0, axis_dim, 2, dtype=F32) / axis_dim)
    ar = row[:, None] * inv_freq
    ac = col[:, None] * inv_freq
    cos = jnp.concatenate([jnp.cos(ar), jnp.cos(ar), jnp.cos(ac), jnp.cos(ac)], -1)
    sin = jnp.concatenate([-jnp.sin(ar), jnp.sin(ar), -jnp.sin(ac), jnp.sin(ac)], -1)
    return cos, sin


LOG2E = math.log2(math.e)
Q_PRESCALE = ATTN_SCALE * LOG2E


def _exp2_parts(parts, sink=None):
    m = None
    for s in parts:
        mi = jnp.max(s, -1, keepdims=True)
        m = mi if m is None else jnp.maximum(m, mi)
    if sink is not None:
        m = jnp.maximum(m, sink)
    es = [jnp.exp2(s - m) for s in parts]
    den = None
    for e in es:
        di = jnp.sum(e, -1, keepdims=True)
        den = di if den is None else den + di
    if sink is not None:
        den = den + jnp.exp2(sink - m)
    return es, 1.0 / den


def _softmax_parts(parts, sink=None):
    m = None
    for s in parts:
        mi = jnp.max(s, -1, keepdims=True)
        m = mi if m is None else jnp.maximum(m, mi)
    if sink is not None:
        m = jnp.maximum(m, sink)
    es = [jnp.exp(s - m) for s in parts]
    den = None
    for e in es:
        di = jnp.sum(e, -1, keepdims=True)
        den = di if den is None else den + di
    if sink is not None:
        den = den + jnp.exp(sink - m)
    inv = 1.0 / den
    return [e * inv for e in es]


def _ctx_even_kernel(sink_ref, a_ref, qb_ref, kb_ref, vb_ref, o_ref, *, sink_base):
    T = a_ref.shape[0]
    for kv in range(A_KV_HEADS):
        k = a_ref[:, (A_Q_HEADS + kv) * HEAD_DIM:(A_Q_HEADS + kv + 1) * HEAD_DIM].astype(BF)
        v = a_ref[:, (A_Q_HEADS + A_KV_HEADS + kv) * HEAD_DIM:(A_Q_HEADS + A_KV_HEADS + kv + 1) * HEAD_DIM].astype(BF)
        for g in range(A_GROUP):
            hq = kv * A_GROUP + g
            q = a_ref[:, hq * HEAD_DIM:(hq + 1) * HEAD_DIM].astype(BF)
            s = _nt_dot(q, k) * ATTN_SCALE
            sink = jnp.full((T, 1), sink_ref[sink_base + hq], F32)
            (p,) = _softmax_parts([s], sink)
            o_ref[:, hq * HEAD_DIM:(hq + 1) * HEAD_DIM] = _dot(p.astype(BF), v).astype(o_ref.dtype)
    for hb in range(B_HEADS):
        sl = slice(hb * HEAD_DIM, (hb + 1) * HEAD_DIM)
        s = _nt_dot(qb_ref[:, sl].astype(BF), kb_ref[:, sl].astype(BF)) * ATTN_SCALE
        (p,) = _softmax_parts([s])
        o_ref[:, A_Q_HEADS * HEAD_DIM + hb * HEAD_DIM:A_Q_HEADS * HEAD_DIM + (hb + 1) * HEAD_DIM] = _dot(
            p.astype(BF), vb_ref[:, sl].astype(BF)).astype(o_ref.dtype)


def ctx_even_attention(arr_a, qb, kb, vb, sinks, e, N, T):
    M = N * T
    return pl.pallas_call(
        functools.partial(_ctx_even_kernel, sink_base=e * A_Q_HEADS),
        grid=(N,),
        in_specs=[
            pl.BlockSpec(memory_space=pltpu.SMEM),
            pl.BlockSpec((T, A_WIDTH), lambda n: (n, 0)),
            pl.BlockSpec((T, B_WIDTH), lambda n: (n, 0)),
            pl.BlockSpec((T, B_WIDTH), lambda n: (n, 0)),
            pl.BlockSpec((T, B_WIDTH), lambda n: (n, 0)),
        ],
        out_specs=pl.BlockSpec((T, A_Q_HEADS * HEAD_DIM + B_WIDTH), lambda n: (n, 0)),
        out_shape=jax.ShapeDtypeStruct((M, A_Q_HEADS * HEAD_DIM + B_WIDTH), BF),
        compiler_params=_cparams("parallel"),
    )(sinks, arr_a, qb, kb, vb)


def _win_kernel(sink_ref, q_ref, kvp_ref, kvc_ref, kvn_ref, kc_ref, vc_ref, o_ref, *, sink_base, nb):
    i = pl.program_id(1)
    R = A_GROUP * A_BLOCK
    qq = lax.broadcasted_iota(jnp.int32, (R, 1), 0) % A_BLOCK
    kk = lax.broadcasted_iota(jnp.int32, (1, 3 * A_BLOCK), 1)
    in_window = jnp.abs(kk - A_BLOCK - qq) <= A_BLOCK
    exists = jnp.logical_and(jnp.logical_or(kk >= A_BLOCK, i > 0), jnp.logical_or(kk < 2 * A_BLOCK, i < nb - 1))
    ok_band = jnp.logical_and(in_window, exists)
    for kv in range(A_KV_HEADS):
        ksl = slice(kv * HEAD_DIM, (kv + 1) * HEAD_DIM)
        vsl = slice((A_KV_HEADS + kv) * HEAD_DIM, (A_KV_HEADS + kv + 1) * HEAD_DIM)
        q = jnp.concatenate(
            [q_ref[:, (kv * A_GROUP + g) * HEAD_DIM:(kv * A_GROUP + g + 1) * HEAD_DIM] for g in range(A_GROUP)], axis=0)
        sink = jnp.concatenate(
            [jnp.full((A_BLOCK, 1), sink_ref[sink_base + kv * A_GROUP + g] * LOG2E, F32) for g in range(A_GROUP)],
            axis=0)
        k_band = jnp.concatenate([kvp_ref[:, ksl], kvc_ref[:, ksl], kvn_ref[:, ksl]], axis=0)
        v_band = jnp.concatenate([kvp_ref[:, vsl], kvc_ref[:, vsl], kvn_ref[:, vsl]], axis=0)
        s_b = jnp.where(ok_band, _nt_dot(q, k_band), -jnp.inf)
        s_x = _nt_dot(q, kc_ref[:, ksl].astype(BF))
        (e_b, e_x), inv = _exp2_parts([s_b, s_x], sink)
        o = (_dot(e_b.astype(BF), v_band) + _dot(e_x.astype(BF), vc_ref[:, ksl].astype(BF))) * inv
        for g in range(A_GROUP):
            hq = kv * A_GROUP + g
            o_ref[:, hq * HEAD_DIM:(hq + 1) * HEAD_DIM] = o[g * A_BLOCK:(g + 1) * A_BLOCK].astype(o_ref.dtype)


def window_attention(arr_a, cache_k, cache_v, sinks, e, N, T):
    M = N * T
    nb = T // A_BLOCK
    L = cache_k.shape[2]
    QW = A_Q_HEADS * HEAD_DIM
    KVW = 2 * A_KV_HEADS * HEAD_DIM
    kvblk = QW // KVW
    return pl.pallas_call(
        functools.partial(_win_kernel, sink_base=e * A_Q_HEADS, nb=nb),
        grid=(N, nb),
        in_specs=[
            pl.BlockSpec(memory_space=pltpu.SMEM),
            pl.BlockSpec((A_BLOCK, QW), lambda n, i: (n * nb + i, 0)),
            pl.BlockSpec((A_BLOCK, KVW), lambda n, i: (n * nb + jnp.maximum(i - 1, 0), kvblk)),
            pl.BlockSpec((A_BLOCK, KVW), lambda n, i: (n * nb + i, kvblk)),
            pl.BlockSpec((A_BLOCK, KVW), lambda n, i: (n * nb + jnp.minimum(i + 1, nb - 1), kvblk)),
            pl.BlockSpec((None, None, L, A_KV_HEADS * HEAD_DIM), lambda n, i: (n, e, 0, 0)),
            pl.BlockSpec((None, None, L, A_KV_HEADS * HEAD_DIM), lambda n, i: (n, e, 0, 0)),
        ],
        out_specs=pl.BlockSpec((A_BLOCK, QW), lambda n, i: (n * nb + i, 0)),
        out_shape=jax.ShapeDtypeStruct((M, QW), BF),
        compiler_params=_cparams("parallel", "parallel"),
    )(sinks, arr_a, arr_a, arr_a, arr_a, cache_k, cache_v)


NB_Q_ROWS = TOKEN_BLOCK // GRID_W
NB_WIN_BLOCKS = 3


def _toeplitz_kernel(rpb_ref, o_ref):
    K = rpb_ref.shape[1]
    W2 = GRID_W * GRID_W
    col = lax.broadcasted_iota(jnp.int32, (K, W2), 1)
    cq = col // GRID_W
    ck = col % GRID_W
    dc = jnp.clip(ck - cq + (NA_COLS - 1), 0, 2 * NA_COLS - 2)
    onehot = (lax.broadcasted_iota(jnp.int32, (K, W2), 0) == dc).astype(BF)
    t = rpb_ref[...]
    hi = t.astype(BF)
    r1 = t - hi.astype(F32)
    mid = r1.astype(BF)
    lo = (r1 - mid.astype(F32)).astype(BF)
    val = _dot(hi, onehot) + _dot(mid, onehot) + _dot(lo, onehot)
    c0 = jnp.clip(cq[0:1] - NA_COLS // 2, 0, GRID_W - NA_COLS)
    ok = jnp.logical_and(ck[0:1] >= c0, ck[0:1] < c0 + NA_COLS)
    o_ref[...] = jnp.where(ok, val * LOG2E, -jnp.inf)


def toeplitz_bias(rpb):
    G, H, NR, NC = rpb.shape
    R = G * H * NR
    K = 32
    flat = jnp.pad(rpb.reshape(R, NC), ((0, -R % 8), (0, K - NC)))
    out = pl.pallas_call(
        _toeplitz_kernel,
        out_shape=jax.ShapeDtypeStruct((flat.shape[0], GRID_W * GRID_W), F32),
        compiler_params=pltpu.CompilerParams(vmem_limit_bytes=VMEM_LIMIT_BYTES),
    )(flat)
    return out[:R].reshape(G, H, NR, GRID_W, GRID_W)


def neighbourhood_bias(tiles, T):
    rows = T // GRID_W
    nblk = T // TOKEN_BLOCK
    H = tiles.shape[0]
    masked = jnp.full((H, GRID_W, GRID_W), -jnp.inf, F32)
    out = []
    for blk in (0, 1, nblk - 1):
        w0 = min(max(blk - 1, 0), nblk - NB_WIN_BLOCKS) * NB_Q_ROWS
        qrows = []
        for ql in range(NB_Q_ROWS):
            rq = blk * NB_Q_ROWS + ql
            r0 = min(max(rq - NA_ROWS // 2, 0), rows - NA_ROWS)
            krow = []
            for kl in range(NB_WIN_BLOCKS * NB_Q_ROWS):
                rk = w0 + kl
                krow.append(tiles[:, rk - rq + NA_ROWS - 1] if r0 <= rk < r0 + NA_ROWS else masked)
            qrows.append(jnp.concatenate(krow, axis=-1))
        out.append(jnp.concatenate(qrows, axis=-2))
    return jnp.stack(out, axis=0)


def _nbr_kernel(q_ref, k0_ref, k1_ref, k2_ref, v0_ref, v1_ref, v2_ref, kc_ref, vc_ref, bias_ref, o_ref):
    for h in range(B_HEADS):
        sl = slice(h * HEAD_DIM, (h + 1) * HEAD_DIM)
        q = q_ref[:, sl]
        parts = []
        for w, k_ref in enumerate((k0_ref, k1_ref, k2_ref)):
            b = bias_ref[h, :, w * TOKEN_BLOCK:(w + 1) * TOKEN_BLOCK]
            parts.append(_nt_dot(q, k_ref[:, sl]) + b)
        parts.append(_nt_dot(q, kc_ref[:, sl].astype(BF)))
        (e0, e1, e2, ex), inv = _exp2_parts(parts)
        o = (_dot(e0.astype(BF), v0_ref[:, sl]) + _dot(e1.astype(BF), v1_ref[:, sl])
             + _dot(e2.astype(BF), v2_ref[:, sl]) + _dot(ex.astype(BF), vc_ref[:, sl].astype(BF))) * inv
        o_ref[:, sl] = o.astype(o_ref.dtype)


def neighbourhood_attention(qb, kb, vb, cache_k, cache_v, bias3, e, N, T):
    M = N * T
    nblk = T // TOKEN_BLOCK
    L = cache_k.shape[2]

    def win(w):
        return lambda n, i: (n * nblk + jnp.clip(i - 1, 0, nblk - NB_WIN_BLOCKS) + w, 0)

    def case(n, i):
        return (jnp.where(i == 0, 0, jnp.where(i == nblk - 1, 2, 1)), 0, 0, 0)

    blk = pl.BlockSpec((TOKEN_BLOCK, B_WIDTH), lambda n, i: (n * nblk + i, 0))
    return pl.pallas_call(
        _nbr_kernel,
        grid=(N, nblk),
        in_specs=[
            blk,
            pl.BlockSpec((TOKEN_BLOCK, B_WIDTH), win(0)),
            pl.BlockSpec((TOKEN_BLOCK, B_WIDTH), win(1)),
            pl.BlockSpec((TOKEN_BLOCK, B_WIDTH), win(2)),
            pl.BlockSpec((TOKEN_BLOCK, B_WIDTH), win(0)),
            pl.BlockSpec((TOKEN_BLOCK, B_WIDTH), win(1)),
            pl.BlockSpec((TOKEN_BLOCK, B_WIDTH), win(2)),
            pl.BlockSpec((None, None, L, B_WIDTH), lambda n, i: (n, e, 0, 0)),
            pl.BlockSpec((None, None, L, B_WIDTH), lambda n, i: (n, e, 0, 0)),
            pl.BlockSpec((None, B_HEADS, TOKEN_BLOCK, NB_WIN_BLOCKS * TOKEN_BLOCK), case),
        ],
        out_specs=blk,
        out_shape=jax.ShapeDtypeStruct((M, B_WIDTH), BF),
        compiler_params=_cparams("parallel", "arbitrary"),
    )(qb, kb, kb, kb, vb, vb, vb, cache_k, cache_v, bias3)


def _diff_dense_kernel(lam_ref, g_ref, q_ref, k_ref, v_ref, o_ref, *, lambda_init):
    lam = _diff_lambda(lam_ref, lambda_init)
    W = C_V_DIM
    for h in range(C_HEADS):
        nums, invs = [], []
        for half in range(2):
            sl = slice(h * W + half * HEAD_DIM, h * W + (half + 1) * HEAD_DIM)
            q = (q_ref[:, sl] * Q_PRESCALE).astype(BF)
            (e,), inv = _exp2_parts([_nt_dot(q, k_ref[:, sl].astype(BF))])
            nums.append(e)
            invs.append(inv)
        a = (nums[0] * invs[0] - nums[1] * (lam * invs[1])).astype(BF)
        o = _dot(a, v_ref[:, h * W:(h + 1) * W].astype(BF))
        of = o * lax.rsqrt(jnp.mean(o * o, -1, keepdims=True) + RMS_EPS)
        o_ref[:, h * W:(h + 1) * W] = ((of * g_ref[...]) * (1.0 - lambda_init)).astype(o_ref.dtype)


def _diff_lambda(lam_ref, lambda_init):
    lp = lam_ref[...]
    return (jnp.exp(jnp.sum(lp[0:1] * lp[1:2], -1, keepdims=True))
            - jnp.exp(jnp.sum(lp[2:3] * lp[3:4], -1, keepdims=True)) + lambda_init)


def _diff_pipe_kernel(lam_ref, g_ref, q_ref, k_ref, v_ref, kc_ref, vc_ref, o_ref,
                      el0, el1, ec0, ec1, cf0, cf1, *, lambda_init):
    i = pl.program_id(2)
    tq = q_ref.shape[0]

    @pl.when(i == 0)
    def _():
        for ref in (el0, el1, ec0, ec1, cf0, cf1):
            ref[...] = jnp.zeros_like(ref)

    def step(e_lat, e_ctx, coef, p_lat, p_ctx, p_coef):
        lam = _diff_lambda(lam_ref, lambda_init)

        def numerators(half):
            sl = slice(half * HEAD_DIM, (half + 1) * HEAD_DIM)
            q = q_ref[:, sl]
            (el, ec), inv = _exp2_parts([_nt_dot(q, k_ref[:, sl]), _nt_dot(q, kc_ref[:, sl].astype(BF))])
            e_lat[half] = el
            e_ctx[half] = ec
            coef[half] = jnp.broadcast_to(inv * lam if half else inv, (tq, HEAD_DIM))

        numerators(0)
        c0 = p_coef[0][:, 0:1]
        c1 = p_coef[1][:, 0:1]
        a = (p_lat[0] * c0 - p_lat[1] * c1).astype(BF)
        ac = (p_ctx[0] * c0 - p_ctx[1] * c1).astype(BF)
        o = _dot(a, v_ref[...]) + _dot(ac, vc_ref[...].astype(BF))
        of = o * lax.rsqrt(jnp.mean(o * o, -1, keepdims=True) + RMS_EPS)
        o_ref[...] = ((of * g_ref[...]) * (1.0 - lambda_init)).astype(o_ref.dtype)
        numerators(1)

    @pl.when(i % 2 == 0)
    def _():
        step(el0, ec0, cf0, el1, ec1, cf1)

    @pl.when(i % 2 == 1)
    def _():
        step(el1, ec1, cf1, el0, ec0, cf0)


def diff_attention_pipelined(q, k, v, lam_p, subln, lambda_init, N, T, ctx, widx):
    M = N * T
    tq = 256
    nq = T // tq
    W = C_V_DIM
    L = ctx[0].shape[2]
    return pl.pallas_call(
        functools.partial(_diff_pipe_kernel, lambda_init=lambda_init),
        grid=(N, C_HEADS, nq + 1),
        in_specs=[
            pl.BlockSpec((None, 4, HEAD_DIM), lambda n, h, i: (widx, 0, 0)),
            pl.BlockSpec((None, 1, W), lambda n, h, i: (widx, 0, 0)),
            pl.BlockSpec((tq, W), lambda n, h, i: (n * nq + jnp.minimum(i, nq - 1), h)),
            pl.BlockSpec((T, W), lambda n, h, i: (n, h)),
            pl.BlockSpec((T, W), lambda n, h, i: (n, h)),
            pl.BlockSpec((None, None, L, W), lambda n, h, i: (n, widx, 0, h)),
            pl.BlockSpec((None, None, L, W), lambda n, h, i: (n, widx, 0, h)),
        ],
        out_specs=pl.BlockSpec((tq, W), lambda n, h, i: (n * nq + jnp.maximum(i - 1, 0), h)),
        out_shape=jax.ShapeDtypeStruct((M, C_HEADS * W), BF),
        scratch_shapes=[pltpu.VMEM((2, tq, T), F32)] * 2 + [pltpu.VMEM((2, tq, L), F32)] * 2
        + [pltpu.VMEM((2, tq, HEAD_DIM), F32)] * 2,
        compiler_params=_cparams("parallel", "parallel", "arbitrary"),
    )(lam_p, subln.reshape(subln.shape[0], 1, W), q, k, v, *ctx)


def diff_attention_dense(q, k, v, lam_p, subln, lambda_init, N, T, widx):
    W = C_V_DIM
    row = pl.BlockSpec((T, C_HEADS * W), lambda n: (n, 0))
    return pl.pallas_call(
        functools.partial(_diff_dense_kernel, lambda_init=lambda_init),
        grid=(N,),
        in_specs=[
            pl.BlockSpec((None, 4, HEAD_DIM), lambda n: (widx, 0, 0)),
            pl.BlockSpec((None, 1, W), lambda n: (widx, 0, 0)),
            row, row, row,
        ],
        out_specs=row,
        out_shape=jax.ShapeDtypeStruct((N * T, C_HEADS * W), BF),
        compiler_params=_cparams("parallel"),
    )(lam_p, subln.reshape(subln.shape[0], 1, W), q, k, v)


def _outproj_kernel(*refs, n_in, alpha):
    o_refs = refs[:n_in]
    w_refs = refs[n_in:2 * n_in]
    x_ref, m_ref, g_ref, b_ref, wr_ref, xo_ref, h_ref, aff_ref = refs[2 * n_in:]
    y = None
    for o_r, w_r in zip(o_refs, w_refs):
        t = _dot(o_r[...], w_r[...])
        y = t if y is None else y + t
    z = alpha * x_ref[...] + m_ref[2:3, :] * y
    xn = _layer_norm(z, g_ref[...], b_ref[...])
    xo_ref[...] = xn
    hb = (xn * (1.0 + m_ref[4:5, :]) + m_ref[3:4, :]).astype(BF)
    h_ref[...] = hb
    logits = _dot(hb, wr_ref[...])
    mx = jnp.max(logits, -1, keepdims=True)
    ex = jnp.exp(logits - mx)
    aff_ref[...] = ex / jnp.sum(ex, -1, keepdims=True)


def outproj_ln_router(os_, w_out, widx, x, mods, layer, row0, per_req, T, ln_g, ln_b, w_router, alpha):
    M, D = x.shape
    tm = min(T, 256)
    E = w_router.shape[-1]
    in_specs, args, koff = [], [], 0
    for o in os_:
        in_specs.append(pl.BlockSpec((tm, o.shape[1]), lambda i: (i, 0)))
        args.append(o)
    for o in os_:
        wk = o.shape[1]
        in_specs.append(pl.BlockSpec((None, wk, D), functools.partial(lambda i, kb: (widx, kb, 0), kb=koff // wk)))
        args.append(w_out)
        koff += wk
    in_specs += [
        pl.BlockSpec((tm, D), lambda i: (i, 0)),
        pl.BlockSpec((None, None, 6, D), lambda i: (layer, row0 + ((i * tm) // T) * per_req, 0, 0)),
        pl.BlockSpec((None, 1, D), lambda i: (layer, 0, 0)),
        pl.BlockSpec((None, 1, D), lambda i: (layer, 0, 0)),
        pl.BlockSpec((None, D, E), lambda i: (layer, 0, 0)),
    ]
    args += [x, mods, ln_g, ln_b, w_router]
    return pl.pallas_call(
        functools.partial(_outproj_kernel, n_in=len(os_), alpha=alpha),
        grid=(M // tm,),
        in_specs=in_specs,
        out_specs=[
            pl.BlockSpec((tm, D), lambda i: (i, 0)),
            pl.BlockSpec((tm, D), lambda i: (i, 0)),
            pl.BlockSpec((tm, E), lambda i: (i, 0)),
        ],
        out_shape=[
            jax.ShapeDtypeStruct((M, D), F32),
            jax.ShapeDtypeStruct((M, D), BF),
            jax.ShapeDtypeStruct((M, E), F32),
        ],
        compiler_params=_cparams("parallel"),
    )(*args)


def _route_kernel(aff_ref, scol_ref, srow_ref, cnt_ref, a3_ref, *, cap, nblk):
    E = N_EXPERTS
    B = aff_ref.shape[0] // nblk
    a = aff_ref[...]
    bits = pltpu.bitcast(a, jnp.int32)

    def search(it, ans):
        cand = ans | jnp.left_shift(jnp.int32(1), 30 - it)
        cnt = jnp.sum((bits >= cand).astype(F32), axis=0, keepdims=True)
        return jnp.where(cnt >= cap, cand, ans)

    thr = lax.fori_loop(0, 31, search, jnp.zeros((1, E), jnp.int32))
    gt = bits > thr
    eq = bits == thr
    need = cap - jnp.sum(gt.astype(F32), axis=0, keepdims=True)

    r = lax.broadcasted_iota(jnp.int32, (B, B), 0)
    c = lax.broadcasted_iota(jnp.int32, (B, B), 1)
    lower = (c < r).astype(BF)
    upper = (r < c).astype(BF)
    eye = (lax.broadcasted_iota(jnp.int32, (E, E), 0) == lax.broadcasted_iota(jnp.int32, (E, E), 1)).astype(BF)

    carry = jnp.zeros((1, E), F32)
    sels = []
    for b in range(nblk):
        sl = slice(b * B, (b + 1) * B)
        eqb = eq[sl].astype(BF)
        pref = _dot(lower, eqb) + carry
        carry = carry + jnp.sum(eqb.astype(F32), axis=0, keepdims=True)
        sels.append(jnp.logical_or(gt[sl], jnp.logical_and(eq[sl], pref < need)))

    carry = jnp.zeros((1, E), F32)
    carry_t = jnp.zeros((E, 1), F32)
    for b in range(nblk):
        sl = slice(b * B, (b + 1) * B)
        selb = sels[b].astype(BF)
        cnt_ref[b:b + 1, :] = carry.astype(jnp.int32)
        pos = _dot(lower, selb) + carry
        scol_ref[sl, :] = jnp.where(sels[b], pos, -1.0)
        carry = carry + jnp.sum(selb.astype(F32), axis=0, keepdims=True)
        sel_t = _nt_dot(eye, selb)
        pos_t = _dot(sel_t.astype(BF), upper) + carry_t
        srow_ref[b] = jnp.where(sel_t > 0.5, pos_t, -1.0)
        carry_t = carry_t + jnp.sum(sel_t, axis=1, keepdims=True)
    cnt_ref[nblk:nblk + 1, :] = carry.astype(jnp.int32)

    hi = a.astype(BF)
    r1 = a - hi.astype(F32)
    mid = r1.astype(BF)
    lo = (r1 - mid.astype(F32)).astype(BF)
    lane = lax.broadcasted_iota(jnp.int32, (E, HEAD_DIM), 1)
    row = lax.broadcasted_iota(jnp.int32, (E, HEAD_DIM), 0)
    a3 = (_dot(hi, (lane == row).astype(BF)) + _dot(mid, (lane == row + E).astype(BF))
          + _dot(lo, (lane == row + 2 * E).astype(BF)))
    a3_ref[...] = a3.astype(BF)


def _gate_from_split(g3, expert):
    lane = lax.broadcasted_iota(jnp.int32, g3.shape, 1)
    pick = jnp.logical_and(lane % N_EXPERTS == expert, lane < 3 * N_EXPERTS)
    return jnp.sum(jnp.where(pick, g3, 0.0), axis=1, keepdims=True)


def route(aff, N, T):
    E = N_EXPERTS
    cap = EC_CAPACITY_FACTOR * T // E
    B = min(T, TOKEN_BLOCK)
    nblk = T // B
    return pl.pallas_call(
        functools.partial(_route_kernel, cap=cap, nblk=nblk),
        grid=(N,),
        in_specs=[pl.BlockSpec((T, E), lambda n: (n, 0))],
        out_specs=[
            pl.BlockSpec((None, T, E), lambda n: (n, 0, 0)),
            pl.BlockSpec((None, nblk, E, B), lambda n: (n, 0, 0, 0)),
            pl.BlockSpec((None, nblk + 1, E), lambda n: (n, 0, 0)),
            pl.BlockSpec((None, T, HEAD_DIM), lambda n: (n, 0, 0)),
        ],
        out_shape=[
            jax.ShapeDtypeStruct((N, T, E), F32),
            jax.ShapeDtypeStruct((N, nblk, E, B), F32),
            jax.ShapeDtypeStruct((N, nblk + 1, E), jnp.int32),
            jax.ShapeDtypeStruct((N, T, HEAD_DIM), BF),
        ],
        compiler_params=_cparams("parallel"),
    )(aff)


def _gather_small_kernel(h_ref, srow_ref, a3_ref, x_all_ref, g_all_ref, x_ref, g_ref, *, cap):
    del x_all_ref, g_all_ref
    E = N_EXPERTS
    S = E * cap
    T = h_ref.shape[0]
    sr = srow_ref[...].astype(BF)
    rep_t = (lax.broadcasted_iota(jnp.int32, (S, E), 0) // cap == lax.broadcasted_iota(jnp.int32, (S, E), 1)).astype(BF)
    slot_of_row = (lax.broadcasted_iota(jnp.int32, (S, 1), 0) % cap).astype(F32)
    onehot = (_dot(rep_t, sr) == slot_of_row).astype(BF)
    x = _dot(onehot, h_ref[...])
    for e in range(E):
        x_ref[e] = x[e * cap:(e + 1) * cap].astype(x_ref.dtype)
    gate = _gate_from_split(_dot(onehot, a3_ref[...]), lax.broadcasted_iota(jnp.int32, (S, 1), 0) // cap)
    gb = jnp.broadcast_to(gate, (S, HEAD_DIM))
    for e in range(E):
        g_ref[e] = gb[e * cap:(e + 1) * cap]


def gather_small(h, srow, a3, x_all, g_all, row0, N, T):
    E = N_EXPERTS
    cap = EC_CAPACITY_FACTOR * T // E
    D = h.shape[1]
    assert row0 % cap == 0
    rb0 = row0 // cap
    return pl.pallas_call(
        functools.partial(_gather_small_kernel, cap=cap),
        grid=(N,),
        in_specs=[
            pl.BlockSpec((T, D), lambda n: (n, 0)),
            pl.BlockSpec((None, None, E, T), lambda n: (n, 0, 0, 0)),
            pl.BlockSpec((None, T, HEAD_DIM), lambda n: (n, 0, 0)),
            pl.BlockSpec(memory_space=pl.ANY),
            pl.BlockSpec(memory_space=pl.ANY),
        ],
        out_specs=[
            pl.BlockSpec((E, cap, D), lambda n: (0, rb0 + n, 0)),
            pl.BlockSpec((E, cap, HEAD_DIM), lambda n: (0, rb0 + n, 0)),
        ],
        out_shape=[
            jax.ShapeDtypeStruct(x_all.shape, x_all.dtype),
            jax.ShapeDtypeStruct(g_all.shape, g_all.dtype),
        ],
        input_output_aliases={3: 0, 4: 1},
        compiler_params=_cparams("parallel"),
    )(h, srow, a3, x_all, g_all)


GATHER_COLS = 1024


def _gather_big_kernel(cnt_ref, h_ref, srow_ref, a3_ref, x_ref, g_ref, *, cap, nblk):
    E = N_EXPERTS
    W = min(SLOT_CHUNK, cap)
    n = pl.program_id(0)
    d = pl.program_id(1)
    b = pl.program_id(2)

    @pl.when(b == 0)
    def _():
        x_ref[...] = jnp.zeros_like(x_ref)

    @pl.when(jnp.logical_and(b == 0, d == 0))
    def _():
        g_ref[...] = jnp.zeros_like(g_ref)

    row = lax.broadcasted_iota(jnp.int32, (W, 1), 0)
    starts, ends, pieces = [], [], []
    for e in range(E):
        base = (n * E + e) * (nblk + 1) + b
        ws = pl.multiple_of(jnp.minimum((cnt_ref[base] // 16) * 16, cap - W), 16)
        pieces.append((srow_ref[e:e + 1, :] == (ws + row).astype(F32)).astype(BF))
        starts.append(ws)
        ends.append(cnt_ref[base + 1])
    onehot = jnp.concatenate(pieces, axis=0)
    prod = _dot(onehot, h_ref[...])
    for e in range(E):
        x_ref[e, pl.ds(starts[e], W), :] += prod[e * W:(e + 1) * W].astype(x_ref.dtype)

    @pl.when(d == 0)
    def _():
        split = _dot(onehot, a3_ref[...])
        for e in range(E):
            g_ref[e, pl.ds(starts[e], W), :] += split[e * W:(e + 1) * W]

    for e in range(E):
        def more(k, carry, e=e):
            first = starts[e] + k * W
            ws = pl.multiple_of(jnp.minimum(first, cap - W), 16)
            ids = ws + row
            piece = jnp.logical_and(srow_ref[e:e + 1, :] == ids.astype(F32), ids >= first).astype(BF)
            x_ref[e, pl.ds(ws, W), :] += _dot(piece, h_ref[...]).astype(x_ref.dtype)

            @pl.when(d == 0)
            def _():
                g_ref[e, pl.ds(ws, W), :] += _dot(piece, a3_ref[...])

            return carry

        lax.fori_loop(1, (ends[e] - starts[e] + W - 1) // W, more, 0)

    @pl.when(jnp.logical_and(b == nblk - 1, d == 0))
    def _():
        for e in range(E):
            g_ref[e] = jnp.broadcast_to(_gate_from_split(g_ref[e], e), (cap, HEAD_DIM))


def gather_big(h, srow, cnt_flat, a3, rows_total, N, T):
    E = N_EXPERTS
    cap = EC_CAPACITY_FACTOR * T // E
    D = h.shape[1]
    nblk = T // TOKEN_BLOCK
    Dc = min(D, GATHER_COLS)
    assert cap % 16 == 0
    return pl.pallas_call(
        functools.partial(_gather_big_kernel, cap=cap, nblk=nblk),
        grid_spec=pltpu.PrefetchScalarGridSpec(
            num_scalar_prefetch=1,
            grid=(N, D // Dc, nblk),
            in_specs=[
                pl.BlockSpec((TOKEN_BLOCK, Dc), lambda n, d, b, c: (n * nblk + b, d)),
                pl.BlockSpec((None, None, E, TOKEN_BLOCK), lambda n, d, b, c: (n, b, 0, 0)),
                pl.BlockSpec((None, TOKEN_BLOCK, HEAD_DIM), lambda n, d, b, c: (n, b, 0)),
            ],
            out_specs=[
                pl.BlockSpec((E, cap, Dc), lambda n, d, b, c: (0, n, d)),
                pl.BlockSpec((E, cap, HEAD_DIM), lambda n, d, b, c: (0, n, 0)),
            ],
        ),
        out_shape=[
            jax.ShapeDtypeStruct((E, rows_total, D), BF),
            jax.ShapeDtypeStruct((E, rows_total, HEAD_DIM), F32),
        ],
        compiler_params=_cparams("parallel", "arbitrary", "arbitrary"),
    )(cnt_flat, h, srow, a3)


def _gate_up_kernel(x_ref, wg_ref, wu_ref, o_ref):
    x = x_ref[...]
    a = _dot(x, wg_ref[...].astype(BF))
    u = _dot(x, wu_ref[...].astype(BF))
    o_ref[...] = ((a * jax.nn.sigmoid(a)) * u).astype(o_ref.dtype)


def expert_gate_up(x, w_gate, w_up, layer):
    E, R, D = x.shape
    F = w_gate.shape[-1]
    tf = min(F, 256)
    return pl.pallas_call(
        _gate_up_kernel,
        grid=(E, F // tf),
        in_specs=[
            pl.BlockSpec((None, R, D), lambda e, f: (e, 0, 0)),
            pl.BlockSpec((None, None, D, tf), lambda e, f: (layer, e, 0, f)),
            pl.BlockSpec((None, None, D, tf), lambda e, f: (layer, e, 0, f)),
        ],
        out_specs=pl.BlockSpec((None, R, tf), lambda e, f: (e, 0, f)),
        out_shape=jax.ShapeDtypeStruct((E, R, F), BF),
        compiler_params=_cparams("parallel", "arbitrary"),
    )(x, w_gate, w_up)


def _down_kernel(h_ref, w_ref, g_ref, o_ref):
    o_ref[...] = (_dot(h_ref[...], w_ref[...].astype(BF)) * g_ref[:, 0:1]).astype(o_ref.dtype)


def expert_down(h, w_down, gate, layer):
    E, R, F = h.shape
    D = w_down.shape[-1]
    td = min(D, 512)
    return pl.pallas_call(
        _down_kernel,
        grid=(E, D // td),
        in_specs=[
            pl.BlockSpec((None, R, F), lambda e, d: (e, 0, 0)),
            pl.BlockSpec((None, None, F, td), lambda e, d: (layer, e, 0, d)),
            pl.BlockSpec((None, R, HEAD_DIM), lambda e, d: (e, 0, 0)),
        ],
        out_specs=pl.BlockSpec((None, R, td), lambda e, d: (e, 0, d)),
        out_shape=jax.ShapeDtypeStruct((E, R, D), BF),
        compiler_params=_cparams("parallel", "arbitrary"),
    )(h, w_down, gate)


def _residual_epilogue(f, x_ref, m_ref, mn_ref, g_ref, b_ref, xo_ref, h_ref, alpha):
    z = alpha * x_ref[...] + m_ref[5:6, :] * f
    xn = _layer_norm(z, g_ref[...], b_ref[...])
    xo_ref[...] = xn
    h_ref[...] = (xn * (1.0 + mn_ref[1:2, :]) + mn_ref[0:1, :]).astype(h_ref.dtype)


def _scatter_small_kernel(y_ref, scol_ref, x_ref, m_ref, mn_ref, g_ref, b_ref, xo_ref, h_ref, *, cap, alpha):
    E = N_EXPERTS
    S = E * cap
    sc = scol_ref[...].astype(BF)
    rep = (lax.broadcasted_iota(jnp.int32, (E, S), 0) == lax.broadcasted_iota(jnp.int32, (E, S), 1) // cap).astype(BF)
    slot_of_col = (lax.broadcasted_iota(jnp.int32, (1, S), 1) % cap).astype(F32)
    onehot_t = (_dot(sc, rep) == slot_of_col).astype(BF)
    y = jnp.concatenate([y_ref[e] for e in range(E)], axis=0)
    f = _dot(onehot_t, y)
    _residual_epilogue(f, x_ref, m_ref, mn_ref, g_ref, b_ref, xo_ref, h_ref, alpha)


def scatter_small(y, y_row0, scol, x, mods, layer, next_layer, row0, per_req, ln_g, ln_b, N, T, alpha):
    E = N_EXPERTS
    cap = EC_CAPACITY_FACTOR * T // E
    D = x.shape[1]
    assert y_row0 % cap == 0
    return pl.pallas_call(
        functools.partial(_scatter_small_kernel, cap=cap, alpha=alpha),
        grid=(N,),
        in_specs=[
            pl.BlockSpec((E, cap, D), lambda n: (0, y_row0 // cap + n, 0)),
            pl.BlockSpec((None, T, E), lambda n: (n, 0, 0)),
            pl.BlockSpec((T, D), lambda n: (n, 0)),
            pl.BlockSpec((None, None, 6, D), lambda n: (layer, row0 + n * per_req, 0, 0)),
            pl.BlockSpec((None, None, 6, D), lambda n: (next_layer, row0 + n * per_req, 0, 0)),
            pl.BlockSpec((None, 1, D), lambda n: (layer, 0, 0)),
            pl.BlockSpec((None, 1, D), lambda n: (layer, 0, 0)),
        ],
        out_specs=[pl.BlockSpec((T, D), lambda n: (n, 0)), pl.BlockSpec((T, D), lambda n: (n, 0))],
        out_shape=[jax.ShapeDtypeStruct((N * T, D), F32), jax.ShapeDtypeStruct((N * T, D), BF)],
        compiler_params=_cparams("parallel"),
    )(y, scol, x, mods, mods, ln_g, ln_b)


def _scatter_big_kernel(cnt_ref, y_ref, scol_ref, x_ref, m_ref, mn_ref, g_ref, b_ref, xo_ref, h_ref,
                        *, nblk, tiles, alpha):
    acc = xo_ref
    E = N_EXPERTS
    C = SLOT_CHUNK
    B = TOKEN_BLOCK
    G = y_ref.shape[0]
    cap = y_ref.shape[1]
    WIN = min(C, cap)
    n = pl.program_id(0)
    tb = pl.program_id(1)
    g = pl.program_id(2)

    @pl.when(g == 0)
    def _():
        acc[...] = jnp.zeros_like(acc)

    lane = lax.broadcasted_iota(jnp.int32, (B, E), 1)
    for bl in range(tiles):
        blk = tb * tiles + bl
        sc = scol_ref[bl * B:(bl + 1) * B, :]
        hits, wins = [], []
        for gi in range(G):
            e = g * G + gi
            base = (n * E + e) * (nblk + 1)
            lo = cnt_ref[base + blk]
            hi = cnt_ref[base + blk + 1]
            scol = jnp.sum(jnp.where(lane == e, sc, 0.0), axis=1, keepdims=True)
            ws = pl.multiple_of(jnp.minimum((lo // 16) * 16, cap - WIN), 16)
            col = lax.broadcasted_iota(jnp.int32, (1, WIN), 1)
            hits.append((scol == (ws + col).astype(F32)).astype(BF))
            wins.append(y_ref[gi, pl.ds(ws, WIN), :])

            def more(k, carry, gi=gi, scol=scol, ws=ws, col=col):
                first = ws + k * WIN
                ks = pl.multiple_of(jnp.minimum(first, cap - WIN), 16)
                ids = ks + col
                hit = jnp.logical_and(scol == ids.astype(F32), ids >= first)
                acc[bl * B:(bl + 1) * B, :] += _dot(hit.astype(BF), y_ref[gi, pl.ds(ks, WIN), :])
                return carry

            lax.fori_loop(1, (hi - ws + WIN - 1) // WIN, more, 0)
        f = None
        for p in range(0, G, 2):
            d = _dot(jnp.concatenate(hits[p:p + 2], axis=1), jnp.concatenate(wins[p:p + 2], axis=0))
            f = d if f is None else f + d
        acc[bl * B:(bl + 1) * B, :] += f

    @pl.when(g == pl.num_programs(2) - 1)
    def _():
        for r in range(0, tiles * B, B):
            rows = slice(r, r + B)
            z = alpha * x_ref[rows, :] + m_ref[5:6, :] * acc[rows, :]
            xn = _layer_norm(z, g_ref[...], b_ref[...])
            xo_ref[rows, :] = xn
            h_ref[rows, :] = (xn * (1.0 + mn_ref[1:2, :]) + mn_ref[0:1, :]).astype(h_ref.dtype)


SCATTER_EXPERT_GROUP = 2


def scatter_big(y, y_row0, scol, cnt_flat, x, mods, layer, next_layer, row0, per_req, ln_g, ln_b, N, T, alpha):
    E = N_EXPERTS
    cap = EC_CAPACITY_FACTOR * T // E
    D = x.shape[1]
    nblk = T // TOKEN_BLOCK
    tiles = max(t for t in (1, 2, 4) if nblk % t == 0)
    tt = tiles * TOKEN_BLOCK
    ntt = T // tt
    G = SCATTER_EXPERT_GROUP
    assert y_row0 % cap == 0 and cap % 16 == 0
    return pl.pallas_call(
        functools.partial(_scatter_big_kernel, nblk=nblk, tiles=tiles, alpha=alpha),
        grid_spec=pltpu.PrefetchScalarGridSpec(
            num_scalar_prefetch=1,
            grid=(N, ntt, E // G),
            in_specs=[
                pl.BlockSpec((G, cap, D), lambda n, t, e, c: (e, y_row0 // cap + n, 0)),
                pl.BlockSpec((None, tt, E), lambda n, t, e, c: (n, t, 0)),
                pl.BlockSpec((tt, D), lambda n, t, e, c: (n * ntt + t, 0), pipeline_mode=pl.Buffered(1)),
                pl.BlockSpec((None, None, 6, D), lambda n, t, e, c: (layer, row0 + n * per_req, 0, 0)),
                pl.BlockSpec((None, None, 6, D), lambda n, t, e, c: (next_layer, row0 + n * per_req, 0, 0)),
                pl.BlockSpec((None, 1, D), lambda n, t, e, c: (layer, 0, 0)),
                pl.BlockSpec((None, 1, D), lambda n, t, e, c: (layer, 0, 0)),
            ],
            out_specs=[
                pl.BlockSpec((tt, D), lambda n, t, e, c: (n * ntt + t, 0)),
                pl.BlockSpec((tt, D), lambda n, t, e, c: (n * ntt + t, 0)),
            ],
        ),
        out_shape=[jax.ShapeDtypeStruct((N * T, D), F32), jax.ShapeDtypeStruct((N * T, D), BF)],
        compiler_params=_cparams("parallel", "parallel", "arbitrary"),
    )(cnt_flat, y, scol, x, mods, mods, ln_g, ln_b)


def moe_layer(groups, mods, layer, next_layer, ln_g, ln_b, w_gate, w_up, w_down, alpha):
    E = N_EXPERTS
    rows_total = sum(gr["N"] * (EC_CAPACITY_FACTOR * gr["T"] // E) for gr in groups)
    routed, row0s, rows, x_all, g_all = [], [], 0, None, None
    for gr in groups:
        N, T = gr["N"], gr["T"]
        scol, srow, cnt, a3 = route(gr["aff"], N, T)
        if T <= TOKEN_BLOCK:
            assert x_all is not None
            cnt_flat = None
            x_all, g_all = gather_small(gr["h"], srow, a3, x_all, g_all, rows, N, T)
        else:
            assert x_all is None
            cnt_flat = jnp.transpose(cnt, (0, 2, 1)).reshape(-1)
            x_all, g_all = gather_big(gr["h"], srow, cnt_flat, a3, rows_total, N, T)
        routed.append((scol, cnt_flat))
        row0s.append(rows)
        rows += N * (EC_CAPACITY_FACTOR * T // E)
    hm = expert_gate_up(x_all, w_gate, w_up, layer)
    y = expert_down(hm, w_down, g_all, layer)
    out = []
    for gr, (scol, cnt_flat), y_row0 in zip(groups, routed, row0s):
        tail = (gr["x"], mods, layer, next_layer, gr["row0"], gr["per_req"], ln_g, ln_b, gr["N"], gr["T"], alpha)
        if cnt_flat is None:
            out.append(scatter_small(y, y_row0, scol, *tail))
        else:
            out.append(scatter_big(y, y_row0, scol, cnt_flat, *tail))
    return out


def kernel(x_prompt, x_sample, cache_a_k, cache_a_v, cache_b_k, cache_b_v, cache_c_k, cache_c_v, c, c_ctx, w_mod, b_mod, w_in_even, w_out_even, a_sink, na_rpb, w_in_odd, w_out_odd, diff_lambda, diff_subln, ln_g, ln_b, w_router, w_gate, w_up, w_down):
    NP, TP, D = x_prompt.shape
    NS, TS, _ = x_sample.shape
    depth = w_mod.shape[0]
    L = cache_a_k.shape[2]
    alpha = (2 * depth) ** 0.25
    n_even = (depth + 1) // 2
    n_odd = depth // 2

    rows = 1 + NS
    rpad = -rows % 8
    cond = jnp.concatenate([c_ctx[None], c, jnp.zeros((rpad, D), F32)], axis=0)
    mods = adaln_all(cond, w_mod, b_mod).reshape(depth, rows + rpad, 6, D)

    rope = rope_tables(TS)
    sinks = a_sink.reshape(-1)
    rpb_tiles = toeplitz_bias(na_rpb)
    w_out_even_bf = w_out_even.astype(BF)
    w_out_odd_bf = w_out_odd.astype(BF)
    w_router_bf = w_router.astype(BF)
    ln_g4 = ln_g.reshape(depth, 2, 1, D)
    ln_b4 = ln_b.reshape(depth, 2, 1, D)
    ca_k = cache_a_k.reshape(NS, n_even, L, A_KV_HEADS * HEAD_DIM)
    ca_v = cache_a_v.reshape(NS, n_even, L, A_KV_HEADS * HEAD_DIM)
    cb_k = cache_b_k.reshape(NS, n_even, L, B_WIDTH)
    cb_v = cache_b_v.reshape(NS, n_even, L, B_WIDTH)
    cc_k = cache_c_k.reshape(NS, n_odd, L, C_HEADS * 2 * HEAD_DIM)
    cc_v = cache_c_v.reshape(NS, n_odd, L, C_HEADS * C_V_DIM)

    xp = x_prompt.reshape(NP * TP, D)
    xs = x_sample.reshape(NS * TS, D)
    hp = modulate_rows(xp, mods, 0, 0, 0, TP)
    hs = modulate_rows(xs, mods, 0, 1, 1, TS)

    new_a_k, new_a_v, new_b_k, new_b_v, new_c_k, new_c_v = [], [], [], [], [], []
    QA = A_Q_HEADS * HEAD_DIM
    KA = A_KV_HEADS * HEAD_DIM
    for l in range(depth):
        if l % 2 == 0:
            e = l // 2
            b0 = A_WIDTH
            pa = project(hp, w_in_even, e, 0, A_WIDTH, 0, None, TP, F32)
            pqb = project(hp, w_in_even, e, b0, B_WIDTH, 0, None, TP, F32)
            pkb = project(hp, w_in_even, e, b0 + B_WIDTH, B_WIDTH, 0, None, TP, F32)
            pvb = project(hp, w_in_even, e, b0 + 2 * B_WIDTH, B_WIDTH, 0, None, TP, F32)
            new_a_k.append(pa[:, QA:QA + KA].reshape(NP, TP, A_KV_HEADS, HEAD_DIM))
            new_a_v.append(pa[:, QA + KA:QA + 2 * KA].reshape(NP, TP, A_KV_HEADS, HEAD_DIM))
            new_b_k.append(pkb.reshape(NP, TP, B_HEADS, HEAD_DIM))
            new_b_v.append(pvb.reshape(NP, TP, B_HEADS, HEAD_DIM))
            op = [ctx_even_attention(pa, pqb, pkb, pvb, sinks, e, NP, TP)]

            sa = project(hs, w_in_even, e, 0, A_WIDTH, A_Q_HEADS + A_KV_HEADS, rope, TS, BF, A_Q_HEADS)
            sqb = project(hs, w_in_even, e, b0, B_WIDTH, 0, rope, TS, BF, B_HEADS)
            skb = project(hs, w_in_even, e, b0 + B_WIDTH, B_WIDTH, 0, rope, TS, BF)
            svb = project(hs, w_in_even, e, b0 + 2 * B_WIDTH, B_WIDTH, 0, rope, TS, BF)
            oa = window_attention(sa, ca_k, ca_v, sinks, e, NS, TS)
            bias3 = neighbourhood_bias(rpb_tiles[e], TS)
            ob = neighbourhood_attention(sqb, skb, svb, cb_k, cb_v, bias3, e, NS, TS)
            os_ = [oa, ob]
            w_out, widx = w_out_even_bf, e
        else:
            o = l // 2
            lambda_init = 0.8 - 0.6 * math.exp(-0.3 * l)
            W = C_HEADS * C_V_DIM
            pq = project(hp, w_in_odd, o, 0, W, 0, None, TP, F32)
            pk = project(hp, w_in_odd, o, W, W, 0, None, TP, F32)
            pv = project(hp, w_in_odd, o, 2 * W, W, 0, None, TP, F32)
            new_c_k.append(pk.reshape(NP, TP, C_HEADS, 2, HEAD_DIM))
            new_c_v.append(pv.reshape(NP, TP, C_HEADS, C_V_DIM))
            op = [diff_attention_dense(pq, pk, pv, diff_lambda, diff_subln, lambda_init, NP, TP, o)]

            sq = project(hs, w_in_odd, o, 0, W, 2 * C_HEADS, rope, TS, BF, 2 * C_HEADS)
            sk = project(hs, w_in_odd, o, W, W, 2 * C_HEADS, rope, TS, BF)
            sv = project(hs, w_in_odd, o, 2 * W, W, 0, rope, TS, BF)
            os_ = [diff_attention_pipelined(sq, sk, sv, diff_lambda, diff_subln, lambda_init, NS, TS, (cc_k, cc_v), o)]
            w_out, widx = w_out_odd_bf, o

        lg1, lb1 = ln_g4[:, 0], ln_b4[:, 0]
        lg2, lb2 = ln_g4[:, 1], ln_b4[:, 1]
        xp, hp, affp = outproj_ln_router(op, w_out, widx, xp, mods, l, 0, 0, TP, lg1, lb1, w_router_bf, alpha)
        xs, hs, affs = outproj_ln_router(os_, w_out, widx, xs, mods, l, 1, 1, TS, lg1, lb1, w_router_bf, alpha)
        nl = min(l + 1, depth - 1)
        groups = [dict(aff=affs, h=hs, x=xs, N=NS, T=TS, row0=1, per_req=1),
                  dict(aff=affp, h=hp, x=xp, N=NP, T=TP, row0=0, per_req=0)]
        (xs, hs), (xp, hp) = moe_layer(groups, mods, l, nl, lg2, lb2, w_gate, w_up, w_down, alpha)

    return (xp.reshape(NP, TP, D), xs.reshape(NS, TS, D),
            jnp.stack(new_a_k, axis=1), jnp.stack(new_a_v, axis=1),
            jnp.stack(new_b_k, axis=1), jnp.stack(new_b_v, axis=1),
            jnp.stack(new_c_k, axis=1), jnp.stack(new_c_v, axis=1))
```

```python
import functools
import math

import jax
import jax.numpy as jnp
from jax import lax
from jax.experimental import pallas as pl
from jax.experimental.pallas import tpu as pltpu

BF = jnp.bfloat16
F32 = jnp.float32

GRID_W = 64
HEAD_DIM = 128
A_Q_HEADS = 8
A_KV_HEADS = 2
A_GROUP = A_Q_HEADS // A_KV_HEADS
A_BLOCK = 128
B_HEADS = 8
NA_ROWS = 8
NA_COLS = 16
C_HEADS = 8
C_V_DIM = 2 * HEAD_DIM
N_EXPERTS = 16
EC_CAPACITY_FACTOR = 2
ROPE_BASE = 10000.0
LN_EPS = 1e-5
RMS_EPS = 1e-5
ATTN_SCALE = HEAD_DIM ** -0.5
A_WIDTH = (A_Q_HEADS + 2 * A_KV_HEADS) * HEAD_DIM
B_WIDTH = B_HEADS * HEAD_DIM

VMEM_LIMIT_BYTES = 52 * 1024 * 1024
TOKEN_BLOCK = 256
SLOT_CHUNK = 128
MODULATE_ROWS = 512
PROJ_ROWS = 1024
OUTPROJ_ROWS = 256
DIFF_QUERY_TILE = 256
EXPERT_UP_COLS = 256
EXPERT_DOWN_COLS = 512


def _cparams(*sem):
    return pltpu.CompilerParams(dimension_semantics=sem, vmem_limit_bytes=VMEM_LIMIT_BYTES)


def _nt_dot(a, b):
    return lax.dot_general(a, b, (((1,), (1,)), ((), ())), preferred_element_type=F32)


def _dot(a, b):
    return jnp.dot(a, b, preferred_element_type=F32)


def _layer_norm(z, g, b):
    mu = jnp.mean(z, -1, keepdims=True)
    d = z - mu
    var = jnp.mean(d * d, -1, keepdims=True)
    return d * lax.rsqrt(var + LN_EPS) * g + b


def _adaln_kernel(c_ref, w_ref, b_ref, o_ref):
    c = c_ref[...]
    s = (c * jax.nn.sigmoid(c)).astype(BF)
    o_ref[...] = _dot(s, w_ref[...].astype(BF)) + b_ref[...]


def adaln_all(cond, w_mod, b_mod):
    L, D, D6 = w_mod.shape
    R = cond.shape[0]
    tn = D // 2
    return pl.pallas_call(
        _adaln_kernel,
        grid=(L, D6 // tn),
        in_specs=[
            pl.BlockSpec((R, D), lambda l, j: (0, 0)),
            pl.BlockSpec((None, D, tn), lambda l, j: (l, 0, j)),
            pl.BlockSpec((None, 1, tn), lambda l, j: (l, 0, j)),
        ],
        out_specs=pl.BlockSpec((None, R, tn), lambda l, j: (l, 0, j)),
        out_shape=jax.ShapeDtypeStruct((L, R, D6), F32),
        compiler_params=_cparams("parallel", "parallel"),
    )(cond, w_mod, b_mod.reshape(L, 1, D6))


def _modulate_kernel(x_ref, m_ref, o_ref):
    o_ref[...] = (x_ref[...] * (1.0 + m_ref[1:2, :]) + m_ref[0:1, :]).astype(o_ref.dtype)


def modulate_rows(x, mods, layer, row0, per_req, T):
    M, D = x.shape
    tt = min(T, MODULATE_ROWS)
    return pl.pallas_call(
        _modulate_kernel,
        grid=(M // tt,),
        in_specs=[
            pl.BlockSpec((tt, D), lambda i: (i, 0)),
            pl.BlockSpec((None, None, 6, D), lambda i: (layer, row0 + ((i * tt) // T) * per_req, 0, 0)),
        ],
        out_specs=pl.BlockSpec((tt, D), lambda i: (i, 0)),
        out_shape=jax.ShapeDtypeStruct((M, D), BF),
        compiler_params=_cparams("parallel"),
    )(x, mods)


def _swap32(x):
    lane = lax.broadcasted_iota(jnp.int32, (1, HEAD_DIM), 1)
    return jnp.where((lane % 64) < 32, pltpu.roll(x, HEAD_DIM - 32, 1), pltpu.roll(x, 32, 1))


def _proj_kernel(*refs, n_rope_heads, n_scaled_heads, heads_per_tile):
    if n_rope_heads:
        h_ref, w_ref, cos_ref, sin_ref, o_ref, wbf = refs
    else:
        h_ref, w_ref, o_ref, wbf = refs
    j = pl.program_id(0)
    i = pl.program_id(1)

    @pl.when(i == 0)
    def _():
        wbf[...] = w_ref[...].astype(BF)

    acc = _dot(h_ref[...], wbf[...])
    bounds = sorted({0, min(n_rope_heads, n_scaled_heads), n_rope_heads, n_scaled_heads, 1 << 30})
    for hh in range(heads_per_tile):
        sl = slice(hh * HEAD_DIM, (hh + 1) * HEAD_DIM)
        x = acc[:, sl]
        head = j * heads_per_tile + hh
        for lo, hi in zip(bounds[:-1], bounds[1:]):
            rot = lo < n_rope_heads
            scale = Q_PRESCALE if lo < n_scaled_heads else None

            def emit(x=x, sl=sl, rot=rot, scale=scale):
                y = x * cos_ref[...] + _swap32(x) * sin_ref[...] if rot else x
                o_ref[:, sl] = (y if scale is None else y * scale).astype(o_ref.dtype)

            if len(bounds) == 2:
                emit()
            else:
                pl.when(jnp.logical_and(head >= lo, head < hi))(emit)


def project(h, w, widx, col0, width, n_rope_heads, rope, T, out_dtype, n_scaled_heads=0):
    M, D = h.shape
    heads = width // HEAD_DIM
    hpt = max(k for k in range(1, 9) if heads % k == 0 and col0 % (k * HEAD_DIM) == 0)
    tn = hpt * HEAD_DIM
    tm = min(T if n_rope_heads else M, PROJ_ROWS)
    tpb = max(T // tm, 1)
    in_specs = [
        pl.BlockSpec((tm, D), lambda j, i: (i, 0)),
        pl.BlockSpec((None, D, tn), lambda j, i: (widx, 0, col0 // tn + j)),
    ]
    args = [h, w]
    if n_rope_heads:
        in_specs += [pl.BlockSpec((tm, HEAD_DIM), lambda j, i: (i % tpb, 0))] * 2
        args += list(rope)
    return pl.pallas_call(
        functools.partial(_proj_kernel, n_rope_heads=n_rope_heads, n_scaled_heads=n_scaled_heads, heads_per_tile=hpt),
        grid=(width // tn, M // tm),
        in_specs=in_specs,
        out_specs=pl.BlockSpec((tm, tn), lambda j, i: (i, j)),
        out_shape=jax.ShapeDtypeStruct((M, width), out_dtype),
        scratch_shapes=[pltpu.VMEM((D, tn), BF)],
        compiler_params=_cparams("parallel", "arbitrary"),
    )(*args)


def rope_tables(T):
    t = jnp.arange(T)
    row = (t // GRID_W).astype(F32)
    col = (t % GRID_W).astype(F32)
    axis_dim = HEAD_DIM // 2
    inv_freq = ROPE_BASE ** (-jnp.arange(0, axis_dim, 2, dtype=F32) / axis_dim)
    ar = row[:, None] * inv_freq
    ac = col[:, None] * inv_freq
    cos = jnp.concatenate([jnp.cos(ar), jnp.cos(ar), jnp.cos(ac), jnp.cos(ac)], -1)
    sin = jnp.concatenate([-jnp.sin(ar), jnp.sin(ar), -jnp.sin(ac), jnp.sin(ac)], -1)
    return cos, sin


LOG2E = math.log2(math.e)
Q_PRESCALE = ATTN_SCALE * LOG2E


def _exp2_parts(parts, sink=None):
    m = None
    for s in parts:
        mi = jnp.max(s, -1, keepdims=True)
        m = mi if m is None else jnp.maximum(m, mi)
    if sink is not None:
        m = jnp.maximum(m, sink)
    es = [jnp.exp2(s - m) for s in parts]
    den = None
    for e in es:
        di = jnp.sum(e, -1, keepdims=True)
        den = di if den is None else den + di
    if sink is not None:
        den = den + jnp.exp2(sink - m)
    return es, 1.0 / den


def _softmax_parts(parts, sink=None):
    m = None
    for s in parts:
        mi = jnp.max(s, -1, keepdims=True)
        m = mi if m is None else jnp.maximum(m, mi)
    if sink is not None:
        m = jnp.maximum(m, sink)
    es = [jnp.exp(s - m) for s in parts]
    den = None
    for e in es:
        di = jnp.sum(e, -1, keepdims=True)
        den = di if den is None else den + di
    if sink is not None:
        den = den + jnp.exp(sink - m)
    inv = 1.0 / den
    return [e * inv for e in es]


def _ctx_even_kernel(sink_ref, a_ref, qb_ref, kb_ref, vb_ref, o_ref, *, sink_base):
    T = a_ref.shape[0]
    for kv in range(A_KV_HEADS):
        k = a_ref[:, (A_Q_HEADS + kv) * HEAD_DIM:(A_Q_HEADS + kv + 1) * HEAD_DIM].astype(BF)
        v = a_ref[:, (A_Q_HEADS + A_KV_HEADS + kv) * HEAD_DIM:(A_Q_HEADS + A_KV_HEADS + kv + 1) * HEAD_DIM].astype(BF)
        for g in range(A_GROUP):
            hq = kv * A_GROUP + g
            q = a_ref[:, hq * HEAD_DIM:(hq + 1) * HEAD_DIM].astype(BF)
            s = _nt_dot(q, k) * ATTN_SCALE
            sink = jnp.full((T, 1), sink_ref[sink_base + hq], F32)
            (p,) = _softmax_parts([s], sink)
            o_ref[:, hq * HEAD_DIM:(hq + 1) * HEAD_DIM] = _dot(p.astype(BF), v).astype(o_ref.dtype)
    for hb in range(B_HEADS):
        sl = slice(hb * HEAD_DIM, (hb + 1) * HEAD_DIM)
        s = _nt_dot(qb_ref[:, sl].astype(BF), kb_ref[:, sl].astype(BF)) * ATTN_SCALE
        (p,) = _softmax_parts([s])
        o_ref[:, A_Q_HEADS * HEAD_DIM + hb * HEAD_DIM:A_Q_HEADS * HEAD_DIM + (hb + 1) * HEAD_DIM] = _dot(
            p.astype(BF), vb_ref[:, sl].astype(BF)).astype(o_ref.dtype)


def ctx_even_attention(arr_a, qb, kb, vb, sinks, e, N, T):
    M = N * T
    return pl.pallas_call(
        functools.partial(_ctx_even_kernel, sink_base=e * A_Q_HEADS),
        grid=(N,),
        in_specs=[
            pl.BlockSpec(memory_space=pltpu.SMEM),
            pl.BlockSpec((T, A_WIDTH), lambda n: (n, 0)),
            pl.BlockSpec((T, B_WIDTH), lambda n: (n, 0)),
            pl.BlockSpec((T, B_WIDTH), lambda n: (n, 0)),
            pl.BlockSpec((T, B_WIDTH), lambda n: (n, 0)),
        ],
        out_specs=pl.BlockSpec((T, A_Q_HEADS * HEAD_DIM + B_WIDTH), lambda n: (n, 0)),
        out_shape=jax.ShapeDtypeStruct((M, A_Q_HEADS * HEAD_DIM + B_WIDTH), BF),
        compiler_params=_cparams("parallel"),
    )(sinks, arr_a, qb, kb, vb)


def _win_kernel(sink_ref, q_ref, kvp_ref, kvc_ref, kvn_ref, kc_ref, vc_ref, o_ref, *, sink_base, nb):
    i = pl.program_id(1)
    R = A_GROUP * A_BLOCK
    qq = lax.broadcasted_iota(jnp.int32, (R, 1), 0) % A_BLOCK
    kk = lax.broadcasted_iota(jnp.int32, (1, 3 * A_BLOCK), 1)
    in_window = jnp.abs(kk - A_BLOCK - qq) <= A_BLOCK
    exists = jnp.logical_and(jnp.logical_or(kk >= A_BLOCK, i > 0), jnp.logical_or(kk < 2 * A_BLOCK, i < nb - 1))
    ok_band = jnp.logical_and(in_window, exists)
    for kv in range(A_KV_HEADS):
        ksl = slice(kv * HEAD_DIM, (kv + 1) * HEAD_DIM)
        vsl = slice((A_KV_HEADS + kv) * HEAD_DIM, (A_KV_HEADS + kv + 1) * HEAD_DIM)
        q = jnp.concatenate(
            [q_ref[:, (kv * A_GROUP + g) * HEAD_DIM:(kv * A_GROUP + g + 1) * HEAD_DIM] for g in range(A_GROUP)], axis=0)
        sink = jnp.concatenate(
            [jnp.full((A_BLOCK, 1), sink_ref[sink_base + kv * A_GROUP + g] * LOG2E, F32) for g in range(A_GROUP)],
            axis=0)
        k_band = jnp.concatenate([kvp_ref[:, ksl], kvc_ref[:, ksl], kvn_ref[:, ksl]], axis=0)
        v_band = jnp.concatenate([kvp_ref[:, vsl], kvc_ref[:, vsl], kvn_ref[:, vsl]], axis=0)
        s_b = jnp.where(ok_band, _nt_dot(q, k_band), -jnp.inf)
        s_x = _nt_dot(q, kc_ref[:, ksl].astype(BF))
        (e_b, e_x), inv = _exp2_parts([s_b, s_x], sink)
        o = (_dot(e_b.astype(BF), v_band) + _dot(e_x.astype(BF), vc_ref[:, ksl].astype(BF))) * inv
        for g in range(A_GROUP):
            hq = kv * A_GROUP + g
            o_ref[:, hq * HEAD_DIM:(hq + 1) * HEAD_DIM] = o[g * A_BLOCK:(g + 1) * A_BLOCK].astype(o_ref.dtype)


def window_attention(arr_a, cache_k, cache_v, sinks, e, N, T):
    M = N * T
    nb = T // A_BLOCK
    L = cache_k.shape[2]
    QW = A_Q_HEADS * HEAD_DIM
    KVW = 2 * A_KV_HEADS * HEAD_DIM
    kvblk = QW // KVW
    return pl.pallas_call(
        functools.partial(_win_kernel, sink_base=e * A_Q_HEADS, nb=nb),
        grid=(N, nb),
        in_specs=[
            pl.BlockSpec(memory_space=pltpu.SMEM),
            pl.BlockSpec((A_BLOCK, QW), lambda n, i: (n * nb + i, 0)),
            pl.BlockSpec((A_BLOCK, KVW), lambda n, i: (n * nb + jnp.maximum(i - 1, 0), kvblk)),
            pl.BlockSpec((A_BLOCK, KVW), lambda n, i: (n * nb + i, kvblk)),
            pl.BlockSpec((A_BLOCK, KVW), lambda n, i: (n * nb + jnp.minimum(i + 1, nb - 1), kvblk)),
            pl.BlockSpec((None, None, L, A_KV_HEADS * HEAD_DIM), lambda n, i: (n, e, 0, 0)),
            pl.BlockSpec((None, None, L, A_KV_HEADS * HEAD_DIM), lambda n, i: (n, e, 0, 0)),
        ],
        out_specs=pl.BlockSpec((A_BLOCK, QW), lambda n, i: (n * nb + i, 0)),
        out_shape=jax.ShapeDtypeStruct((M, QW), BF),
        compiler_params=_cparams("parallel", "parallel"),
    )(sinks, arr_a, arr_a, arr_a, arr_a, cache_k, cache_v)


NB_Q_ROWS = TOKEN_BLOCK // GRID_W
NB_WIN_BLOCKS = 3


def _toeplitz_kernel(rpb_ref, o_ref):
    K = rpb_ref.shape[1]
    W2 = GRID_W * GRID_W
    col = lax.broadcasted_iota(jnp.int32, (K, W2), 1)
    cq = col // GRID_W
    ck = col % GRID_W
    dc = jnp.clip(ck - cq + (NA_COLS - 1), 0, 2 * NA_COLS - 2)
    onehot = (lax.broadcasted_iota(jnp.int32, (K, W2), 0) == dc).astype(BF)
    t = rpb_ref[...]
    hi = t.astype(BF)
    r1 = t - hi.astype(F32)
    mid = r1.astype(BF)
    lo = (r1 - mid.astype(F32)).astype(BF)
    val = _dot(hi, onehot) + _dot(mid, onehot) + _dot(lo, onehot)
    c0 = jnp.clip(cq[0:1] - NA_COLS // 2, 0, GRID_W - NA_COLS)
    ok = jnp.logical_and(ck[0:1] >= c0, ck[0:1] < c0 + NA_COLS)
    o_ref[...] = jnp.where(ok, val * LOG2E, -jnp.inf)


def toeplitz_bias(rpb):
    G, H, NR, NC = rpb.shape
    R = G * H * NR
    K = 32
    flat = jnp.pad(rpb.reshape(R, NC), ((0, -R % 8), (0, K - NC)))
    out = pl.pallas_call(
        _toeplitz_kernel,
        out_shape=jax.ShapeDtypeStruct((flat.shape[0], GRID_W * GRID_W), F32),
        compiler_params=pltpu.CompilerParams(vmem_limit_bytes=VMEM_LIMIT_BYTES),
    )(flat)
    return out[:R].reshape(G, H, NR, GRID_W, GRID_W)


def neighbourhood_bias(tiles, T):
    rows = T // GRID_W
    nblk = T // TOKEN_BLOCK
    H = tiles.shape[0]
    masked = jnp.full((H, GRID_W, GRID_W), -jnp.inf, F32)
    out = []
    for blk in (0, 1, nblk - 1):
        w0 = min(max(blk - 1, 0), nblk - NB_WIN_BLOCKS) * NB_Q_ROWS
        qrows = []
        for ql in range(NB_Q_ROWS):
            rq = blk * NB_Q_ROWS + ql
            r0 = min(max(rq - NA_ROWS // 2, 0), rows - NA_ROWS)
            krow = []
            for kl in range(NB_WIN_BLOCKS * NB_Q_ROWS):
                rk = w0 + kl
                krow.append(tiles[:, rk - rq + NA_ROWS - 1] if r0 <= rk < r0 + NA_ROWS else masked)
            qrows.append(jnp.concatenate(krow, axis=-1))
        out.append(jnp.concatenate(qrows, axis=-2))
    return jnp.stack(out, axis=0)


def _nbr_kernel(q_ref, k0_ref, k1_ref, k2_ref, v0_ref, v1_ref, v2_ref, kc_ref, vc_ref, bias_ref, o_ref):
    for h in range(B_HEADS):
        sl = slice(h * HEAD_DIM, (h + 1) * HEAD_DIM)
        q = q_ref[:, sl]
        k_win = jnp.concatenate([k0_ref[:, sl], k1_ref[:, sl], k2_ref[:, sl]], axis=0)
        v_win = jnp.concatenate([v0_ref[:, sl], v1_ref[:, sl], v2_ref[:, sl]], axis=0)
        s_w = _nt_dot(q, k_win) + bias_ref[h]
        s_x = _nt_dot(q, kc_ref[:, sl].astype(BF))
        (e_w, e_x), inv = _exp2_parts([s_w, s_x])
        o = (_dot(e_w.astype(BF), v_win) + _dot(e_x.astype(BF), vc_ref[:, sl].astype(BF))) * inv
        o_ref[:, sl] = o.astype(o_ref.dtype)


def neighbourhood_attention(qb, kb, vb, cache_k, cache_v, bias3, e, N, T):
    M = N * T
    nblk = T // TOKEN_BLOCK
    L = cache_k.shape[2]

    def win(w):
        return lambda n, i: (n * nblk + jnp.clip(i - 1, 0, nblk - NB_WIN_BLOCKS) + w, 0)

    def case(n, i):
        return (jnp.where(i == 0, 0, jnp.where(i == nblk - 1, 2, 1)), 0, 0, 0)

    blk = pl.BlockSpec((TOKEN_BLOCK, B_WIDTH), lambda n, i: (n * nblk + i, 0))
    return pl.pallas_call(
        _nbr_kernel,
        grid=(N, nblk),
        in_specs=[
            blk,
            pl.BlockSpec((TOKEN_BLOCK, B_WIDTH), win(0)),
            pl.BlockSpec((TOKEN_BLOCK, B_WIDTH), win(1)),
            pl.BlockSpec((TOKEN_BLOCK, B_WIDTH), win(2)),
            pl.BlockSpec((TOKEN_BLOCK, B_WIDTH), win(0)),
            pl.BlockSpec((TOKEN_BLOCK, B_WIDTH), win(1)),
            pl.BlockSpec((TOKEN_BLOCK, B_WIDTH), win(2)),
            pl.BlockSpec((None, None, L, B_WIDTH), lambda n, i: (n, e, 0, 0)),
            pl.BlockSpec((None, None, L, B_WIDTH), lambda n, i: (n, e, 0, 0)),
            pl.BlockSpec((None, B_HEADS, TOKEN_BLOCK, NB_WIN_BLOCKS * TOKEN_BLOCK), case),
        ],
        out_specs=blk,
        out_shape=jax.ShapeDtypeStruct((M, B_WIDTH), BF),
        compiler_params=_cparams("parallel", "arbitrary"),
    )(qb, kb, kb, kb, vb, vb, vb, cache_k, cache_v, bias3)


def _diff_dense_kernel(lam_ref, g_ref, q_ref, k_ref, v_ref, o_ref, *, lambda_init):
    lam = _diff_lambda(lam_ref, lambda_init)
    W = C_V_DIM
    for h in range(C_HEADS):
        nums, invs = [], []
        for half in range(2):
            sl = slice(h * W + half * HEAD_DIM, h * W + (half + 1) * HEAD_DIM)
            q = (q_ref[:, sl] * Q_PRESCALE).astype(BF)
            (e,), inv = _exp2_parts([_nt_dot(q, k_ref[:, sl].astype(BF))])
            nums.append(e)
            invs.append(inv)
        a = (nums[0] * invs[0] - nums[1] * (lam * invs[1])).astype(BF)
        o = _dot(a, v_ref[:, h * W:(h + 1) * W].astype(BF))
        of = o * lax.rsqrt(jnp.mean(o * o, -1, keepdims=True) + RMS_EPS)
        o_ref[:, h * W:(h + 1) * W] = ((of * g_ref[...]) * (1.0 - lambda_init)).astype(o_ref.dtype)


def _diff_lambda(lam_ref, lambda_init):
    lp = lam_ref[...]
    return (jnp.exp(jnp.sum(lp[0:1] * lp[1:2], -1, keepdims=True))
            - jnp.exp(jnp.sum(lp[2:3] * lp[3:4], -1, keepdims=True)) + lambda_init)


def _diff_pipe_kernel(lam_ref, g_ref, q_ref, k_ref, v_ref, kc_ref, vc_ref, o_ref,
                      el0, el1, ec0, ec1, cf0, cf1, *, lambda_init):
    i = pl.program_id(2)
    tq = q_ref.shape[0]

    @pl.when(i == 0)
    def _():
        for ref in (el0, el1, ec0, ec1, cf0, cf1):
            ref[...] = jnp.zeros_like(ref)

    def step(e_lat, e_ctx, coef, p_lat, p_ctx, p_coef):
        lam = _diff_lambda(lam_ref, lambda_init)

        def numerators(half):
            sl = slice(half * HEAD_DIM, (half + 1) * HEAD_DIM)
            q = q_ref[:, sl]
            (el, ec), inv = _exp2_parts([_nt_dot(q, k_ref[:, sl]), _nt_dot(q, kc_ref[:, sl].astype(BF))])
            e_lat[half] = el
            e_ctx[half] = ec
            coef[half] = jnp.broadcast_to(inv * lam if half else inv, (tq, HEAD_DIM))

        numerators(0)
        c0 = p_coef[0][:, 0:1]
        c1 = p_coef[1][:, 0:1]
        a = (p_lat[0] * c0 - p_lat[1] * c1).astype(BF)
        ac = (p_ctx[0] * c0 - p_ctx[1] * c1).astype(BF)
        o = _dot(a, v_ref[...]) + _dot(ac, vc_ref[...].astype(BF))
        of = o * lax.rsqrt(jnp.mean(o * o, -1, keepdims=True) + RMS_EPS)
        o_ref[...] = ((of * g_ref[...]) * (1.0 - lambda_init)).astype(o_ref.dtype)
        numerators(1)

    @pl.when(i % 2 == 0)
    def _():
        step(el0, ec0, cf0, el1, ec1, cf1)

    @pl.when(i % 2 == 1)
    def _():
        step(el1, ec1, cf1, el0, ec0, cf0)


def diff_attention_pipelined(q, k, v, lam_p, subln, lambda_init, N, T, ctx, widx):
    M = N * T
    tq = DIFF_QUERY_TILE
    nq = T // tq
    W = C_V_DIM
    L = ctx[0].shape[2]
    return pl.pallas_call(
        functools.partial(_diff_pipe_kernel, lambda_init=lambda_init),
        grid=(N, C_HEADS, nq + 1),
        in_specs=[
            pl.BlockSpec((None, 4, HEAD_DIM), lambda n, h, i: (widx, 0, 0)),
            pl.BlockSpec((None, 1, W), lambda n, h, i: (widx, 0, 0)),
            pl.BlockSpec((tq, W), lambda n, h, i: (n * nq + jnp.minimum(i, nq - 1), h)),
            pl.BlockSpec((T, W), lambda n, h, i: (n, h)),
            pl.BlockSpec((T, W), lambda n, h, i: (n, h)),
            pl.BlockSpec((None, None, L, W), lambda n, h, i: (n, widx, 0, h)),
            pl.BlockSpec((None, None, L, W), lambda n, h, i: (n, widx, 0, h)),
        ],
        out_specs=pl.BlockSpec((tq, W), lambda n, h, i: (n * nq + jnp.maximum(i - 1, 0), h)),
        out_shape=jax.ShapeDtypeStruct((M, C_HEADS * W), BF),
        scratch_shapes=[pltpu.VMEM((2, tq, T), F32)] * 2 + [pltpu.VMEM((2, tq, L), F32)] * 2
        + [pltpu.VMEM((2, tq, HEAD_DIM), F32)] * 2,
        compiler_params=_cparams("parallel", "parallel", "arbitrary"),
    )(lam_p, subln.reshape(subln.shape[0], 1, W), q, k, v, *ctx)


def diff_attention_dense(q, k, v, lam_p, subln, lambda_init, N, T, widx):
    W = C_V_DIM
    row = pl.BlockSpec((T, C_HEADS * W), lambda n: (n, 0))
    return pl.pallas_call(
        functools.partial(_diff_dense_kernel, lambda_init=lambda_init),
        grid=(N,),
        in_specs=[
            pl.BlockSpec((None, 4, HEAD_DIM), lambda n: (widx, 0, 0)),
            pl.BlockSpec((None, 1, W), lambda n: (widx, 0, 0)),
            row, row, row,
        ],
        out_specs=row,
        out_shape=jax.ShapeDtypeStruct((N * T, C_HEADS * W), BF),
        compiler_params=_cparams("parallel"),
    )(lam_p, subln.reshape(subln.shape[0], 1, W), q, k, v)


def _outproj_kernel(*refs, n_in, alpha):
    o_refs = refs[:n_in]
    w_refs = refs[n_in:2 * n_in]
    x_ref, m_ref, g_ref, b_ref, wr_ref, xo_ref, h_ref, aff_ref = refs[2 * n_in:]
    y = None
    for o_r, w_r in zip(o_refs, w_refs):
        t = _dot(o_r[...], w_r[...])
        y = t if y is None else y + t
    z = alpha * x_ref[...] + m_ref[2:3, :] * y
    xn = _layer_norm(z, g_ref[...], b_ref[...])
    xo_ref[...] = xn
    hb = (xn * (1.0 + m_ref[4:5, :]) + m_ref[3:4, :]).astype(BF)
    h_ref[...] = hb
    logits = _dot(hb, wr_ref[...])
    mx = jnp.max(logits, -1, keepdims=True)
    ex = jnp.exp(logits - mx)
    aff_ref[...] = ex / jnp.sum(ex, -1, keepdims=True)


def outproj_ln_router(os_, w_out, widx, x, mods, layer, row0, per_req, T, ln_g, ln_b, w_router, alpha):
    M, D = x.shape
    tm = min(T, OUTPROJ_ROWS)
    E = w_router.shape[-1]
    in_specs, args, koff = [], [], 0
    for o in os_:
        in_specs.append(pl.BlockSpec((tm, o.shape[1]), lambda i: (i, 0)))
        args.append(o)
    for o in os_:
        wk = o.shape[1]
        in_specs.append(pl.BlockSpec((None, wk, D), functools.partial(lambda i, kb: (widx, kb, 0), kb=koff // wk)))
        args.append(w_out)
        koff += wk
    in_specs += [
        pl.BlockSpec((tm, D), lambda i: (i, 0)),
        pl.BlockSpec((None, None, 6, D), lambda i: (layer, row0 + ((i * tm) // T) * per_req, 0, 0)),
        pl.BlockSpec((None, 1, D), lambda i: (layer, 0, 0)),
        pl.BlockSpec((None, 1, D), lambda i: (layer, 0, 0)),
        pl.BlockSpec((None, D, E), lambda i: (layer, 0, 0)),
    ]
    args += [x, mods, ln_g, ln_b, w_router]
    return pl.pallas_call(
        functools.partial(_outproj_kernel, n_in=len(os_), alpha=alpha),
        grid=(M // tm,),
        in_specs=in_specs,
        out_specs=[
            pl.BlockSpec((tm, D), lambda i: (i, 0)),
            pl.BlockSpec((tm, D), lambda i: (i, 0)),
            pl.BlockSpec((tm, E), lambda i: (i, 0)),
        ],
        out_shape=[
            jax.ShapeDtypeStruct((M, D), F32),
            jax.ShapeDtypeStruct((M, D), BF),
            jax.ShapeDtypeStruct((M, E), F32),
        ],
        compiler_params=_cparams("parallel"),
    )(*args)


def _route_kernel(aff_ref, scol_ref, srow_ref, cnt_ref, a3_ref, *, cap, nblk):
    E = N_EXPERTS
    B = aff_ref.shape[0] // nblk
    a = aff_ref[...]
    bits = pltpu.bitcast(a, jnp.int32)

    def search(it, ans):
        cand = ans | jnp.left_shift(jnp.int32(1), 30 - it)
        cnt = jnp.sum((bits >= cand).astype(F32), axis=0, keepdims=True)
        return jnp.where(cnt >= cap, cand, ans)

    thr = lax.fori_loop(0, 31, search, jnp.zeros((1, E), jnp.int32))
    gt = bits > thr
    eq = bits == thr
    need = cap - jnp.sum(gt.astype(F32), axis=0, keepdims=True)

    r = lax.broadcasted_iota(jnp.int32, (B, B), 0)
    c = lax.broadcasted_iota(jnp.int32, (B, B), 1)
    lower = (c < r).astype(BF)
    upper = (r < c).astype(BF)
    eye = (lax.broadcasted_iota(jnp.int32, (E, E), 0) == lax.broadcasted_iota(jnp.int32, (E, E), 1)).astype(BF)

    carry = jnp.zeros((1, E), F32)
    sels = []
    for b in range(nblk):
        sl = slice(b * B, (b + 1) * B)
        eqb = eq[sl].astype(BF)
        pref = _dot(lower, eqb) + carry
        carry = carry + jnp.sum(eqb.astype(F32), axis=0, keepdims=True)
        sels.append(jnp.logical_or(gt[sl], jnp.logical_and(eq[sl], pref < need)))

    carry = jnp.zeros((1, E), F32)
    carry_t = jnp.zeros((E, 1), F32)
    for b in range(nblk):
        sl = slice(b * B, (b + 1) * B)
        selb = sels[b].astype(BF)
        cnt_ref[b:b + 1, :] = carry.astype(jnp.int32)
        pos = _dot(lower, selb) + carry
        scol_ref[sl, :] = jnp.where(sels[b], pos, -1.0)
        carry = carry + jnp.sum(selb.astype(F32), axis=0, keepdims=True)
        sel_t = _nt_dot(eye, selb)
        pos_t = _dot(sel_t.astype(BF), upper) + carry_t
        srow_ref[b] = jnp.where(sel_t > 0.5, pos_t, -1.0)
        carry_t = carry_t + jnp.sum(sel_t, axis=1, keepdims=True)
    cnt_ref[nblk:nblk + 1, :] = carry.astype(jnp.int32)

    hi = a.astype(BF)
    r1 = a - hi.astype(F32)
    mid = r1.astype(BF)
    lo = (r1 - mid.astype(F32)).astype(BF)
    lane = lax.broadcasted_iota(jnp.int32, (E, HEAD_DIM), 1)
    row = lax.broadcasted_iota(jnp.int32, (E, HEAD_DIM), 0)
    a3 = (_dot(hi, (lane == row).astype(BF)) + _dot(mid, (lane == row + E).astype(BF))
          + _dot(lo, (lane == row + 2 * E).astype(BF)))
    a3_ref[...] = a3.astype(BF)


def _gate_from_split(g3, expert):
    lane = lax.broadcasted_iota(jnp.int32, g3.shape, 1)
    pick = jnp.logical_and(lane % N_EXPERTS == expert, lane < 3 * N_EXPERTS)
    return jnp.sum(jnp.where(pick, g3, 0.0), axis=1, keepdims=True)


def route(aff, N, T):
    E = N_EXPERTS
    cap = EC_CAPACITY_FACTOR * T // E
    B = min(T, TOKEN_BLOCK)
    nblk = T // B
    return pl.pallas_call(
        functools.partial(_route_kernel, cap=cap, nblk=nblk),
        grid=(N,),
        in_specs=[pl.BlockSpec((T, E), lambda n: (n, 0))],
        out_specs=[
            pl.BlockSpec((None, T, E), lambda n: (n, 0, 0)),
            pl.BlockSpec((None, nblk, E, B), lambda n: (n, 0, 0, 0)),
            pl.BlockSpec((None, nblk + 1, E), lambda n: (n, 0, 0)),
            pl.BlockSpec((None, T, HEAD_DIM), lambda n: (n, 0, 0)),
        ],
        out_shape=[
            jax.ShapeDtypeStruct((N, T, E), F32),
            jax.ShapeDtypeStruct((N, nblk, E, B), F32),
            jax.ShapeDtypeStruct((N, nblk + 1, E), jnp.int32),
            jax.ShapeDtypeStruct((N, T, HEAD_DIM), BF),
        ],
        compiler_params=_cparams("parallel"),
    )(aff)


def _gather_small_kernel(h_ref, srow_ref, a3_ref, x_all_ref, g_all_ref, x_ref, g_ref, *, cap):
    del x_all_ref, g_all_ref
    E = N_EXPERTS
    S = E * cap
    T = h_ref.shape[0]
    sr = srow_ref[...].astype(BF)
    rep_t = (lax.broadcasted_iota(jnp.int32, (S, E), 0) // cap == lax.broadcasted_iota(jnp.int32, (S, E), 1)).astype(BF)
    slot_of_row = (lax.broadcasted_iota(jnp.int32, (S, 1), 0) % cap).astype(F32)
    onehot = (_dot(rep_t, sr) == slot_of_row).astype(BF)
    x = _dot(onehot, h_ref[...])
    for e in range(E):
        x_ref[e] = x[e * cap:(e + 1) * cap].astype(x_ref.dtype)
    gate = _gate_from_split(_dot(onehot, a3_ref[...]), lax.broadcasted_iota(jnp.int32, (S, 1), 0) // cap)
    gb = jnp.broadcast_to(gate, (S, HEAD_DIM))
    for e in range(E):
        g_ref[e] = gb[e * cap:(e + 1) * cap]


def gather_small(h, srow, a3, x_all, g_all, row0, N, T):
    E = N_EXPERTS
    cap = EC_CAPACITY_FACTOR * T // E
    D = h.shape[1]
    assert row0 % cap == 0
    rb0 = row0 // cap
    return pl.pallas_call(
        functools.partial(_gather_small_kernel, cap=cap),
        grid=(N,),
        in_specs=[
            pl.BlockSpec((T, D), lambda n: (n, 0)),
            pl.BlockSpec((None, None, E, T), lambda n: (n, 0, 0, 0)),
            pl.BlockSpec((None, T, HEAD_DIM), lambda n: (n, 0, 0)),
            pl.BlockSpec(memory_space=pl.ANY),
            pl.BlockSpec(memory_space=pl.ANY),
        ],
        out_specs=[
            pl.BlockSpec((E, cap, D), lambda n: (0, rb0 + n, 0)),
            pl.BlockSpec((E, cap, HEAD_DIM), lambda n: (0, rb0 + n, 0)),
        ],
        out_shape=[
            jax.ShapeDtypeStruct(x_all.shape, x_all.dtype),
            jax.ShapeDtypeStruct(g_all.shape, g_all.dtype),
        ],
        input_output_aliases={3: 0, 4: 1},
        compiler_params=_cparams("parallel"),
    )(h, srow, a3, x_all, g_all)


GATHER_COLS = 1024


def _gather_big_kernel(cnt_ref, h_ref, srow_ref, a3_ref, x_ref, g_ref, *, cap, nblk):
    E = N_EXPERTS
    W = min(SLOT_CHUNK, cap)
    n = pl.program_id(0)
    d = pl.program_id(1)
    b = pl.program_id(2)

    @pl.when(b == 0)
    def _():
        x_ref[...] = jnp.zeros_like(x_ref)

    @pl.when(jnp.logical_and(b == 0, d == 0))
    def _():
        g_ref[...] = jnp.zeros_like(g_ref)

    row = lax.broadcasted_iota(jnp.int32, (W, 1), 0)
    starts, ends, pieces = [], [], []
    for e in range(E):
        base = (n * E + e) * (nblk + 1) + b
        ws = pl.multiple_of(jnp.minimum((cnt_ref[base] // 16) * 16, cap - W), 16)
        pieces.append((srow_ref[e:e + 1, :] == (ws + row).astype(F32)).astype(BF))
        starts.append(ws)
        ends.append(cnt_ref[base + 1])
    onehot = jnp.concatenate(pieces, axis=0)
    prod = _dot(onehot, h_ref[...])
    for e in range(E):
        x_ref[e, pl.ds(starts[e], W), :] += prod[e * W:(e + 1) * W].astype(x_ref.dtype)

    @pl.when(d == 0)
    def _():
        split = _dot(onehot, a3_ref[...])
        for e in range(E):
            g_ref[e, pl.ds(starts[e], W), :] += split[e * W:(e + 1) * W]

    for e in range(E):
        def more(k, carry, e=e):
            first = starts[e] + k * W
            ws = pl.multiple_of(jnp.minimum(first, cap - W), 16)
            ids = ws + row
            piece = jnp.logical_and(srow_ref[e:e + 1, :] == ids.astype(F32), ids >= first).astype(BF)
            x_ref[e, pl.ds(ws, W), :] += _dot(piece, h_ref[...]).astype(x_ref.dtype)

            @pl.when(d == 0)
            def _():
                g_ref[e, pl.ds(ws, W), :] += _dot(piece, a3_ref[...])

            return carry

        lax.fori_loop(1, (ends[e] - starts[e] + W - 1) // W, more, 0)

    @pl.when(jnp.logical_and(b == nblk - 1, d == 0))
    def _():
        for e in range(E):
            g_ref[e] = jnp.broadcast_to(_gate_from_split(g_ref[e], e), (cap, HEAD_DIM))


def gather_big(h, srow, cnt_flat, a3, rows_total, N, T):
    E = N_EXPERTS
    cap = EC_CAPACITY_FACTOR * T // E
    D = h.shape[1]
    nblk = T // TOKEN_BLOCK
    Dc = min(D, GATHER_COLS)
    assert cap % 16 == 0
    return pl.pallas_call(
        functools.partial(_gather_big_kernel, cap=cap, nblk=nblk),
        grid_spec=pltpu.PrefetchScalarGridSpec(
            num_scalar_prefetch=1,
            grid=(N, D // Dc, nblk),
            in_specs=[
                pl.BlockSpec((TOKEN_BLOCK, Dc), lambda n, d, b, c: (n * nblk + b, d)),
                pl.BlockSpec((None, None, E, TOKEN_BLOCK), lambda n, d, b, c: (n, b, 0, 0)),
                pl.BlockSpec((None, TOKEN_BLOCK, HEAD_DIM), lambda n, d, b, c: (n, b, 0)),
            ],
            out_specs=[
                pl.BlockSpec((E, cap, Dc), lambda n, d, b, c: (0, n, d)),
                pl.BlockSpec((E, cap, HEAD_DIM), lambda n, d, b, c: (0, n, 0)),
            ],
        ),
        out_shape=[
            jax.ShapeDtypeStruct((E, rows_total, D), BF),
            jax.ShapeDtypeStruct((E, rows_total, HEAD_DIM), F32),
        ],
        compiler_params=_cparams("parallel", "arbitrary", "arbitrary"),
    )(cnt_flat, h, srow, a3)


def _gate_up_kernel(x_ref, wg_ref, wu_ref, o_ref):
    x = x_ref[...]
    a = _dot(x, wg_ref[...].astype(BF))
    u = _dot(x, wu_ref[...].astype(BF))
    o_ref[...] = ((a * jax.nn.sigmoid(a)) * u).astype(o_ref.dtype)


def expert_gate_up(x, w_gate, w_up, layer):
    E, R, D = x.shape
    F = w_gate.shape[-1]
    tf = min(F, EXPERT_UP_COLS)
    return pl.pallas_call(
        _gate_up_kernel,
        grid=(E, F // tf),
        in_specs=[
            pl.BlockSpec((None, R, D), lambda e, f: (e, 0, 0)),
            pl.BlockSpec((None, None, D, tf), lambda e, f: (layer, e, 0, f)),
            pl.BlockSpec((None, None, D, tf), lambda e, f: (layer, e, 0, f)),
        ],
        out_specs=pl.BlockSpec((None, R, tf), lambda e, f: (e, 0, f)),
        out_shape=jax.ShapeDtypeStruct((E, R, F), BF),
        compiler_params=_cparams("parallel", "arbitrary"),
    )(x, w_gate, w_up)


def _down_kernel(h_ref, w_ref, g_ref, o_ref):
    o_ref[...] = (_dot(h_ref[...], w_ref[...].astype(BF)) * g_ref[:, 0:1]).astype(o_ref.dtype)


def expert_down(h, w_down, gate, layer):
    E, R, F = h.shape
    D = w_down.shape[-1]
    td = min(D, EXPERT_DOWN_COLS)
    return pl.pallas_call(
        _down_kernel,
        grid=(E, D // td),
        in_specs=[
            pl.BlockSpec((None, R, F), lambda e, d: (e, 0, 0)),
            pl.BlockSpec((None, None, F, td), lambda e, d: (layer, e, 0, d)),
            pl.BlockSpec((None, R, HEAD_DIM), lambda e, d: (e, 0, 0)),
        ],
        out_specs=pl.BlockSpec((None, R, td), lambda e, d: (e, 0, d)),
        out_shape=jax.ShapeDtypeStruct((E, R, D), BF),
        compiler_params=_cparams("parallel", "arbitrary"),
    )(h, w_down, gate)


def _residual_epilogue(f, x_ref, m_ref, mn_ref, g_ref, b_ref, xo_ref, h_ref, alpha):
    z = alpha * x_ref[...] + m_ref[5:6, :] * f
    xn = _layer_norm(z, g_ref[...], b_ref[...])
    xo_ref[...] = xn
    h_ref[...] = (xn * (1.0 + mn_ref[1:2, :]) + mn_ref[0:1, :]).astype(h_ref.dtype)


def _scatter_small_kernel(y_ref, scol_ref, x_ref, m_ref, mn_ref, g_ref, b_ref, xo_ref, h_ref, *, cap, alpha):
    E = N_EXPERTS
    S = E * cap
    sc = scol_ref[...].astype(BF)
    rep = (lax.broadcasted_iota(jnp.int32, (E, S), 0) == lax.broadcasted_iota(jnp.int32, (E, S), 1) // cap).astype(BF)
    slot_of_col = (lax.broadcasted_iota(jnp.int32, (1, S), 1) % cap).astype(F32)
    onehot_t = (_dot(sc, rep) == slot_of_col).astype(BF)
    y = jnp.concatenate([y_ref[e] for e in range(E)], axis=0)
    f = _dot(onehot_t, y)
    _residual_epilogue(f, x_ref, m_ref, mn_ref, g_ref, b_ref, xo_ref, h_ref, alpha)


def scatter_small(y, y_row0, scol, x, mods, layer, next_layer, row0, per_req, ln_g, ln_b, N, T, alpha):
    E = N_EXPERTS
    cap = EC_CAPACITY_FACTOR * T // E
    D = x.shape[1]
    assert y_row0 % cap == 0
    return pl.pallas_call(
        functools.partial(_scatter_small_kernel, cap=cap, alpha=alpha),
        grid=(N,),
        in_specs=[
            pl.BlockSpec((E, cap, D), lambda n: (0, y_row0 // cap + n, 0)),
            pl.BlockSpec((None, T, E), lambda n: (n, 0, 0)),
            pl.BlockSpec((T, D), lambda n: (n, 0)),
            pl.BlockSpec((None, None, 6, D), lambda n: (layer, row0 + n * per_req, 0, 0)),
            pl.BlockSpec((None, None, 6, D), lambda n: (next_layer, row0 + n * per_req, 0, 0)),
            pl.BlockSpec((None, 1, D), lambda n: (layer, 0, 0)),
            pl.BlockSpec((None, 1, D), lambda n: (layer, 0, 0)),
        ],
        out_specs=[pl.BlockSpec((T, D), lambda n: (n, 0)), pl.BlockSpec((T, D), lambda n: (n, 0))],
        out_shape=[jax.ShapeDtypeStruct((N * T, D), F32), jax.ShapeDtypeStruct((N * T, D), BF)],
        compiler_params=_cparams("parallel"),
    )(y, scol, x, mods, mods, ln_g, ln_b)


def _scatter_big_kernel(cnt_ref, y_ref, scol_ref, x_ref, m_ref, mn_ref, g_ref, b_ref, xo_ref, h_ref,
                        *, nblk, tiles, alpha):
    acc = xo_ref
    E = N_EXPERTS
    C = SLOT_CHUNK
    B = TOKEN_BLOCK
    G = y_ref.shape[0]
    cap = y_ref.shape[1]
    WIN = min(C, cap)
    n = pl.program_id(0)
    tb = pl.program_id(1)
    g = pl.program_id(2)

    @pl.when(g == 0)
    def _():
        acc[...] = jnp.zeros_like(acc)

    lane = lax.broadcasted_iota(jnp.int32, (B, E), 1)
    for bl in range(tiles):
        blk = tb * tiles + bl
        sc = scol_ref[bl * B:(bl + 1) * B, :]
        hits, wins = [], []
        for gi in range(G):
            e = g * G + gi
            base = (n * E + e) * (nblk + 1)
            lo = cnt_ref[base + blk]
            hi = cnt_ref[base + blk + 1]
            scol = jnp.sum(jnp.where(lane == e, sc, 0.0), axis=1, keepdims=True)
            ws = pl.multiple_of(jnp.minimum((lo // 16) * 16, cap - WIN), 16)
            col = lax.broadcasted_iota(jnp.int32, (1, WIN), 1)
            hits.append((scol == (ws + col).astype(F32)).astype(BF))
            wins.append(y_ref[gi, pl.ds(ws, WIN), :])

            def more(k, carry, gi=gi, scol=scol, ws=ws, col=col):
                first = ws + k * WIN
                ks = pl.multiple_of(jnp.minimum(first, cap - WIN), 16)
                ids = ks + col
                hit = jnp.logical_and(scol == ids.astype(F32), ids >= first)
                acc[bl * B:(bl + 1) * B, :] += _dot(hit.astype(BF), y_ref[gi, pl.ds(ks, WIN), :])
                return carry

            lax.fori_loop(1, (hi - ws + WIN - 1) // WIN, more, 0)
        f = None
        for p in range(0, G, 2):
            d = _dot(jnp.concatenate(hits[p:p + 2], axis=1), jnp.concatenate(wins[p:p + 2], axis=0))
            f = d if f is None else f + d
        acc[bl * B:(bl + 1) * B, :] += f

    @pl.when(g == pl.num_programs(2) - 1)
    def _():
        for r in range(0, tiles * B, B):
            rows = slice(r, r + B)
            z = alpha * x_ref[rows, :] + m_ref[5:6, :] * acc[rows, :]
            xn = _layer_norm(z, g_ref[...], b_ref[...])
            xo_ref[rows, :] = xn
            h_ref[rows, :] = (xn * (1.0 + mn_ref[1:2, :]) + mn_ref[0:1, :]).astype(h_ref.dtype)


SCATTER_EXPERT_GROUP = 2


def scatter_big(y, y_row0, scol, cnt_flat, x, mods, layer, next_layer, row0, per_req, ln_g, ln_b, N, T, alpha):
    E = N_EXPERTS
    cap = EC_CAPACITY_FACTOR * T // E
    D = x.shape[1]
    nblk = T // TOKEN_BLOCK
    tiles = max(t for t in (1, 2, 4) if nblk % t == 0)
    tt = tiles * TOKEN_BLOCK
    ntt = T // tt
    G = SCATTER_EXPERT_GROUP
    assert y_row0 % cap == 0 and cap % 16 == 0
    return pl.pallas_call(
        functools.partial(_scatter_big_kernel, nblk=nblk, tiles=tiles, alpha=alpha),
        grid_spec=pltpu.PrefetchScalarGridSpec(
            num_scalar_prefetch=1,
            grid=(N, ntt, E // G),
            in_specs=[
                pl.BlockSpec((G, cap, D), lambda n, t, e, c: (e, y_row0 // cap + n, 0)),
                pl.BlockSpec((None, tt, E), lambda n, t, e, c: (n, t, 0)),
                pl.BlockSpec((tt, D), lambda n, t, e, c: (n * ntt + t, 0), pipeline_mode=pl.Buffered(1)),
                pl.BlockSpec((None, None, 6, D), lambda n, t, e, c: (layer, row0 + n * per_req, 0, 0)),
                pl.BlockSpec((None, None, 6, D), lambda n, t, e, c: (next_layer, row0 + n * per_req, 0, 0)),
                pl.BlockSpec((None, 1, D), lambda n, t, e, c: (layer, 0, 0)),
                pl.BlockSpec((None, 1, D), lambda n, t, e, c: (layer, 0, 0)),
            ],
            out_specs=[
                pl.BlockSpec((tt, D), lambda n, t, e, c: (n * ntt + t, 0)),
                pl.BlockSpec((tt, D), lambda n, t, e, c: (n * ntt + t, 0)),
            ],
        ),
        out_shape=[jax.ShapeDtypeStruct((N * T, D), F32), jax.ShapeDtypeStruct((N * T, D), BF)],
        compiler_params=_cparams("parallel", "parallel", "arbitrary"),
    )(cnt_flat, y, scol, x, mods, mods, ln_g, ln_b)


def moe_layer(groups, mods, layer, next_layer, ln_g, ln_b, w_gate, w_up, w_down, alpha):
    E = N_EXPERTS
    rows_total = sum(gr["N"] * (EC_CAPACITY_FACTOR * gr["T"] // E) for gr in groups)
    routed, row0s, rows, x_all, g_all = [], [], 0, None, None
    for gr in groups:
        N, T = gr["N"], gr["T"]
        scol, srow, cnt, a3 = route(gr["aff"], N, T)
        if T <= TOKEN_BLOCK:
            assert x_all is not None
            cnt_flat = None
            x_all, g_all = gather_small(gr["h"], srow, a3, x_all, g_all, rows, N, T)
        else:
            assert x_all is None
            cnt_flat = jnp.transpose(cnt, (0, 2, 1)).reshape(-1)
            x_all, g_all = gather_big(gr["h"], srow, cnt_flat, a3, rows_total, N, T)
        routed.append((scol, cnt_flat))
        row0s.append(rows)
        rows += N * (EC_CAPACITY_FACTOR * T // E)
    hm = expert_gate_up(x_all, w_gate, w_up, layer)
    y = expert_down(hm, w_down, g_all, layer)
    out = []
    for gr, (scol, cnt_flat), y_row0 in zip(groups, routed, row0s):
        tail = (gr["x"], mods, layer, next_layer, gr["row0"], gr["per_req"], ln_g, ln_b, gr["N"], gr["T"], alpha)
        if cnt_flat is None:
            out.append(scatter_small(y, y_row0, scol, *tail))
        else:
            out.append(scatter_big(y, y_row0, scol, cnt_flat, *tail))
    return out


def kernel(x_prompt, x_sample, cache_a_k, cache_a_v, cache_b_k, cache_b_v, cache_c_k, cache_c_v, c, c_ctx, w_mod, b_mod, w_in_even, w_out_even, a_sink, na_rpb, w_in_odd, w_out_odd, diff_lambda, diff_subln, ln_g, ln_b, w_router, w_gate, w_up, w_down):
    NP, TP, D = x_prompt.shape
    NS, TS, _ = x_sample.shape
    depth = w_mod.shape[0]
    L = cache_a_k.shape[2]
    alpha = (2 * depth) ** 0.25
    n_even = (depth + 1) // 2
    n_odd = depth // 2

    rows = 1 + NS
    rpad = -rows % 8
    cond = jnp.concatenate([c_ctx[None], c, jnp.zeros((rpad, D), F32)], axis=0)
    mods = adaln_all(cond, w_mod, b_mod).reshape(depth, rows + rpad, 6, D)

    rope = rope_tables(TS)
    sinks = a_sink.reshape(-1)
    rpb_tiles = toeplitz_bias(na_rpb)
    w_out_even_bf = w_out_even.astype(BF)
    w_out_odd_bf = w_out_odd.astype(BF)
    w_router_bf = w_router.astype(BF)
    ln_g4 = ln_g.reshape(depth, 2, 1, D)
    ln_b4 = ln_b.reshape(depth, 2, 1, D)
    ca_k = cache_a_k.reshape(NS, n_even, L, A_KV_HEADS * HEAD_DIM)
    ca_v = cache_a_v.reshape(NS, n_even, L, A_KV_HEADS * HEAD_DIM)
    cb_k = cache_b_k.reshape(NS, n_even, L, B_WIDTH)
    cb_v = cache_b_v.reshape(NS, n_even, L, B_WIDTH)
    cc_k = cache_c_k.reshape(NS, n_odd, L, C_HEADS * 2 * HEAD_DIM)
    cc_v = cache_c_v.reshape(NS, n_odd, L, C_HEADS * C_V_DIM)

    xp = x_prompt.reshape(NP * TP, D)
    xs = x_sample.reshape(NS * TS, D)
    hp = modulate_rows(xp, mods, 0, 0, 0, TP)
    hs = modulate_rows(xs, mods, 0, 1, 1, TS)

    new_a_k, new_a_v, new_b_k, new_b_v, new_c_k, new_c_v = [], [], [], [], [], []
    QA = A_Q_HEADS * HEAD_DIM
    KA = A_KV_HEADS * HEAD_DIM
    for l in range(depth):
        if l % 2 == 0:
            e = l // 2
            b0 = A_WIDTH
            pa = project(hp, w_in_even, e, 0, A_WIDTH, 0, None, TP, F32)
            pqb = project(hp, w_in_even, e, b0, B_WIDTH, 0, None, TP, F32)
            pkb = project(hp, w_in_even, e, b0 + B_WIDTH, B_WIDTH, 0, None, TP, F32)
            pvb = project(hp, w_in_even, e, b0 + 2 * B_WIDTH, B_WIDTH, 0, None, TP, F32)
            new_a_k.append(pa[:, QA:QA + KA].reshape(NP, TP, A_KV_HEADS, HEAD_DIM))
            new_a_v.append(pa[:, QA + KA:QA + 2 * KA].reshape(NP, TP, A_KV_HEADS, HEAD_DIM))
            new_b_k.append(pkb.reshape(NP, TP, B_HEADS, HEAD_DIM))
            new_b_v.append(pvb.reshape(NP, TP, B_HEADS, HEAD_DIM))
            op = [ctx_even_attention(pa, pqb, pkb, pvb, sinks, e, NP, TP)]

            sa = project(hs, w_in_even, e, 0, A_WIDTH, A_Q_HEADS + A_KV_HEADS, rope, TS, BF, A_Q_HEADS)
            sqb = project(hs, w_in_even, e, b0, B_WIDTH, 0, rope, TS, BF, B_HEADS)
            skb = project(hs, w_in_even, e, b0 + B_WIDTH, B_WIDTH, 0, rope, TS, BF)
            svb = project(hs, w_in_even, e, b0 + 2 * B_WIDTH, B_WIDTH, 0, rope, TS, BF)
            oa = window_attention(sa, ca_k, ca_v, sinks, e, NS, TS)
            bias3 = neighbourhood_bias(rpb_tiles[e], TS)
            ob = neighbourhood_attention(sqb, skb, svb, cb_k, cb_v, bias3, e, NS, TS)
            os_ = [oa, ob]
            w_out, widx = w_out_even_bf, e
        else:
            o = l // 2
            lambda_init = 0.8 - 0.6 * math.exp(-0.3 * l)
            W = C_HEADS * C_V_DIM
            pq = project(hp, w_in_odd, o, 0, W, 0, None, TP, F32)
            pk = project(hp, w_in_odd, o, W, W, 0, None, TP, F32)
            pv = project(hp, w_in_odd, o, 2 * W, W, 0, None, TP, F32)
            new_c_k.append(pk.reshape(NP, TP, C_HEADS, 2, HEAD_DIM))
            new_c_v.append(pv.reshape(NP, TP, C_HEADS, C_V_DIM))
            op = [diff_attention_dense(pq, pk, pv, diff_lambda, diff_subln, lambda_init, NP, TP, o)]

            sq = project(hs, w_in_odd, o, 0, W, 2 * C_HEADS, rope, TS, BF, 2 * C_HEADS)
            sk = project(hs, w_in_odd, o, W, W, 2 * C_HEADS, rope, TS, BF)
            sv = project(hs, w_in_odd, o, 2 * W, W, 0, rope, TS, BF)
            os_ = [diff_attention_pipelined(sq, sk, sv, diff_lambda, diff_subln, lambda_init, NS, TS, (cc_k, cc_v), o)]
            w_out, widx = w_out_odd_bf, o

        lg1, lb1 = ln_g4[:, 0], ln_b4[:, 0]
        lg2, lb2 = ln_g4[:, 1], ln_b4[:, 1]
        xp, hp, affp = outproj_ln_router(op, w_out, widx, xp, mods, l, 0, 0, TP, lg1, lb1, w_router_bf, alpha)
        xs, hs, affs = outproj_ln_router(os_, w_out, widx, xs, mods, l, 1, 1, TS, lg1, lb1, w_router_bf, alpha)
        nl = min(l + 1, depth - 1)
        groups = [dict(aff=affs, h=hs, x=xs, N=NS, T=TS, row0=1, per_req=1),
                  dict(aff=affp, h=hp, x=xp, N=NP, T=TP, row0=0, per_req=0)]
        (xs, hs), (xp, hp) = moe_layer(groups, mods, l, nl, lg2, lb2, w_gate, w_up, w_down, alpha)

    return (xp.reshape(NP, TP, D), xs.reshape(NS, TS, D),
            jnp.stack(new_a_k, axis=1), jnp.stack(new_a_v, axis=1),
            jnp.stack(new_b_k, axis=1), jnp.stack(new_b_v, axis=1),
            jnp.stack(new_c_k, axis=1), jnp.stack(new_c_v, axis=1))
```

```python
import functools
import math

import jax
import jax.numpy as jnp
from jax import lax
from jax.experimental import pallas as pl
from jax.experimental.pallas import tpu as pltpu

BF = jnp.bfloat16
F32 = jnp.float32

GRID_W = 64
HEAD_DIM = 128
A_Q_HEADS = 8
A_KV_HEADS = 2
A_GROUP = A_Q_HEADS // A_KV_HEADS
A_BLOCK = 128
B_HEADS = 8
NA_ROWS = 8
NA_COLS = 16
C_HEADS = 8
C_V_DIM = 2 * HEAD_DIM
N_EXPERTS = 16
EC_CAPACITY_FACTOR = 2
ROPE_BASE = 10000.0
LN_EPS = 1e-5
RMS_EPS = 1e-5
ATTN_SCALE = HEAD_DIM ** -0.5
A_WIDTH = (A_Q_HEADS + 2 * A_KV_HEADS) * HEAD_DIM
B_WIDTH = B_HEADS * HEAD_DIM

VMEM_LIMIT_BYTES = 52 * 1024 * 1024
TOKEN_BLOCK = 256
SLOT_CHUNK = 128


def _cparams(*sem):
    return pltpu.CompilerParams(dimension_semantics=sem, vmem_limit_bytes=VMEM_LIMIT_BYTES)


def _nt_dot(a, b):
    return lax.dot_general(a, b, (((1,), (1,)), ((), ())), preferred_element_type=F32)


def _dot(a, b):
    return jnp.dot(a, b, preferred_element_type=F32)


def _layer_norm(z, g, b):
    mu = jnp.mean(z, -1, keepdims=True)
    d = z - mu
    var = jnp.mean(d * d, -1, keepdims=True)
    return d * lax.rsqrt(var + LN_EPS) * g + b


def _adaln_kernel(c_ref, w_ref, b_ref, o_ref):
    c = c_ref[...]
    s = (c * jax.nn.sigmoid(c)).astype(BF)
    o_ref[...] = _dot(s, w_ref[...].astype(BF)) + b_ref[...]


def adaln_all(cond, w_mod, b_mod):
    L, D, D6 = w_mod.shape
    R = cond.shape[0]
    tn = D // 2
    return pl.pallas_call(
        _adaln_kernel,
        grid=(L, D6 // tn),
        in_specs=[
            pl.BlockSpec((R, D), lambda l, j: (0, 0)),
            pl.BlockSpec((None, D, tn), lambda l, j: (l, 0, j)),
            pl.BlockSpec((None, 1, tn), lambda l, j: (l, 0, j)),
        ],
        out_specs=pl.BlockSpec((None, R, tn), lambda l, j: (l, 0, j)),
        out_shape=jax.ShapeDtypeStruct((L, R, D6), F32),
        compiler_params=_cparams("parallel", "parallel"),
    )(cond, w_mod, b_mod.reshape(L, 1, D6))


def _modulate_kernel(x_ref, m_ref, o_ref):
    o_ref[...] = (x_ref[...] * (1.0 + m_ref[1:2, :]) + m_ref[0:1, :]).astype(o_ref.dtype)


def modulate_rows(x, mods, layer, row0, per_req, T):
    M, D = x.shape
    tt = min(T, 512)
    return pl.pallas_call(
        _modulate_kernel,
        grid=(M // tt,),
        in_specs=[
            pl.BlockSpec((tt, D), lambda i: (i, 0)),
            pl.BlockSpec((None, None, 6, D), lambda i: (layer, row0 + ((i * tt) // T) * per_req, 0, 0)),
        ],
        out_specs=pl.BlockSpec((tt, D), lambda i: (i, 0)),
        out_shape=jax.ShapeDtypeStruct((M, D), BF),
        compiler_params=_cparams("parallel"),
    )(x, mods)


def _swap32(x):
    lane = lax.broadcasted_iota(jnp.int32, (1, HEAD_DIM), 1)
    return jnp.where((lane % 64) < 32, pltpu.roll(x, HEAD_DIM - 32, 1), pltpu.roll(x, 32, 1))


def _proj_kernel(*refs, n_rope_heads, n_scaled_heads, heads_per_tile):
    if n_rope_heads:
        h_ref, w_ref, cos_ref, sin_ref, o_ref, wbf = refs
    else:
        h_ref, w_ref, o_ref, wbf = refs
    j = pl.program_id(0)
    i = pl.program_id(1)

    @pl.when(i == 0)
    def _():
        wbf[...] = w_ref[...].astype(BF)

    acc = _dot(h_ref[...], wbf[...])
    bounds = sorted({0, min(n_rope_heads, n_scaled_heads), n_rope_heads, n_scaled_heads, 1 << 30})
    for hh in range(heads_per_tile):
        sl = slice(hh * HEAD_DIM, (hh + 1) * HEAD_DIM)
        x = acc[:, sl]
        head = j * heads_per_tile + hh
        for lo, hi in zip(bounds[:-1], bounds[1:]):
            rot = lo < n_rope_heads
            scale = Q_PRESCALE if lo < n_scaled_heads else None

            def emit(x=x, sl=sl, rot=rot, scale=scale):
                y = x * cos_ref[...] + _swap32(x) * sin_ref[...] if rot else x
                o_ref[:, sl] = (y if scale is None else y * scale).astype(o_ref.dtype)

            if len(bounds) == 2:
                emit()
            else:
                pl.when(jnp.logical_and(head >= lo, head < hi))(emit)


def project(h, w, widx, col0, width, n_rope_heads, rope, T, out_dtype, n_scaled_heads=0):
    M, D = h.shape
    heads = width // HEAD_DIM
    hpt = max(k for k in range(1, 9) if heads % k == 0 and col0 % (k * HEAD_DIM) == 0)
    tn = hpt * HEAD_DIM
    tm = min(T if n_rope_heads else M, 1024)
    tpb = max(T // tm, 1)
    in_specs = [
        pl.BlockSpec((tm, D), lambda j, i: (i, 0)),
        pl.BlockSpec((None, D, tn), lambda j, i: (widx, 0, col0 // tn + j)),
    ]
    args = [h, w]
    if n_rope_heads:
        in_specs += [pl.BlockSpec((tm, HEAD_DIM), lambda j, i: (i % tpb, 0))] * 2
        args += list(rope)
    return pl.pallas_call(
        functools.partial(_proj_kernel, n_rope_heads=n_rope_heads, n_scaled_heads=n_scaled_heads, heads_per_tile=hpt),
        grid=(width // tn, M // tm),
        in_specs=in_specs,
        out_specs=pl.BlockSpec((tm, tn), lambda j, i: (i, j)),
        out_shape=jax.ShapeDtypeStruct((M, width), out_dtype),
        scratch_shapes=[pltpu.VMEM((D, tn), BF)],
        compiler_params=_cparams("parallel", "arbitrary"),
    )(*args)


def rope_tables(T):
    t = jnp.arange(T)
    row = (t // GRID_W).astype(F32)
    col = (t % GRID_W).astype(F32)
    axis_dim = HEAD_DIM // 2
    inv_freq = ROPE_BASE ** (-jnp.arange(0, axis_dim, 2, dtype=F32) / axis_dim)
    ar = row[:, None] * inv_freq
    ac = col[:, None] * inv_freq
    cos = jnp.concatenate([jnp.cos(ar), jnp.cos(ar), jnp.cos(ac), jnp.cos(ac)], -1)
    sin = jnp.concatenate([-jnp.sin(ar), jnp.sin(ar), -jnp.sin(ac), jnp.sin(ac)], -1)
    return cos, sin


LOG2E = math.log2(math.e)
Q_PRESCALE = ATTN_SCALE * LOG2E


def _exp2_parts(parts, sink=None):
    m = None
    for s in parts:
        mi = jnp.max(s, -1, keepdims=True)
        m = mi if m is None else jnp.maximum(m, mi)
    if sink is not None:
        m = jnp.maximum(m, sink)
    es = [jnp.exp2(s - m) for s in parts]
    den = None
    for e in es:
        di = jnp.sum(e, -1, keepdims=True)
        den = di if den is None else den + di
    if sink is not None:
        den = den + jnp.exp2(sink - m)
    return es, 1.0 / den


def _softmax_parts(parts, sink=None):
    m = None
    for s in parts:
        mi = jnp.max(s, -1, keepdims=True)
        m = mi if m is None else jnp.maximum(m, mi)
    if sink is not None:
        m = jnp.maximum(m, sink)
    es = [jnp.exp(s - m) for s in parts]
    den = None
    for e in es:
        di = jnp.sum(e, -1, keepdims=True)
        den = di if den is None else den + di
    if sink is not None:
        den = den + jnp.exp(sink - m)
    inv = 1.0 / den
    return [e * inv for e in es]


def _ctx_even_kernel(sink_ref, a_ref, qb_ref, kb_ref, vb_ref, o_ref, *, sink_base):
    T = a_ref.shape[0]
    for kv in range(A_KV_HEADS):
        k = a_ref[:, (A_Q_HEADS + kv) * HEAD_DIM:(A_Q_HEADS + kv + 1) * HEAD_DIM].astype(BF)
        v = a_ref[:, (A_Q_HEADS + A_KV_HEADS + kv) * HEAD_DIM:(A_Q_HEADS + A_KV_HEADS + kv + 1) * HEAD_DIM].astype(BF)
        for g in range(A_GROUP):
            hq = kv * A_GROUP + g
            q = a_ref[:, hq * HEAD_DIM:(hq + 1) * HEAD_DIM].astype(BF)
            s = _nt_dot(q, k) * ATTN_SCALE
            sink = jnp.full((T, 1), sink_ref[sink_base + hq], F32)
            (p,) = _softmax_parts([s], sink)
            o_ref[:, hq * HEAD_DIM:(hq + 1) * HEAD_DIM] = _dot(p.astype(BF), v).astype(o_ref.dtype)
    for hb in range(B_HEADS):
        sl = slice(hb * HEAD_DIM, (hb + 1) * HEAD_DIM)
        s = _nt_dot(qb_ref[:, sl].astype(BF), kb_ref[:, sl].astype(BF)) * ATTN_SCALE
        (p,) = _softmax_parts([s])
        o_ref[:, A_Q_HEADS * HEAD_DIM + hb * HEAD_DIM:A_Q_HEADS * HEAD_DIM + (hb + 1) * HEAD_DIM] = _dot(
            p.astype(BF), vb_ref[:, sl].astype(BF)).astype(o_ref.dtype)


def ctx_even_attention(arr_a, qb, kb, vb, sinks, e, N, T):
    M = N * T
    return pl.pallas_call(
        functools.partial(_ctx_even_kernel, sink_base=e * A_Q_HEADS),
        grid=(N,),
        in_specs=[
            pl.BlockSpec(memory_space=pltpu.SMEM),
            pl.BlockSpec((T, A_WIDTH), lambda n: (n, 0)),
            pl.BlockSpec((T, B_WIDTH), lambda n: (n, 0)),
            pl.BlockSpec((T, B_WIDTH), lambda n: (n, 0)),
            pl.BlockSpec((T, B_WIDTH), lambda n: (n, 0)),
        ],
        out_specs=pl.BlockSpec((T, A_Q_HEADS * HEAD_DIM + B_WIDTH), lambda n: (n, 0)),
        out_shape=jax.ShapeDtypeStruct((M, A_Q_HEADS * HEAD_DIM + B_WIDTH), BF),
        compiler_params=_cparams("parallel"),
    )(sinks, arr_a, qb, kb, vb)


def _win_kernel(sink_ref, q_ref, kvp_ref, kvc_ref, kvn_ref, kc_ref, vc_ref, o_ref, *, sink_base, nb):
    i = pl.program_id(1)
    R = A_GROUP * A_BLOCK
    qq = lax.broadcasted_iota(jnp.int32, (R, 1), 0) % A_BLOCK
    kk = lax.broadcasted_iota(jnp.int32, (1, 3 * A_BLOCK), 1)
    in_window = jnp.abs(kk - A_BLOCK - qq) <= A_BLOCK
    exists = jnp.logical_and(jnp.logical_or(kk >= A_BLOCK, i > 0), jnp.logical_or(kk < 2 * A_BLOCK, i < nb - 1))
    ok_band = jnp.logical_and(in_window, exists)
    for kv in range(A_KV_HEADS):
        ksl = slice(kv * HEAD_DIM, (kv + 1) * HEAD_DIM)
        vsl = slice((A_KV_HEADS + kv) * HEAD_DIM, (A_KV_HEADS + kv + 1) * HEAD_DIM)
        q = jnp.concatenate(
            [q_ref[:, (kv * A_GROUP + g) * HEAD_DIM:(kv * A_GROUP + g + 1) * HEAD_DIM] for g in range(A_GROUP)], axis=0)
        sink = jnp.concatenate(
            [jnp.full((A_BLOCK, 1), sink_ref[sink_base + kv * A_GROUP + g] * LOG2E, F32) for g in range(A_GROUP)],
            axis=0)
        k_band = jnp.concatenate([kvp_ref[:, ksl], kvc_ref[:, ksl], kvn_ref[:, ksl]], axis=0)
        v_band = jnp.concatenate([kvp_ref[:, vsl], kvc_ref[:, vsl], kvn_ref[:, vsl]], axis=0)
        s_b = jnp.where(ok_band, _nt_dot(q, k_band), -jnp.inf)
        s_x = _nt_dot(q, kc_ref[:, ksl].astype(BF))
        (e_b, e_x), inv = _exp2_parts([s_b, s_x], sink)
        o = (_dot(e_b.astype(BF), v_band) + _dot(e_x.astype(BF), vc_ref[:, ksl].astype(BF))) * inv
        for g in range(A_GROUP):
            hq = kv * A_GROUP + g
            o_ref[:, hq * HEAD_DIM:(hq + 1) * HEAD_DIM] = o[g * A_BLOCK:(g + 1) * A_BLOCK].astype(o_ref.dtype)


def window_attention(arr_a, cache_k, cache_v, sinks, e, N, T):
    M = N * T
    nb = T // A_BLOCK
    L = cache_k.shape[2]
    QW = A_Q_HEADS * HEAD_DIM
    KVW = 2 * A_KV_HEADS * HEAD_DIM
    kvblk = QW // KVW
    return pl.pallas_call(
        functools.partial(_win_kernel, sink_base=e * A_Q_HEADS, nb=nb),
        grid=(N, nb),
        in_specs=[
            pl.BlockSpec(memory_space=pltpu.SMEM),
            pl.BlockSpec((A_BLOCK, QW), lambda n, i: (n * nb + i, 0)),
            pl.BlockSpec((A_BLOCK, KVW), lambda n, i: (n * nb + jnp.maximum(i - 1, 0), kvblk)),
            pl.BlockSpec((A_BLOCK, KVW), lambda n, i: (n * nb + i, kvblk)),
            pl.BlockSpec((A_BLOCK, KVW), lambda n, i: (n * nb + jnp.minimum(i + 1, nb - 1), kvblk)),
            pl.BlockSpec((None, None, L, A_KV_HEADS * HEAD_DIM), lambda n, i: (n, e, 0, 0)),
            pl.BlockSpec((None, None, L, A_KV_HEADS * HEAD_DIM), lambda n, i: (n, e, 0, 0)),
        ],
        out_specs=pl.BlockSpec((A_BLOCK, QW), lambda n, i: (n * nb + i, 0)),
        out_shape=jax.ShapeDtypeStruct((M, QW), BF),
        compiler_params=_cparams("parallel", "parallel"),
    )(sinks, arr_a, arr_a, arr_a, arr_a, cache_k, cache_v)


NB_Q_ROWS = TOKEN_BLOCK // GRID_W
NB_WIN_BLOCKS = 3


def _toeplitz_kernel(rpb_ref, o_ref):
    K = rpb_ref.shape[1]
    W2 = GRID_W * GRID_W
    col = lax.broadcasted_iota(jnp.int32, (K, W2), 1)
    cq = col // GRID_W
    ck = col % GRID_W
    dc = jnp.clip(ck - cq + (NA_COLS - 1), 0, 2 * NA_COLS - 2)
    onehot = (lax.broadcasted_iota(jnp.int32, (K, W2), 0) == dc).astype(BF)
    t = rpb_ref[...]
    hi = t.astype(BF)
    r1 = t - hi.astype(F32)
    mid = r1.astype(BF)
    lo = (r1 - mid.astype(F32)).astype(BF)
    val = _dot(hi, onehot) + _dot(mid, onehot) + _dot(lo, onehot)
    c0 = jnp.clip(cq[0:1] - NA_COLS // 2, 0, GRID_W - NA_COLS)
    ok = jnp.logical_and(ck[0:1] >= c0, ck[0:1] < c0 + NA_COLS)
    o_ref[...] = jnp.where(ok, val * LOG2E, -jnp.inf)


def toeplitz_bias(rpb):
    G, H, NR, NC = rpb.shape
    R = G * H * NR
    K = 32
    flat = jnp.pad(rpb.reshape(R, NC), ((0, -R % 8), (0, K - NC)))
    out = pl.pallas_call(
        _toeplitz_kernel,
        out_shape=jax.ShapeDtypeStruct((flat.shape[0], GRID_W * GRID_W), F32),
        compiler_params=pltpu.CompilerParams(vmem_limit_bytes=VMEM_LIMIT_BYTES),
    )(flat)
    return out[:R].reshape(G, H, NR, GRID_W, GRID_W)


def neighbourhood_bias(tiles, T):
    rows = T // GRID_W
    nblk = T // TOKEN_BLOCK
    H = tiles.shape[0]
    masked = jnp.full((H, GRID_W, GRID_W), -jnp.inf, F32)
    out = []
    for blk in (0, 1, nblk - 1):
        w0 = min(max(blk - 1, 0), nblk - NB_WIN_BLOCKS) * NB_Q_ROWS
        qrows = []
        for ql in range(NB_Q_ROWS):
            rq = blk * NB_Q_ROWS + ql
            r0 = min(max(rq - NA_ROWS // 2, 0), rows - NA_ROWS)
            krow = []
            for kl in range(NB_WIN_BLOCKS * NB_Q_ROWS):
                rk = w0 + kl
                krow.append(tiles[:, rk - rq + NA_ROWS - 1] if r0 <= rk < r0 + NA_ROWS else masked)
            qrows.append(jnp.concatenate(krow, axis=-1))
        out.append(jnp.concatenate(qrows, axis=-2))
    return jnp.stack(out, axis=0)


def _nbr_kernel(q_ref, k0_ref, k1_ref, k2_ref, v0_ref, v1_ref, v2_ref, kc_ref, vc_ref, bias_ref, o_ref):
    for h in range(B_HEADS):
        sl = slice(h * HEAD_DIM, (h + 1) * HEAD_DIM)
        q = q_ref[:, sl]
        parts = []
        for w, k_ref in enumerate((k0_ref, k1_ref, k2_ref)):
            b = bias_ref[h, :, w * TOKEN_BLOCK:(w + 1) * TOKEN_BLOCK]
            parts.append(_nt_dot(q, k_ref[:, sl]) + b)
        parts.append(_nt_dot(q, kc_ref[:, sl].astype(BF)))
        (e0, e1, e2, ex), inv = _exp2_parts(parts)
        o = (_dot(e0.astype(BF), v0_ref[:, sl]) + _dot(e1.astype(BF), v1_ref[:, sl])
             + _dot(e2.astype(BF), v2_ref[:, sl]) + _dot(ex.astype(BF), vc_ref[:, sl].astype(BF))) * inv
        o_ref[:, sl] = o.astype(o_ref.dtype)


def neighbourhood_attention(qb, kb, vb, cache_k, cache_v, bias3, e, N, T):
    M = N * T
    nblk = T // TOKEN_BLOCK
    L = cache_k.shape[2]

    def win(w):
        return lambda n, i: (n * nblk + jnp.clip(i - 1, 0, nblk - NB_WIN_BLOCKS) + w, 0)

    def case(n, i):
        return (jnp.where(i == 0, 0, jnp.where(i == nblk - 1, 2, 1)), 0, 0, 0)

    blk = pl.BlockSpec((TOKEN_BLOCK, B_WIDTH), lambda n, i: (n * nblk + i, 0))
    return pl.pallas_call(
        _nbr_kernel,
        grid=(N, nblk),
        in_specs=[
            blk,
            pl.BlockSpec((TOKEN_BLOCK, B_WIDTH), win(0)),
            pl.BlockSpec((TOKEN_BLOCK, B_WIDTH), win(1)),
            pl.BlockSpec((TOKEN_BLOCK, B_WIDTH), win(2)),
            pl.BlockSpec((TOKEN_BLOCK, B_WIDTH), win(0)),
            pl.BlockSpec((TOKEN_BLOCK, B_WIDTH), win(1)),
            pl.BlockSpec((TOKEN_BLOCK, B_WIDTH), win(2)),
            pl.BlockSpec((None, None, L, B_WIDTH), lambda n, i: (n, e, 0, 0)),
            pl.BlockSpec((None, None, L, B_WIDTH), lambda n, i: (n, e, 0, 0)),
            pl.BlockSpec((None, B_HEADS, TOKEN_BLOCK, NB_WIN_BLOCKS * TOKEN_BLOCK), case),
        ],
        out_specs=blk,
        out_shape=jax.ShapeDtypeStruct((M, B_WIDTH), BF),
        compiler_params=_cparams("parallel", "arbitrary"),
    )(qb, kb, kb, kb, vb, vb, vb, cache_k, cache_v, bias3)


def _diff_dense_kernel(lam_ref, g_ref, q_ref, k_ref, v_ref, o_ref, *, lambda_init):
    lam = _diff_lambda(lam_ref, lambda_init)
    W = C_V_DIM
    for h in range(C_HEADS):
        nums, invs = [], []
        for half in range(2):
            sl = slice(h * W + half * HEAD_DIM, h * W + (half + 1) * HEAD_DIM)
            q = (q_ref[:, sl] * Q_PRESCALE).astype(BF)
            (e,), inv = _exp2_parts([_nt_dot(q, k_ref[:, sl].astype(BF))])
            nums.append(e)
            invs.append(inv)
        a = (nums[0] * invs[0] - nums[1] * (lam * invs[1])).astype(BF)
        o = _dot(a, v_ref[:, h * W:(h + 1) * W].astype(BF))
        of = o * lax.rsqrt(jnp.mean(o * o, -1, keepdims=True) + RMS_EPS)
        o_ref[:, h * W:(h + 1) * W] = ((of * g_ref[...]) * (1.0 - lambda_init)).astype(o_ref.dtype)


def _diff_lambda(lam_ref, lambda_init):
    lp = lam_ref[...]
    return (jnp.exp(jnp.sum(lp[0:1] * lp[1:2], -1, keepdims=True))
            - jnp.exp(jnp.sum(lp[2:3] * lp[3:4], -1, keepdims=True)) + lambda_init)


def _diff_pipe_kernel(lam_ref, g_ref, q_ref, k_ref, v_ref, kc_ref, vc_ref, o_ref,
                      el0, el1, ec0, ec1, cf0, cf1, *, lambda_init):
    i = pl.program_id(2)
    tq = q_ref.shape[0]

    @pl.when(i == 0)
    def _():
        for ref in (el0, el1, ec0, ec1, cf0, cf1):
            ref[...] = jnp.zeros_like(ref)

    def step(e_lat, e_ctx, coef, p_lat, p_ctx, p_coef):
        lam = _diff_lambda(lam_ref, lambda_init)

        def numerators(half):
            sl = slice(half * HEAD_DIM, (half + 1) * HEAD_DIM)
            q = q_ref[:, sl]
            (el, ec), inv = _exp2_parts([_nt_dot(q, k_ref[:, sl]), _nt_dot(q, kc_ref[:, sl].astype(BF))])
            e_lat[half] = el
            e_ctx[half] = ec
            coef[half] = jnp.broadcast_to(inv * lam if half else inv, (tq, HEAD_DIM))

        numerators(0)
        c0 = p_coef[0][:, 0:1]
        c1 = p_coef[1][:, 0:1]
        a = (p_lat[0] * c0 - p_lat[1] * c1).astype(BF)
        ac = (p_ctx[0] * c0 - p_ctx[1] * c1).astype(BF)
        o = _dot(a, v_ref[...]) + _dot(ac, vc_ref[...].astype(BF))
        of = o * lax.rsqrt(jnp.mean(o * o, -1, keepdims=True) + RMS_EPS)
        o_ref[...] = ((of * g_ref[...]) * (1.0 - lambda_init)).astype(o_ref.dtype)
        numerators(1)

    @pl.when(i % 2 == 0)
    def _():
        step(el0, ec0, cf0, el1, ec1, cf1)

    @pl.when(i % 2 == 1)
    def _():
        step(el1, ec1, cf1, el0, ec0, cf0)


def diff_attention_pipelined(q, k, v, lam_p, subln, lambda_init, N, T, ctx, widx):
    M = N * T
    tq = 256
    nq = T // tq
    W = C_V_DIM
    L = ctx[0].shape[2]
    return pl.pallas_call(
        functools.partial(_diff_pipe_kernel, lambda_init=lambda_init),
        grid=(N, C_HEADS, nq + 1),
        in_specs=[
            pl.BlockSpec((None, 4, HEAD_DIM), lambda n, h, i: (widx, 0, 0)),
            pl.BlockSpec((None, 1, W), lambda n, h, i: (widx, 0, 0)),
            pl.BlockSpec((tq, W), lambda n, h, i: (n * nq + jnp.minimum(i, nq - 1), h)),
            pl.BlockSpec((T, W), lambda n, h, i: (n, h)),
            pl.BlockSpec((T, W), lambda n, h, i: (n, h)),
            pl.BlockSpec((None, None, L, W), lambda n, h, i: (n, widx, 0, h)),
            pl.BlockSpec((None, None, L, W), lambda n, h, i: (n, widx, 0, h)),
        ],
        out_specs=pl.BlockSpec((tq, W), lambda n, h, i: (n * nq + jnp.maximum(i - 1, 0), h)),
        out_shape=jax.ShapeDtypeStruct((M, C_HEADS * W), BF),
        scratch_shapes=[pltpu.VMEM((2, tq, T), F32)] * 2 + [pltpu.VMEM((2, tq, L), F32)] * 2
        + [pltpu.VMEM((2, tq, HEAD_DIM), F32)] * 2,
        compiler_params=_cparams("parallel", "parallel", "arbitrary"),
    )(lam_p, subln.reshape(subln.shape[0], 1, W), q, k, v, *ctx)


def diff_attention_dense(q, k, v, lam_p, subln, lambda_init, N, T, widx):
    W = C_V_DIM
    row = pl.BlockSpec((T, C_HEADS * W), lambda n: (n, 0))
    return pl.pallas_call(
        functools.partial(_diff_dense_kernel, lambda_init=lambda_init),
        grid=(N,),
        in_specs=[
            pl.BlockSpec((None, 4, HEAD_DIM), lambda n: (widx, 0, 0)),
            pl.BlockSpec((None, 1, W), lambda n: (widx, 0, 0)),
            row, row, row,
        ],
        out_specs=row,
        out_shape=jax.ShapeDtypeStruct((N * T, C_HEADS * W), BF),
        compiler_params=_cparams("parallel"),
    )(lam_p, subln.reshape(subln.shape[0], 1, W), q, k, v)


def _outproj_kernel(*refs, n_in, alpha):
    o_refs = refs[:n_in]
    w_refs = refs[n_in:2 * n_in]
    x_ref, m_ref, g_ref, b_ref, wr_ref, xo_ref, h_ref, aff_ref = refs[2 * n_in:]
    y = None
    for o_r, w_r in zip(o_refs, w_refs):
        t = _dot(o_r[...], w_r[...])
        y = t if y is None else y + t
    z = alpha * x_ref[...] + m_ref[2:3, :] * y
    xn = _layer_norm(z, g_ref[...], b_ref[...])
    xo_ref[...] = xn
    hb = (xn * (1.0 + m_ref[4:5, :]) + m_ref[3:4, :]).astype(BF)
    h_ref[...] = hb
    logits = _dot(hb, wr_ref[...])
    mx = jnp.max(logits, -1, keepdims=True)
    ex = jnp.exp(logits - mx)
    aff_ref[...] = ex / jnp.sum(ex, -1, keepdims=True)


def outproj_ln_router(os_, w_out, widx, x, mods, layer, row0, per_req, T, ln_g, ln_b, w_router, alpha):
    M, D = x.shape
    tm = min(T, 256)
    E = w_router.shape[-1]
    in_specs, args, koff = [], [], 0
    for o in os_:
        in_specs.append(pl.BlockSpec((tm, o.shape[1]), lambda i: (i, 0)))
        args.append(o)
    for o in os_:
        wk = o.shape[1]
        in_specs.append(pl.BlockSpec((None, wk, D), functools.partial(lambda i, kb: (widx, kb, 0), kb=koff // wk)))
        args.append(w_out)
        koff += wk
    in_specs += [
        pl.BlockSpec((tm, D), lambda i: (i, 0)),
        pl.BlockSpec((None, None, 6, D), lambda i: (layer, row0 + ((i * tm) // T) * per_req, 0, 0)),
        pl.BlockSpec((None, 1, D), lambda i: (layer, 0, 0)),
        pl.BlockSpec((None, 1, D), lambda i: (layer, 0, 0)),
        pl.BlockSpec((None, D, E), lambda i: (layer, 0, 0)),
    ]
    args += [x, mods, ln_g, ln_b, w_router]
    return pl.pallas_call(
        functools.partial(_outproj_kernel, n_in=len(os_), alpha=alpha),
        grid=(M // tm,),
        in_specs=in_specs,
        out_specs=[
            pl.BlockSpec((tm, D), lambda i: (i, 0)),
            pl.BlockSpec((tm, D), lambda i: (i, 0)),
            pl.BlockSpec((tm, E), lambda i: (i, 0)),
        ],
        out_shape=[
            jax.ShapeDtypeStruct((M, D), F32),
            jax.ShapeDtypeStruct((M, D), BF),
            jax.ShapeDtypeStruct((M, E), F32),
        ],
        compiler_params=_cparams("parallel"),
    )(*args)


def _route_kernel(aff_ref, scol_ref, srow_ref, cnt_ref, a3_ref, *, cap, nblk):
    E = N_EXPERTS
    B = aff_ref.shape[0] // nblk
    a = aff_ref[...]
    bits = pltpu.bitcast(a, jnp.int32)

    def search(it, ans):
        cand = ans | jnp.left_shift(jnp.int32(1), 30 - it)
        cnt = jnp.sum((bits >= cand).astype(F32), axis=0, keepdims=True)
        return jnp.where(cnt >= cap, cand, ans)

    thr = lax.fori_loop(0, 31, search, jnp.zeros((1, E), jnp.int32))
    gt = bits > thr
    eq = bits == thr
    need = cap - jnp.sum(gt.astype(F32), axis=0, keepdims=True)

    r = lax.broadcasted_iota(jnp.int32, (B, B), 0)
    c = lax.broadcasted_iota(jnp.int32, (B, B), 1)
    lower = (c < r).astype(BF)
    upper = (r < c).astype(BF)
    eye = (lax.broadcasted_iota(jnp.int32, (E, E), 0) == lax.broadcasted_iota(jnp.int32, (E, E), 1)).astype(BF)

    carry = jnp.zeros((1, E), F32)
    sels = []
    for b in range(nblk):
        sl = slice(b * B, (b + 1) * B)
        eqb = eq[sl].astype(BF)
        pref = _dot(lower, eqb) + carry
        carry = carry + jnp.sum(eqb.astype(F32), axis=0, keepdims=True)
        sels.append(jnp.logical_or(gt[sl], jnp.logical_and(eq[sl], pref < need)))

    carry = jnp.zeros((1, E), F32)
    carry_t = jnp.zeros((E, 1), F32)
    for b in range(nblk):
        sl = slice(b * B, (b + 1) * B)
        selb = sels[b].astype(BF)
        cnt_ref[b:b + 1, :] = carry.astype(jnp.int32)
        pos = _dot(lower, selb) + carry
        scol_ref[sl, :] = jnp.where(sels[b], pos, -1.0)
        carry = carry + jnp.sum(selb.astype(F32), axis=0, keepdims=True)
        sel_t = _nt_dot(eye, selb)
        pos_t = _dot(sel_t.astype(BF), upper) + carry_t
        srow_ref[b] = jnp.where(sel_t > 0.5, pos_t, -1.0)
        carry_t = carry_t + jnp.sum(sel_t, axis=1, keepdims=True)
    cnt_ref[nblk:nblk + 1, :] = carry.astype(jnp.int32)

    hi = a.astype(BF)
    r1 = a - hi.astype(F32)
    mid = r1.astype(BF)
    lo = (r1 - mid.astype(F32)).astype(BF)
    lane = lax.broadcasted_iota(jnp.int32, (E, HEAD_DIM), 1)
    row = lax.broadcasted_iota(jnp.int32, (E, HEAD_DIM), 0)
    a3 = (_dot(hi, (lane == row).astype(BF)) + _dot(mid, (lane == row + E).astype(BF))
          + _dot(lo, (lane == row + 2 * E).astype(BF)))
    a3_ref[...] = a3.astype(BF)


def _gate_from_split(g3, expert):
    lane = lax.broadcasted_iota(jnp.int32, g3.shape, 1)
    pick = jnp.logical_and(lane % N_EXPERTS == expert, lane < 3 * N_EXPERTS)
    return jnp.sum(jnp.where(pick, g3, 0.0), axis=1, keepdims=True)


def route(aff, N, T):
    E = N_EXPERTS
    cap = EC_CAPACITY_FACTOR * T // E
    B = min(T, TOKEN_BLOCK)
    nblk = T // B
    return pl.pallas_call(
        functools.partial(_route_kernel, cap=cap, nblk=nblk),
        grid=(N,),
        in_specs=[pl.BlockSpec((T, E), lambda n: (n, 0))],
        out_specs=[
            pl.BlockSpec((None, T, E), lambda n: (n, 0, 0)),
            pl.BlockSpec((None, nblk, E, B), lambda n: (n, 0, 0, 0)),
            pl.BlockSpec((None, nblk + 1, E), lambda n: (n, 0, 0)),
            pl.BlockSpec((None, T, HEAD_DIM), lambda n: (n, 0, 0)),
        ],
        out_shape=[
            jax.ShapeDtypeStruct((N, T, E), F32),
            jax.ShapeDtypeStruct((N, nblk, E, B), F32),
            jax.ShapeDtypeStruct((N, nblk + 1, E), jnp.int32),
            jax.ShapeDtypeStruct((N, T, HEAD_DIM), BF),
        ],
        compiler_params=_cparams("parallel"),
    )(aff)


def _gather_small_kernel(h_ref, srow_ref, a3_ref, x_all_ref, g_all_ref, x_ref, g_ref, *, cap):
    del x_all_ref, g_all_ref
    E = N_EXPERTS
    S = E * cap
    T = h_ref.shape[0]
    sr = srow_ref[...].astype(BF)
    rep_t = (lax.broadcasted_iota(jnp.int32, (S, E), 0) // cap == lax.broadcasted_iota(jnp.int32, (S, E), 1)).astype(BF)
    slot_of_row = (lax.broadcasted_iota(jnp.int32, (S, 1), 0) % cap).astype(F32)
    onehot = (_dot(rep_t, sr) == slot_of_row).astype(BF)
    x = _dot(onehot, h_ref[...])
    for e in range(E):
        x_ref[e] = x[e * cap:(e + 1) * cap].astype(x_ref.dtype)
    gate = _gate_from_split(_dot(onehot, a3_ref[...]), lax.broadcasted_iota(jnp.int32, (S, 1), 0) // cap)
    gb = jnp.broadcast_to(gate, (S, HEAD_DIM))
    for e in range(E):
        g_ref[e] = gb[e * cap:(e + 1) * cap]


def gather_small(h, srow, a3, x_all, g_all, row0, N, T):
    E = N_EXPERTS
    cap = EC_CAPACITY_FACTOR * T // E
    D = h.shape[1]
    assert row0 % cap == 0
    rb0 = row0 // cap
    return pl.pallas_call(
        functools.partial(_gather_small_kernel, cap=cap),
        grid=(N,),
        in_specs=[
            pl.BlockSpec((T, D), lambda n: (n, 0)),
            pl.BlockSpec((None, None, E, T), lambda n: (n, 0, 0, 0)),
            pl.BlockSpec((None, T, HEAD_DIM), lambda n: (n, 0, 0)),
            pl.BlockSpec(memory_space=pl.ANY),
            pl.BlockSpec(memory_space=pl.ANY),
        ],
        out_specs=[
            pl.BlockSpec((E, cap, D), lambda n: (0, rb0 + n, 0)),
            pl.BlockSpec((E, cap, HEAD_DIM), lambda n: (0, rb0 + n, 0)),
        ],
        out_shape=[
            jax.ShapeDtypeStruct(x_all.shape, x_all.dtype),
            jax.ShapeDtypeStruct(g_all.shape, g_all.dtype),
        ],
        input_output_aliases={3: 0, 4: 1},
        compiler_params=_cparams("parallel"),
    )(h, srow, a3, x_all, g_all)


GATHER_COLS = 1024


def _gather_big_kernel(cnt_ref, h_ref, srow_ref, a3_ref, x_init_ref, g_init_ref, x_ref, g_ref, *, cap, nblk):
    del x_init_ref, g_init_ref
    E = N_EXPERTS
    W = min(SLOT_CHUNK, cap)
    n = pl.program_id(0)
    d = pl.program_id(1)
    b = pl.program_id(2)

    @pl.when(b == 0)
    def _():
        x_ref[...] = jnp.zeros_like(x_ref)

    @pl.when(jnp.logical_and(b == 0, d == 0))
    def _():
        g_ref[...] = jnp.zeros_like(g_ref)

    row = lax.broadcasted_iota(jnp.int32, (W, 1), 0)
    starts, ends, pieces = [], [], []
    for e in range(E):
        base = (n * E + e) * (nblk + 1) + b
        ws = pl.multiple_of(jnp.minimum((cnt_ref[base] // 16) * 16, cap - W), 16)
        pieces.append((srow_ref[e:e + 1, :] == (ws + row).astype(F32)).astype(BF))
        starts.append(ws)
        ends.append(cnt_ref[base + 1])
    onehot = jnp.concatenate(pieces, axis=0)
    prod = _dot(onehot, h_ref[...])
    for e in range(E):
        x_ref[e, pl.ds(starts[e], W), :] += prod[e * W:(e + 1) * W].astype(x_ref.dtype)

    @pl.when(d == 0)
    def _():
        split = _dot(onehot, a3_ref[...])
        for e in range(E):
            g_ref[e, pl.ds(starts[e], W), :] += split[e * W:(e + 1) * W]

    for e in range(E):
        def more(k, carry, e=e):
            first = starts[e] + k * W
            ws = pl.multiple_of(jnp.minimum(first, cap - W), 16)
            ids = ws + row
            piece = jnp.logical_and(srow_ref[e:e + 1, :] == ids.astype(F32), ids >= first).astype(BF)
            x_ref[e, pl.ds(ws, W), :] += _dot(piece, h_ref[...]).astype(x_ref.dtype)

            @pl.when(d == 0)
            def _():
                g_ref[e, pl.ds(ws, W), :] += _dot(piece, a3_ref[...])

            return carry

        lax.fori_loop(1, (ends[e] - starts[e] + W - 1) // W, more, 0)

    @pl.when(jnp.logical_and(b == nblk - 1, d == 0))
    def _():
        for e in range(E):
            g_ref[e] = jnp.broadcast_to(_gate_from_split(g_ref[e], e), (cap, HEAD_DIM))


def gather_big(h, srow, cnt_flat, a3, rows_total, N, T):
    E = N_EXPERTS
    cap = EC_CAPACITY_FACTOR * T // E
    D = h.shape[1]
    nblk = T // TOKEN_BLOCK
    Dc = min(D, GATHER_COLS)
    assert cap % 16 == 0
    x_init = jnp.zeros((E, rows_total, D), BF)
    g_init = jnp.zeros((E, rows_total, HEAD_DIM), F32)
    return pl.pallas_call(
        functools.partial(_gather_big_kernel, cap=cap, nblk=nblk),
        grid_spec=pltpu.PrefetchScalarGridSpec(
            num_scalar_prefetch=1,
            grid=(N, D // Dc, nblk),
            in_specs=[
                pl.BlockSpec((TOKEN_BLOCK, Dc), lambda n, d, b, c: (n * nblk + b, d)),
                pl.BlockSpec((None, None, E, TOKEN_BLOCK), lambda n, d, b, c: (n, b, 0, 0)),
                pl.BlockSpec((None, TOKEN_BLOCK, HEAD_DIM), lambda n, d, b, c: (n, b, 0)),
                pl.BlockSpec(memory_space=pl.ANY),
                pl.BlockSpec(memory_space=pl.ANY),
            ],
            out_specs=[
                pl.BlockSpec((E, cap, Dc), lambda n, d, b, c: (0, n, d)),
                pl.BlockSpec((E, cap, HEAD_DIM), lambda n, d, b, c: (0, n, 0)),
            ],
        ),
        out_shape=[
            jax.ShapeDtypeStruct((E, rows_total, D), BF),
            jax.ShapeDtypeStruct((E, rows_total, HEAD_DIM), F32),
        ],
        input_output_aliases={4: 0, 5: 1},
        compiler_params=_cparams("parallel", "arbitrary", "arbitrary"),
    )(cnt_flat, h, srow, a3, x_init, g_init)


def _gate_up_kernel(x_ref, wg_ref, wu_ref, o_ref):
    x = x_ref[...]
    a = _dot(x, wg_ref[...].astype(BF))
    u = _dot(x, wu_ref[...].astype(BF))
    o_ref[...] = ((a * jax.nn.sigmoid(a)) * u).astype(o_ref.dtype)


def expert_gate_up(x, w_gate, w_up, layer):
    E, R, D = x.shape
    F = w_gate.shape[-1]
    tf = min(F, 256)
    return pl.pallas_call(
        _gate_up_kernel,
        grid=(E, F // tf),
        in_specs=[
            pl.BlockSpec((None, R, D), lambda e, f: (e, 0, 0)),
            pl.BlockSpec((None, None, D, tf), lambda e, f: (layer, e, 0, f)),
            pl.BlockSpec((None, None, D, tf), lambda e, f: (layer, e, 0, f)),
        ],
        out_specs=pl.BlockSpec((None, R, tf), lambda e, f: (e, 0, f)),
        out_shape=jax.ShapeDtypeStruct((E, R, F), BF),
        compiler_params=_cparams("parallel", "arbitrary"),
    )(x, w_gate, w_up)


def _down_kernel(h_ref, w_ref, g_ref, o_ref):
    o_ref[...] = (_dot(h_ref[...], w_ref[...].astype(BF)) * g_ref[:, 0:1]).astype(o_ref.dtype)


def expert_down(h, w_down, gate, layer):
    E, R, F = h.shape
    D = w_down.shape[-1]
    td = min(D, 512)
    return pl.pallas_call(
        _down_kernel,
        grid=(E, D // td),
        in_specs=[
            pl.BlockSpec((None, R, F), lambda e, d: (e, 0, 0)),
            pl.BlockSpec((None, None, F, td), lambda e, d: (layer, e, 0, d)),
            pl.BlockSpec((None, R, HEAD_DIM), lambda e, d: (e, 0, 0)),
        ],
        out_specs=pl.BlockSpec((None, R, td), lambda e, d: (e, 0, d)),
        out_shape=jax.ShapeDtypeStruct((E, R, D), BF),
        compiler_params=_cparams("parallel", "arbitrary"),
    )(h, w_down, gate)


def _residual_epilogue(f, x_ref, m_ref, mn_ref, g_ref, b_ref, xo_ref, h_ref, alpha):
    z = alpha * x_ref[...] + m_ref[5:6, :] * f
    xn = _layer_norm(z, g_ref[...], b_ref[...])
    xo_ref[...] = xn
    h_ref[...] = (xn * (1.0 + mn_ref[1:2, :]) + mn_ref[0:1, :]).astype(h_ref.dtype)


def _scatter_small_kernel(y_ref, scol_ref, x_ref, m_ref, mn_ref, g_ref, b_ref, xo_ref, h_ref, *, cap, alpha):
    E = N_EXPERTS
    S = E * cap
    sc = scol_ref[...].astype(BF)
    rep = (lax.broadcasted_iota(jnp.int32, (E, S), 0) == lax.broadcasted_iota(jnp.int32, (E, S), 1) // cap).astype(BF)
    slot_of_col = (lax.broadcasted_iota(jnp.int32, (1, S), 1) % cap).astype(F32)
    onehot_t = (_dot(sc, rep) == slot_of_col).astype(BF)
    y = jnp.concatenate([y_ref[e] for e in range(E)], axis=0)
    f = _dot(onehot_t, y)
    _residual_epilogue(f, x_ref, m_ref, mn_ref, g_ref, b_ref, xo_ref, h_ref, alpha)


def scatter_small(y, y_row0, scol, x, mods, layer, next_layer, row0, per_req, ln_g, ln_b, N, T, alpha):
    E = N_EXPERTS
    cap = EC_CAPACITY_FACTOR * T // E
    D = x.shape[1]
    assert y_row0 % cap == 0
    return pl.pallas_call(
        functools.partial(_scatter_small_kernel, cap=cap, alpha=alpha),
        grid=(N,),
        in_specs=[
            pl.BlockSpec((E, cap, D), lambda n: (0, y_row0 // cap + n, 0)),
            pl.BlockSpec((None, T, E), lambda n: (n, 0, 0)),
            pl.BlockSpec((T, D), lambda n: (n, 0)),
            pl.BlockSpec((None, None, 6, D), lambda n: (layer, row0 + n * per_req, 0, 0)),
            pl.BlockSpec((None, None, 6, D), lambda n: (next_layer, row0 + n * per_req, 0, 0)),
            pl.BlockSpec((None, 1, D), lambda n: (layer, 0, 0)),
            pl.BlockSpec((None, 1, D), lambda n: (layer, 0, 0)),
        ],
        out_specs=[pl.BlockSpec((T, D), lambda n: (n, 0)), pl.BlockSpec((T, D), lambda n: (n, 0))],
        out_shape=[jax.ShapeDtypeStruct((N * T, D), F32), jax.ShapeDtypeStruct((N * T, D), BF)],
        compiler_params=_cparams("parallel"),
    )(y, scol, x, mods, mods, ln_g, ln_b)


def _scatter_big_kernel(cnt_ref, y_ref, scol_ref, x_ref, m_ref, mn_ref, g_ref, b_ref, xo_ref, h_ref,
                        *, nblk, tiles, alpha):
    acc = xo_ref
    E = N_EXPERTS
    C = SLOT_CHUNK
    B = TOKEN_BLOCK
    G = y_ref.shape[0]
    cap = y_ref.shape[1]
    WIN = min(C, cap)
    n = pl.program_id(0)
    tb = pl.program_id(1)
    g = pl.program_id(2)

    @pl.when(g == 0)
    def _():
        acc[...] = jnp.zeros_like(acc)

    lane = lax.broadcasted_iota(jnp.int32, (B, E), 1)
    for bl in range(tiles):
        blk = tb * tiles + bl
        sc = scol_ref[bl * B:(bl + 1) * B, :]
        hits, wins = [], []
        for gi in range(G):
            e = g * G + gi
            base = (n * E + e) * (nblk + 1)
            lo = cnt_ref[base + blk]
            hi = cnt_ref[base + blk + 1]
            scol = jnp.sum(jnp.where(lane == e, sc, 0.0), axis=1, keepdims=True)
            ws = pl.multiple_of(jnp.minimum((lo // 16) * 16, cap - WIN), 16)
            col = lax.broadcasted_iota(jnp.int32, (1, WIN), 1)
            hits.append((scol == (ws + col).astype(F32)).astype(BF))
            wins.append(y_ref[gi, pl.ds(ws, WIN), :])

            def more(k, carry, gi=gi, scol=scol, ws=ws, col=col):
                first = ws + k * WIN
                ks = pl.multiple_of(jnp.minimum(first, cap - WIN), 16)
                ids = ks + col
                hit = jnp.logical_and(scol == ids.astype(F32), ids >= first)
                acc[bl * B:(bl + 1) * B, :] += _dot(hit.astype(BF), y_ref[gi, pl.ds(ks, WIN), :])
                return carry

            lax.fori_loop(1, (hi - ws + WIN - 1) // WIN, more, 0)
        f = None
        for p in range(0, G, 2):
            d = _dot(jnp.concatenate(hits[p:p + 2], axis=1), jnp.concatenate(wins[p:p + 2], axis=0))
            f = d if f is None else f + d
        acc[bl * B:(bl + 1) * B, :] += f

    @pl.when(g == pl.num_programs(2) - 1)
    def _():
        for r in range(0, tiles * B, B):
            rows = slice(r, r + B)
            z = alpha * x_ref[rows, :] + m_ref[5:6, :] * acc[rows, :]
            xn = _layer_norm(z, g_ref[...], b_ref[...])
            xo_ref[rows, :] = xn
            h_ref[rows, :] = (xn * (1.0 + mn_ref[1:2, :]) + mn_ref[0:1, :]).astype(h_ref.dtype)


SCATTER_EXPERT_GROUP = 2


def scatter_big(y, y_row0, scol, cnt_flat, x, mods, layer, next_layer, row0, per_req, ln_g, ln_b, N, T, alpha):
    E = N_EXPERTS
    cap = EC_CAPACITY_FACTOR * T // E
    D = x.shape[1]
    nblk = T // TOKEN_BLOCK
    tiles = max(t for t in (1, 2, 4) if nblk % t == 0)
    tt = tiles * TOKEN_BLOCK
    ntt = T // tt
    G = SCATTER_EXPERT_GROUP
    assert y_row0 % cap == 0 and cap % 16 == 0
    return pl.pallas_call(
        functools.partial(_scatter_big_kernel, nblk=nblk, tiles=tiles, alpha=alpha),
        grid_spec=pltpu.PrefetchScalarGridSpec(
            num_scalar_prefetch=1,
            grid=(N, ntt, E // G),
            in_specs=[
                pl.BlockSpec((G, cap, D), lambda n, t, e, c: (e, y_row0 // cap + n, 0)),
                pl.BlockSpec((None, tt, E), lambda n, t, e, c: (n, t, 0)),
                pl.BlockSpec((tt, D), lambda n, t, e, c: (n * ntt + t, 0), pipeline_mode=pl.Buffered(1)),
                pl.BlockSpec((None, None, 6, D), lambda n, t, e, c: (layer, row0 + n * per_req, 0, 0)),
                pl.BlockSpec((None, None, 6, D), lambda n, t, e, c: (next_layer, row0 + n * per_req, 0, 0)),
                pl.BlockSpec((None, 1, D), lambda n, t, e, c: (layer, 0, 0)),
                pl.BlockSpec((None, 1, D), lambda n, t, e, c: (layer, 0, 0)),
            ],
            out_specs=[
                pl.BlockSpec((tt, D), lambda n, t, e, c: (n * ntt + t, 0)),
                pl.BlockSpec((tt, D), lambda n, t, e, c: (n * ntt + t, 0)),
            ],
        ),
        out_shape=[jax.ShapeDtypeStruct((N * T, D), F32), jax.ShapeDtypeStruct((N * T, D), BF)],
        compiler_params=_cparams("parallel", "parallel", "arbitrary"),
    )(cnt_flat, y, scol, x, mods, mods, ln_g, ln_b)


def moe_layer(groups, mods, layer, next_layer, ln_g, ln_b, w_gate, w_up, w_down, alpha):
    E = N_EXPERTS
    rows_total = sum(gr["N"] * (EC_CAPACITY_FACTOR * gr["T"] // E) for gr in groups)
    routed, row0s, rows, x_all, g_all = [], [], 0, None, None
    for gr in groups:
        N, T = gr["N"], gr["T"]
        scol, srow, cnt, a3 = route(gr["aff"], N, T)
        if T <= TOKEN_BLOCK:
            assert x_all is not None
            cnt_flat = None
            x_all, g_all = gather_small(gr["h"], srow, a3, x_all, g_all, rows, N, T)
        else:
            assert x_all is None
            cnt_flat = jnp.transpose(cnt, (0, 2, 1)).reshape(-1)
            x_all, g_all = gather_big(gr["h"], srow, cnt_flat, a3, rows_total, N, T)
        routed.append((scol, cnt_flat))
        row0s.append(rows)
        rows += N * (EC_CAPACITY_FACTOR * T // E)
    hm = expert_gate_up(x_all, w_gate, w_up, layer)
    y = expert_down(hm, w_down, g_all, layer)
    out = []
    for gr, (scol, cnt_flat), y_row0 in zip(groups, routed, row0s):
        tail = (gr["x"], mods, layer, next_layer, gr["row0"], gr["per_req"], ln_g, ln_b, gr["N"], gr["T"], alpha)
        if cnt_flat is None:
            out.append(scatter_small(y, y_row0, scol, *tail))
        else:
            out.append(scatter_big(y, y_row0, scol, cnt_flat, *tail))
    return out


def kernel(x_prompt, x_sample, cache_a_k, cache_a_v, cache_b_k, cache_b_v, cache_c_k, cache_c_v, c, c_ctx, w_mod, b_mod, w_in_even, w_out_even, a_sink, na_rpb, w_in_odd, w_out_odd, diff_lambda, diff_subln, ln_g, ln_b, w_router, w_gate, w_up, w_down):
    NP, TP, D = x_prompt.shape
    NS, TS, _ = x_sample.shape
    depth = w_mod.shape[0]
    L = cache_a_k.shape[2]
    alpha = (2 * depth) ** 0.25
    n_even = (depth + 1) // 2
    n_odd = depth // 2

    rows = 1 + NS
    rpad = -rows % 8
    cond = jnp.concatenate([c_ctx[None], c, jnp.zeros((rpad, D), F32)], axis=0)
    mods = adaln_all(cond, w_mod, b_mod).reshape(depth, rows + rpad, 6, D)

    rope = rope_tables(TS)
    sinks = a_sink.reshape(-1)
    rpb_tiles = toeplitz_bias(na_rpb)
    w_out_even_bf = w_out_even.astype(BF)
    w_out_odd_bf = w_out_odd.astype(BF)
    w_router_bf = w_router.astype(BF)
    ln_g4 = ln_g.reshape(depth, 2, 1, D)
    ln_b4 = ln_b.reshape(depth, 2, 1, D)
    ca_k = cache_a_k.reshape(NS, n_even, L, A_KV_HEADS * HEAD_DIM)
    ca_v = cache_a_v.reshape(NS, n_even, L, A_KV_HEADS * HEAD_DIM)
    cb_k = cache_b_k.reshape(NS, n_even, L, B_WIDTH)
    cb_v = cache_b_v.reshape(NS, n_even, L, B_WIDTH)
    cc_k = cache_c_k.reshape(NS, n_odd, L, C_HEADS * 2 * HEAD_DIM)
    cc_v = cache_c_v.reshape(NS, n_odd, L, C_HEADS * C_V_DIM)

    xp = x_prompt.reshape(NP * TP, D)
    xs = x_sample.reshape(NS * TS, D)
    hp = modulate_rows(xp, mods, 0, 0, 0, TP)
    hs = modulate_rows(xs, mods, 0, 1, 1, TS)

    new_a_k, new_a_v, new_b_k, new_b_v, new_c_k, new_c_v = [], [], [], [], [], []
    QA = A_Q_HEADS * HEAD_DIM
    KA = A_KV_HEADS * HEAD_DIM
    for l in range(depth):
        if l % 2 == 0:
            e = l // 2
            b0 = A_WIDTH
            pa = project(hp, w_in_even, e, 0, A_WIDTH, 0, None, TP, F32)
            pqb = project(hp, w_in_even, e, b0, B_WIDTH, 0, None, TP, F32)
            pkb = project(hp, w_in_even, e, b0 + B_WIDTH, B_WIDTH, 0, None, TP, F32)
            pvb = project(hp, w_in_even, e, b0 + 2 * B_WIDTH, B_WIDTH, 0, None, TP, F32)
            new_a_k.append(pa[:, QA:QA + KA].reshape(NP, TP, A_KV_HEADS, HEAD_DIM))
            new_a_v.append(pa[:, QA + KA:QA + 2 * KA].reshape(NP, TP, A_KV_HEADS, HEAD_DIM))
            new_b_k.append(pkb.reshape(NP, TP, B_HEADS, HEAD_DIM))
            new_b_v.append(pvb.reshape(NP, TP, B_HEADS, HEAD_DIM))
            op = [ctx_even_attention(pa, pqb, pkb, pvb, sinks, e, NP, TP)]

            sa = project(hs, w_in_even, e, 0, A_WIDTH, A_Q_HEADS + A_KV_HEADS, rope, TS, BF, A_Q_HEADS)
            sqb = project(hs, w_in_even, e, b0, B_WIDTH, 0, rope, TS, BF, B_HEADS)
            skb = project(hs, w_in_even, e, b0 + B_WIDTH, B_WIDTH, 0, rope, TS, BF)
            svb = project(hs, w_in_even, e, b0 + 2 * B_WIDTH, B_WIDTH, 0, rope, TS, BF)
            oa = window_attention(sa, ca_k, ca_v, sinks, e, NS, TS)
            bias3 = neighbourhood_bias(rpb_tiles[e], TS)
            ob = neighbourhood_attention(sqb, skb, svb, cb_k, cb_v, bias3, e, NS, TS)
            os_ = [oa, ob]
            w_out, widx = w_out_even_bf, e
        else:
            o = l // 2
            lambda_init = 0.8 - 0.6 * math.exp(-0.3 * l)
            W = C_HEADS * C_V_DIM
            pq = project(hp, w_in_odd, o, 0, W, 0, None, TP, F32)
            pk = project(hp, w_in_odd, o, W, W, 0, None, TP, F32)
            pv = project(hp, w_in_odd, o, 2 * W, W, 0, None, TP, F32)
            new_c_k.append(pk.reshape(NP, TP, C_HEADS, 2, HEAD_DIM))
            new_c_v.append(pv.reshape(NP, TP, C_HEADS, C_V_DIM))
            op = [diff_attention_dense(pq, pk, pv, diff_lambda, diff_subln, lambda_init, NP, TP, o)]

            sq = project(hs, w_in_odd, o, 0, W, 2 * C_HEADS, rope, TS, BF, 2 * C_HEADS)
            sk = project(hs, w_in_odd, o, W, W, 2 * C_HEADS, rope, TS, BF)
            sv = project(hs, w_in_odd, o, 2 * W, W, 0, rope, TS, BF)
            os_ = [diff_attention_pipelined(sq, sk, sv, diff_lambda, diff_subln, lambda_init, NS, TS, (cc_k, cc_v), o)]
            w_out, widx = w_out_odd_bf, o

        lg1, lb1 = ln_g4[:, 0], ln_b4[:, 0]
        lg2, lb2 = ln_g4[:, 1], ln_b4[:, 1]
        xp, hp, affp = outproj_ln_router(op, w_out, widx, xp, mods, l, 0, 0, TP, lg1, lb1, w_router_bf, alpha)
        xs, hs, affs = outproj_ln_router(os_, w_out, widx, xs, mods, l, 1, 1, TS, lg1, lb1, w_router_bf, alpha)
        nl = min(l + 1, depth - 1)
        groups = [dict(aff=affs, h=hs, x=xs, N=NS, T=TS, row0=1, per_req=1),
                  dict(aff=affp, h=hp, x=xp, N=NP, T=TP, row0=0, per_req=0)]
        (xs, hs), (xp, hp) = moe_layer(groups, mods, l, nl, lg2, lb2, w_gate, w_up, w_down, alpha)

    return (xp.reshape(NP, TP, D), xs.reshape(NS, TS, D),
            jnp.stack(new_a_k, axis=1), jnp.stack(new_a_v, axis=1),
            jnp.stack(new_b_k, axis=1), jnp.stack(new_b_v, axis=1),
            jnp.stack(new_c_k, axis=1), jnp.stack(new_c_v, axis=1))
```
